```python
import jax, jax.numpy as jnp
from jax import lax
import numpy as np

D_MODEL = 1024
BATCH = 8
SEQ = 2048
DEPTH = 2

RW_HEADS = 16
RW_HEAD_DIM = 64
RW_WIDTH = RW_HEADS * RW_HEAD_DIM
LORA_DECAY = 64
LORA_AAA = 64
LORA_VRES = 32
LORA_GATE = 160
CONV_WIDTH = 1024
CONV_K = 3
MEM_LEN = 256
MEM_HEADS = 4
MEM_HEAD_DIM = D_MODEL // MEM_HEADS
N_GROUPS = 4
EXPERTS_PER_GROUP = 4
N_EXPERTS = N_GROUPS * EXPERTS_PER_GROUP
TOP_K_IN_GROUP = 2
D_EXPERT = 512

RMS_EPS = 1e-6
LNX_EPS = 64e-5
L2_EPS = 1e-12

RW_COLS = 3 * RW_WIDTH + LORA_DECAY + LORA_AAA + LORA_GATE
CONV_COLS = 3 * CONV_WIDTH
GATE_COLS = 2 * D_MODEL
IN_COLS = RW_COLS + CONV_COLS + GATE_COLS

kernel_name = "hybrid_rwkv7_shortconv_memxattn_hmoe"


def rms_norm(x, g):
    xf = x.astype(jnp.float32)
    y = xf * lax.rsqrt(jnp.mean(xf * xf, axis=-1, keepdims=True) + RMS_EPS)
    return (y * g.astype(jnp.float32)).astype(x.dtype)


def token_shift(z):
    return jnp.pad(z, ((0, 0), (1, 0), (0, 0)))[:, :-1]


def shift_mix(z, mu):
    return z + (token_shift(z) - z) * mu


def wkv7_scan(r, w, k, v, a, b):
    bsz, _, h, n = r.shape
    seqs = tuple(jnp.moveaxis(t, 1, 0) for t in (r, w, k, v, a, b))

    def step(S, inp):
        r_t, w_t, k_t, v_t, a_t, b_t = inp
        sa = jnp.einsum('bhvk,bhk->bhv', S, a_t)
        S = (S * w_t[:, :, None, :] + sa[..., None] * b_t[:, :, None, :]
             + v_t[..., None] * k_t[:, :, None, :])
        y = jnp.einsum('bhvk,bhk->bhv', S, r_t)
        return S, y

    S0 = jnp.zeros((bsz, h, n, n), jnp.float32)
    _, ys = lax.scan(step, S0, seqs)
    return jnp.moveaxis(ys, 0, 1)


def causal_dwconv(u, w):
    seq = u.shape[1]
    up = jnp.pad(u, ((0, 0), (CONV_K - 1, 0), (0, 0)))
    y = w[0] * up[:, 0:seq]
    for j in range(1, CONV_K):
        y = y + w[j] * up[:, j:j + seq]
    return y


def mem_cross_attn(h, mem, gq, gkv, wq, wkv, wo):
    bsz, seq, _ = h.shape
    hn = rms_norm(h, gq)
    mn = rms_norm(mem, gkv)
    q = (hn @ wq).reshape(bsz, seq, MEM_HEADS, MEM_HEAD_DIM)
    k, v = jnp.split(mn @ wkv, 2, axis=-1)
    k = k.reshape(bsz, -1, MEM_HEADS, MEM_HEAD_DIM)
    v = v.reshape(bsz, -1, MEM_HEADS, MEM_HEAD_DIM)
    s = jnp.einsum('bqhd,bmhd->bhqm', q, k).astype(jnp.float32) * (MEM_HEAD_DIM ** -0.5)
    p = jax.nn.softmax(s, axis=-1).astype(v.dtype)
    o = jnp.einsum('bhqm,bmhd->bqhd', p, v).reshape(bsz, seq, MEM_HEADS * MEM_HEAD_DIM)
    return o @ wo


def hier_moe(h, rg, rgb, re, reb, wg, wu, wd):
    bsz, seq, d = h.shape
    n = h.reshape(bsz * seq, d)
    gp = jax.nn.softmax((n @ rg + rgb).astype(jnp.float32), axis=-1)
    g_w, g_i = lax.top_k(gp, 1)
    el = (n @ re + reb).astype(jnp.float32).reshape(-1, N_GROUPS, EXPERTS_PER_GROUP)
    el_sel = jnp.sum(el * jax.nn.one_hot(g_i[:, 0], N_GROUPS, dtype=jnp.float32)[:, :, None], axis=1)
    ep = jax.nn.softmax(el_sel, axis=-1)
    e_p, e_i = lax.top_k(ep, TOP_K_IN_GROUP)
    e_p = e_p / jnp.sum(e_p, axis=-1, keepdims=True)
    comb_w = g_w * e_p
    e_glob = g_i * EXPERTS_PER_GROUP + e_i
    combine = jnp.einsum('nk,nke->ne', comb_w,
                         jax.nn.one_hot(e_glob, N_EXPERTS, dtype=jnp.float32)).astype(n.dtype)
    out = jnp.zeros_like(n)
    for e in range(N_EXPERTS):
        he = jax.nn.silu(n @ wg[e]) * (n @ wu[e])
        out = out + combine[:, e:e + 1] * (he @ wd[e])
    return out.reshape(bsz, seq, d)


def setup_inputs(seed: int = 0) -> dict:
    key = jax.random.key(seed)
    ks = iter(jax.random.split(key, 64))
    L, D = DEPTH, D_MODEL

    def nrm(shape, scale):
        return jax.random.normal(next(ks), shape, jnp.float32) * scale

    def uni(shape, lo, hi):
        return jax.random.uniform(next(ks), shape, jnp.float32, lo, hi)

    def gain(shape):
        return 1.0 + nrm(shape, 0.02)

    return {
        "x": nrm((BATCH, SEQ, D), 1.0),
        "mem": nrm((BATCH, MEM_LEN, D), 1.0),
        "norm_mix": gain((L, D)),
        "w_in": nrm((L, D, IN_COLS), D ** -0.5),
        "mu_in": uni((L, RW_COLS), 0.0, 1.0),
        "w_vres_in": nrm((L - 1, D, LORA_VRES), D ** -0.5),
        "mu_vres": uni((L - 1, LORA_VRES), 0.0, 1.0),
        "w0": uni((L, RW_WIDTH), -6.0, -1.0),
        "w_decay_up": nrm((L, LORA_DECAY, RW_WIDTH), 0.1 * LORA_DECAY ** -0.5),
        "a0": nrm((L, RW_WIDTH), 0.5),
        "w_aaa_up": nrm((L, LORA_AAA, RW_WIDTH), LORA_AAA ** -0.5),
        "w_gate_up": nrm((L, LORA_GATE, RW_WIDTH), LORA_GATE ** -0.5),
        "v0": nrm((L - 1, RW_WIDTH), 0.5),
        "w_vres_up": nrm((L - 1, LORA_VRES, RW_WIDTH), LORA_VRES ** -0.5),
        "k_k": 0.85 + nrm((L, RW_WIDTH), 0.05),
        "k_a": 1.0 + nrm((L, RW_WIDTH), 0.05),
        "r_k": nrm((L, RW_HEADS, RW_HEAD_DIM), 0.1),
        "lnx_g": gain((L, RW_WIDTH)),
        "lnx_b": nrm((L, RW_WIDTH), 0.02),
        "conv_w": nrm((L, CONV_K, CONV_WIDTH), CONV_K ** -0.5),
        "gate_b": nrm((L, GATE_COLS), 0.1),
        "w_br_a": nrm((L, RW_WIDTH, D), RW_WIDTH ** -0.5),
        "w_br_b": nrm((L, CONV_WIDTH, D), CONV_WIDTH ** -0.5),
        "w_mix_out": nrm((L, D, D), D ** -0.5),
        "norm_memq": gain((L, D)),
        "norm_memkv": gain((L, D)),
        "wq_mem": nrm((L, D, MEM_HEADS * MEM_HEAD_DIM), D ** -0.5),
        "wkv_mem": nrm((L, D, 2 * MEM_HEADS * MEM_HEAD_DIM), D ** -0.5),
        "wo_mem": nrm((L, MEM_HEADS * MEM_HEAD_DIM, D), (MEM_HEADS * MEM_HEAD_DIM) ** -0.5),
        "norm_ffn": gain((L, D)),
        "router_g": nrm((L, D, N_GROUPS), D ** -0.5),
        "router_g_b": nrm((L, N_GROUPS), 0.01),
        "router_e": nrm((L, D, N_EXPERTS), D ** -0.5),
        "router_e_b": nrm((L, N_EXPERTS), 0.01),
        "w_e_gate": nrm((L, N_EXPERTS, D, D_EXPERT), D ** -0.5),
        "w_e_up": nrm((L, N_EXPERTS, D, D_EXPERT), D ** -0.5),
        "w_e_down": nrm((L, N_EXPERTS, D_EXPERT, D), D_EXPERT ** -0.5),
        "norm_f": gain((D,)),
    }


def reference(x, mem, norm_mix, w_in, mu_in, w_vres_in, mu_vres, w0, w_decay_up, a0,
              w_aaa_up, w_gate_up, v0, w_vres_up, k_k, k_a, r_k, lnx_g, lnx_b, conv_w,
              gate_b, w_br_a, w_br_b, w_mix_out, norm_memq, norm_memkv, wq_mem, wkv_mem,
              wo_mem, norm_ffn, router_g, router_g_b, router_e, router_e_b, w_e_gate,
              w_e_up, w_e_down, norm_f):
    bsz, seq, _ = x.shape
    f32 = jnp.float32

    def heads(z):
        return z.reshape(bsz, seq, RW_HEADS, RW_HEAD_DIM).astype(f32)

    v_first = None
    for l in range(DEPTH):
        xn = rms_norm(x, norm_mix[l])
        w_proj = w_in[l] if l == 0 else jnp.concatenate([w_in[l], w_vres_in[l - 1]], axis=1)
        p = xn @ w_proj
        rw = shift_mix(p[..., :RW_COLS], mu_in[l])
        conv_cols = p[..., RW_COLS:RW_COLS + CONV_COLS]
        gate_cols = p[..., RW_COLS + CONV_COLS:IN_COLS]

        o = 0
        r = rw[..., o:o + RW_WIDTH]; o += RW_WIDTH
        k = rw[..., o:o + RW_WIDTH]; o += RW_WIDTH
        v = rw[..., o:o + RW_WIDTH]; o += RW_WIDTH
        wd = rw[..., o:o + LORA_DECAY]; o += LORA_DECAY
        ad = rw[..., o:o + LORA_AAA]; o += LORA_AAA
        gd = rw[..., o:o + LORA_GATE]

        wlog = -jax.nn.softplus(-(w0[l] + jnp.tanh(wd) @ w_decay_up[l]).astype(f32)) - 0.5
        decay = jnp.exp(-jnp.exp(wlog))
        aa = jax.nn.sigmoid(a0[l] + ad @ w_aaa_up[l])
        og = jax.nn.sigmoid(gd) @ w_gate_up[l]
        if l == 0:
            v_first = v
        else:
            vd = shift_mix(p[..., IN_COLS:], mu_vres[l - 1])
            v = v + (v_first - v) * jax.nn.sigmoid(v0[l - 1] + vd @ w_vres_up[l - 1])
        kk = heads(k * k_k[l])
        kk = kk / jnp.maximum(jnp.sqrt(jnp.sum(kk * kk, axis=-1, keepdims=True)), L2_EPS)
        k = k * (1.0 + (aa - 1.0) * k_a[l])
        rh, kh, vh = heads(r), heads(k), heads(v)
        y = wkv7_scan(rh, heads(decay), kh, vh, -kk, kk * heads(aa))
        y_mu = jnp.mean(y, axis=-1, keepdims=True)
        y_var = jnp.mean(jnp.square(y - y_mu), axis=-1, keepdims=True)
        y = ((y - y_mu) * lax.rsqrt(y_var + LNX_EPS)).reshape(bsz, seq, RW_WIDTH)
        y = y * lnx_g[l] + lnx_b[l]
        bonus = jnp.sum(rh * kh * r_k[l], axis=-1, keepdims=True) * vh
        y_a = ((y + bonus.reshape(bsz, seq, RW_WIDTH)) * og).astype(x.dtype)

        cx = conv_cols[..., :CONV_WIDTH]
        cb = conv_cols[..., CONV_WIDTH:2 * CONV_WIDTH]
        cc = conv_cols[..., 2 * CONV_WIDTH:]
        y_b = cb * causal_dwconv(cc * cx, conv_w[l])

        gts = gate_cols + gate_b[l]
        g_a = jax.nn.sigmoid(gts[..., :D_MODEL])
        g_b = jax.nn.sigmoid(gts[..., D_MODEL:])
        merged = g_a * (y_a @ w_br_a[l]) + g_b * (y_b @ w_br_b[l])
        x = x + merged @ w_mix_out[l]

        x = x + mem_cross_attn(x, mem, norm_memq[l], norm_memkv[l], wq_mem[l], wkv_mem[l], wo_mem[l])

        hn = rms_norm(x, norm_ffn[l])
        x = x + hier_moe(hn, router_g[l], router_g_b[l], router_e[l], router_e_b[l],
                         w_e_gate[l], w_e_up[l], w_e_down[l])

    return rms_norm(x, norm_f)
```

```python
import functools

import jax
import jax.numpy as jnp
from jax import lax
from jax.experimental import pallas as pl
from jax.experimental.pallas import tpu as pltpu

F32 = jnp.float32
BF16 = jnp.bfloat16

D = 1024
HEAD = 64
N_HEADS = 16
LANES = 128
N_PAIRS = D // LANES
CHUNK = 64
LORA_DECAY, LORA_AAA, LORA_GATE, LORA_VRES = 64, 64, 160, 32
LORA_W = 512
CONV_K = 3
MEM_LEN = 256
MEM_HEADS = 4
MEM_HD = D // MEM_HEADS
N_GROUPS, EPG, N_EXPERTS, D_EXPERT = 4, 4, 16, 512
RMS_EPS, LNX_EPS, L2_EPS = 1e-6, 64e-5, 1e-12
IN_W = 8 * D + LORA_W
VMEM_LIMIT = 56 * 1024 * 1024


def _cparams(sem):
    return pltpu.CompilerParams(dimension_semantics=sem, vmem_limit_bytes=VMEM_LIMIT)


def _rms(x, g):
    return x * lax.rsqrt(jnp.mean(x * x, axis=-1, keepdims=True) + RMS_EPS) * g


def _sigmoid(x):
    return 1.0 / (1.0 + jnp.exp(-x))


def _dot(a, b):
    return jnp.dot(a, b, preferred_element_type=F32)


def _dot_nt(a, b):
    return lax.dot_general(a, b, (((1,), (1,)), ((), ())), preferred_element_type=F32)


def _split_bf16(x):
    hi = x.astype(BF16)
    lo = (x - hi.astype(F32)).astype(BF16)
    return hi, lo


def _inproj_kernel(x_ref, g_ref, w_ref, o_ref, xn_ref):
    @pl.when(pl.program_id(1) == 0)
    def _():
        xn_ref[...] = _rms(x_ref[...], g_ref[...]).astype(BF16)

    o_ref[...] = _dot(xn_ref[...], w_ref[...]).astype(BF16)


def _inproj(x2, g, w, tm, tn):
    n = x2.shape[0]
    return pl.pallas_call(
        _inproj_kernel,
        grid=(n // tm, IN_W // tn),
        in_specs=[pl.BlockSpec((tm, D), lambda i, j: (i, 0)),
                  pl.BlockSpec((1, D), lambda i, j: (0, 0)),
                  pl.BlockSpec((D, tn), lambda i, j: (0, j))],
        out_specs=pl.BlockSpec((tm, tn), lambda i, j: (i, j)),
        out_shape=jax.ShapeDtypeStruct((n, IN_W), BF16),
        scratch_shapes=[pltpu.VMEM((tm, D), BF16)],
        compiler_params=_cparams(("parallel", "arbitrary")),
        name="inproj",
    )(x2, g, w)


def _shift_rows(z, prev_rows, k):
    tm = z.shape[0]
    rolled = pltpu.roll(z, k, 0)
    rows = lax.broadcasted_iota(jnp.int32, z.shape, 0)
    out = rolled
    for j in range(k):
        src = prev_rows[8 - k + j:8 - k + j + 1, :]
        out = jnp.where(rows == j, src, out)
    return out


def _prep_kernel(has_vres, tiles_per_seq, *refs):
    if has_vres:
        (r_ref, k_ref, v_ref, cx_ref, cb_ref, cc_ref, lo_ref, vf_ref, mu_ref, mul_ref, wupa_ref,
         wupb_ref, w0_ref, a0_ref, v0_ref, kk_ref, ka_ref, cw_ref,
         ro_ref, ko_ref, vo_ref, kko_ref, aao_ref, lwo_ref, ogo_ref, ybo_ref,
         crkv_ref, clo_ref, cu_ref) = refs
    else:
        (r_ref, k_ref, v_ref, cx_ref, cb_ref, cc_ref, lo_ref, mu_ref, mul_ref, wupa_ref,
         wupb_ref, w0_ref, a0_ref, kk_ref, ka_ref, cw_ref,
         ro_ref, ko_ref, vo_ref, kko_ref, aao_ref, lwo_ref, ogo_ref, ybo_ref,
         crkv_ref, clo_ref, cu_ref) = refs
        vf_ref = v0_ref = None

    @pl.when(pl.program_id(0) % tiles_per_seq == 0)
    def _():
        crkv_ref[...] = jnp.zeros_like(crkv_ref)
        clo_ref[...] = jnp.zeros_like(clo_ref)
        cu_ref[...] = jnp.zeros_like(cu_ref)

    tm = r_ref.shape[0]

    def mixed(z, carry, mu):
        return z + (_shift_rows(z, carry, 1) - z) * mu

    lo = lo_ref[...].astype(F32)
    lom = mixed(lo, clo_ref[...], mul_ref[...])
    clo_ref[...] = lo[tm - 8:, :]
    lane = lax.broadcasted_iota(jnp.int32, lom.shape, 1)
    gate_lo, gate_hi = LORA_DECAY + LORA_AAA, LORA_DECAY + LORA_AAA + LORA_GATE
    act = jnp.where(lane < LORA_DECAY, jnp.tanh(lom),
                    jnp.where((lane >= gate_lo) & (lane < gate_hi), _sigmoid(lom), lom)).astype(BF16)
    up_a = _dot(act[:, :LANES], wupa_ref[...])
    up_b = _dot(act[:, LANES:3 * LANES], wupb_ref[...])
    zdec = -(w0_ref[...] + up_a[:, :D])
    softplus = jnp.maximum(zdec, 0.0) + jnp.log(1.0 + jnp.exp(-jnp.abs(zdec)))
    lwo_ref[...] = -jnp.exp(-softplus - 0.5)
    aa = _sigmoid(a0_ref[...] + up_a[:, D:])
    aao_ref[...] = aa.astype(BF16)
    ogo_ref[...] = up_b[:, :D].astype(BF16)

    for idx, (src, dst) in enumerate(((r_ref, ro_ref), (k_ref, None), (v_ref, None))):
        z = src[...].astype(F32)
        zm = mixed(z, crkv_ref[:, idx * D:(idx + 1) * D], mu_ref[:, idx * D:(idx + 1) * D])
        crkv_ref[:, idx * D:(idx + 1) * D] = z[tm - 8:, :]
        if idx == 0:
            ro_ref[...] = zm.astype(BF16)
        elif idx == 1:
            kko_ref[...] = (zm * kk_ref[...]).astype(BF16)
            ko_ref[...] = (zm * (1.0 + (aa - 1.0) * ka_ref[...])).astype(BF16)
        else:
            if has_vres:
                vgate = _sigmoid(v0_ref[...] + up_b[:, D:])
                zm = zm + (vf_ref[...].astype(F32) - zm) * vgate
            vo_ref[...] = zm.astype(BF16)

    u = cc_ref[...].astype(F32) * cx_ref[...].astype(F32)
    cu = cu_ref[...]
    conv = cw_ref[CONV_K - 1:CONV_K, :] * u
    for j in range(CONV_K - 1):
        conv = conv + cw_ref[j:j + 1, :] * _shift_rows(u, cu, CONV_K - 1 - j)
    cu_ref[...] = u[tm - 8:, :]
    ybo_ref[...] = (cb_ref[...].astype(F32) * conv).astype(BF16)


def _prep(p, vfirst, lw, tm, seq):
    n = p.shape[0]
    has_vres = vfirst is not None
    col = lambda c: pl.BlockSpec((tm, D), lambda i, c=c: (i, c))
    row = lambda w: pl.BlockSpec((1, w), lambda i: (0, 0))
    full = lambda a: pl.BlockSpec(a.shape, lambda i: (0, 0))
    in_specs = [col(0), col(1), col(2), col(3), col(4), col(5),
                pl.BlockSpec((tm, LORA_W), lambda i: (i, 8 * D // LORA_W))]
    args = [p, p, p, p, p, p, p]
    if has_vres:
        in_specs.append(col(0))
        args.append(vfirst)
    in_specs += [row(3 * D), row(LORA_W), full(lw["wup_a"]), full(lw["wup_b"]), row(D), row(D)]
    args += [lw["mu_rkv"], lw["mu_lora"], lw["wup_a"], lw["wup_b"], lw["w0"], lw["a0"]]
    if has_vres:
        in_specs.append(row(D))
        args.append(lw["v0"])
    in_specs += [row(D), row(D), full(lw["conv_w"])]
    args += [lw["k_k"], lw["k_a"], lw["conv_w"]]
    bf = jax.ShapeDtypeStruct((n, D), BF16)
    out_shape = [bf, bf, bf, bf, bf, jax.ShapeDtypeStruct((n, D), F32), bf, bf]
    return pl.pallas_call(
        functools.partial(_prep_kernel, has_vres, seq // tm),
        grid=(n // tm,),
        in_specs=in_specs,
        out_specs=[col(0)] * 8,
        out_shape=out_shape,
        scratch_shapes=[pltpu.VMEM((8, 3 * D), F32), pltpu.VMEM((8, LORA_W), F32),
                        pltpu.VMEM((8, D), F32)],
        compiler_params=_cparams(("arbitrary",)),
        name="prep",
    )(*args)


def _pair_expand(x, m0, m1):
    return jnp.concatenate([jnp.where(m0, x, 0.0), jnp.where(m1, x, 0.0)], axis=0)


def _halves_sum(x, m0):
    s0 = jnp.sum(jnp.where(m0, x, 0.0), axis=-1, keepdims=True)
    s1 = jnp.sum(jnp.where(m0, 0.0, x), axis=-1, keepdims=True)
    return jnp.where(m0, s0, s1)


def _wkv_kernel(r_ref, k_ref, v_ref, kk_ref, aa_ref, lw_ref, rk_ref, g_ref, b_ref, o_ref, s_ref):
    @pl.when(pl.program_id(1) == 0)
    def _():
        s_ref[...] = jnp.zeros_like(s_ref)

    c2 = 2 * CHUNK
    lane = lax.broadcasted_iota(jnp.int32, (CHUNK, LANES), 1)
    m0 = lane < HEAD
    m1 = jnp.logical_not(m0)
    rows = lax.broadcasted_iota(jnp.int32, (c2, c2), 0)
    cols = lax.broadcasted_iota(jnp.int32, (c2, c2), 1)
    strict = cols < rows
    incl = cols <= rows
    eye = cols == rows
    tri = (lax.broadcasted_iota(jnp.int32, (CHUNK, CHUNK), 1)
           <= lax.broadcasted_iota(jnp.int32, (CHUNK, CHUNK), 0)).astype(BF16)
    n_chunks = r_ref.shape[0] // CHUNK

    def chunk_body(ci, carry):
        t0 = pl.multiple_of(ci * CHUNK, CHUNK)
        lw_all = lw_ref[pl.ds(t0, CHUNK), :]
        lw_hi, lw_lo = _split_bf16(lw_all)
        cum_all = _dot(tri, lw_hi) + _dot(tri, lw_lo)
        for pi in range(N_PAIRS):
            sl = slice(pi * LANES, (pi + 1) * LANES)
            lw = lw_all[:, sl]
            cum = cum_all[:, sl]
            cum_end = cum[CHUNK - 1:CHUNK, :]
            w_in = jnp.exp(cum)
            w_prev = jnp.exp(cum - lw)
            w_inv = jnp.exp(-cum)
            w_end = jnp.exp(cum_end - cum)
            r = r_ref[pl.ds(t0, CHUNK), sl].astype(F32)
            k = k_ref[pl.ds(t0, CHUNK), sl].astype(F32)
            v = v_ref[pl.ds(t0, CHUNK), sl].astype(F32)
            kk = kk_ref[pl.ds(t0, CHUNK), sl].astype(F32)
            aa = aa_ref[pl.ds(t0, CHUNK), sl].astype(F32)
            kk = kk / jnp.maximum(jnp.sqrt(_halves_sum(kk * kk, m0)), L2_EPS)
            b = kk * aa
            a_t = _pair_expand(-kk * w_prev, m0, m1)
            r_t = _pair_expand(r * w_in, m0, m1)
            b_t = _pair_expand(b * w_inv, m0, m1)
            k_t = _pair_expand(k * w_inv, m0, m1)
            b_h = _pair_expand(b * w_end, m0, m1)
            k_h = _pair_expand(k * w_end, m0, m1)
            v_e = _pair_expand(v, m0, m1)

            aa_m = _dot_nt(jnp.concatenate([a_t, r_t], axis=0).astype(BF16),
                           jnp.concatenate([b_t, k_t], axis=0).astype(BF16))
            a_ab = jnp.where(strict, aa_m[:c2, :c2], 0.0)
            a_ak = jnp.where(strict, aa_m[:c2, c2:], 0.0)
            a_rb = jnp.where(incl, aa_m[c2:, :c2], 0.0)
            a_rk = jnp.where(incl, aa_m[c2:, c2:], 0.0)

            t_m = jnp.where(eye, 1.0, 0.0) + a_ab
            a_b = a_ab.astype(BF16)
            a_m = _dot(a_b, a_b)
            n_steps = CHUNK.bit_length() - 1
            for step in range(1, n_steps):
                a_b = a_m.astype(BF16)
                if step < n_steps - 1:
                    both = _dot(a_b, jnp.concatenate([a_b, t_m.astype(BF16)], axis=1))
                    a_m = both[:, :c2]
                    t_m = t_m + both[:, c2:]
                else:
                    t_m = t_m + _dot(a_b, t_m.astype(BF16))

            v_b = v_e.astype(BF16)
            akv = _dot(a_ak.astype(BF16), v_b)
            tx = _dot(t_m.astype(BF16), jnp.concatenate([a_t, akv], axis=1).astype(BF16))
            rhs = jnp.concatenate(
                [tx.astype(BF16), jnp.concatenate([jnp.zeros((c2, c2), BF16), v_b], axis=1)], axis=0)
            lhs = jnp.concatenate(
                [jnp.concatenate([a_rb, a_rk], axis=1), jnp.concatenate([b_h.T, k_h.T], axis=1)],
                axis=0).astype(BF16)
            z = _dot(lhs, rhs)
            g_m = r_t + z[:c2, :c2]
            h_m = z[:c2, c2:]
            p_m = z[c2:, :c2] + jnp.where(eye, jnp.exp(cum_end), 0.0)
            q_m = z[c2:, c2:]

            s_t = s_ref[pi]
            zz = _dot(jnp.concatenate([g_m, p_m], axis=0).astype(BF16), s_t.astype(BF16))
            y_bd = zz[:c2] + h_m
            s_ref[pi] = zz[c2:] + q_m
            y = y_bd[:CHUNK] + y_bd[CHUNK:]

            mu = _halves_sum(y, m0) * (1.0 / HEAD)
            yc = y - mu
            var = _halves_sum(yc * yc, m0) * (1.0 / HEAD)
            yn = yc * lax.rsqrt(var + LNX_EPS) * g_ref[:, sl] + b_ref[:, sl]
            bonus = _halves_sum(r * k * rk_ref[:, sl], m0) * v
            o_ref[pl.ds(t0, CHUNK), sl] = (yn + bonus).astype(BF16)
        return carry

    lax.fori_loop(0, n_chunks, chunk_body, 0)


def _wkv(r, k, v, kk, aa, lw, lwts, bsz, seq, tc):
    n = r.shape[0]
    nt = seq // tc
    blk = pl.BlockSpec((tc, D), lambda b, t: (b * nt + t, 0))
    row = pl.BlockSpec((1, D), lambda b, t: (0, 0))
    return pl.pallas_call(
        _wkv_kernel,
        grid=(bsz, nt),
        in_specs=[blk] * 6 + [row] * 3,
        out_specs=blk,
        out_shape=jax.ShapeDtypeStruct((n, D), BF16),
        scratch_shapes=[pltpu.VMEM((N_PAIRS, LANES, LANES), F32)],
        compiler_params=_cparams(("parallel", "arbitrary")),
        name="wkv",
    )(r, k, v, kk, aa, lw, lwts["r_k"], lwts["lnx_g"], lwts["lnx_b"])


def _memkv_kernel(m_ref, g_ref, w_ref, o_ref):
    o_ref[...] = _dot(_rms(m_ref[...], g_ref[...]).astype(BF16), w_ref[...]).astype(BF16)


def _memkv(mem2, g, w):
    n = mem2.shape[0]
    return pl.pallas_call(
        _memkv_kernel,
        grid=(n // MEM_LEN,),
        in_specs=[pl.BlockSpec((MEM_LEN, D), lambda i: (i, 0)),
                  pl.BlockSpec((1, D), lambda i: (0, 0)),
                  pl.BlockSpec((D, 2 * D), lambda i: (0, 0))],
        out_specs=pl.BlockSpec((MEM_LEN, 2 * D), lambda i: (i, 0)),
        out_shape=jax.ShapeDtypeStruct((n, 2 * D), BF16),
        compiler_params=_cparams(("parallel",)),
        name="memkv",
    )(mem2, g, w)


def _attn_kernel(x_ref, ya_ref, og_ref, yb_ref, ga_ref, gb_ref, kv_ref, gbias_ref, wa_ref, wb_ref,
                 wmix_ref, gq_ref, wq_ref, wo_ref, o_ref):
    y_a = (ya_ref[...].astype(F32) * og_ref[...].astype(F32)).astype(BF16)
    g_a = _sigmoid(ga_ref[...].astype(F32) + gbias_ref[:, :D])
    g_b = _sigmoid(gb_ref[...].astype(F32) + gbias_ref[:, D:])
    merged = g_a * _dot(y_a, wa_ref[...]) + g_b * _dot(yb_ref[...], wb_ref[...])
    x1 = x_ref[...] + _dot(merged.astype(BF16), wmix_ref[...])

    hn = _rms(x1, gq_ref[...]).astype(BF16)
    q = _dot(hn, wq_ref[...]).astype(BF16)
    heads = []
    for h in range(MEM_HEADS):
        sl = slice(h * MEM_HD, (h + 1) * MEM_HD)
        s = _dot_nt(q[:, sl], kv_ref[:, sl]) * (MEM_HD ** -0.5)
        s = s - jnp.max(s, axis=-1, keepdims=True)
        e = jnp.exp(s)
        pr = (e / jnp.sum(e, axis=-1, keepdims=True)).astype(BF16)
        heads.append(_dot(pr, kv_ref[:, D + h * MEM_HD:D + (h + 1) * MEM_HD]).astype(BF16))
    o = jnp.concatenate(heads, axis=1)
    o_ref[...] = x1 + _dot(o, wo_ref[...])


def _attn(x2, ya, og, yb, p, kv, lw, tm, seq):
    n = x2.shape[0]
    tps = seq // tm
    col = lambda c: pl.BlockSpec((tm, D), lambda i, c=c: (i, c))
    full = lambda a: pl.BlockSpec(a.shape, lambda i: (0, 0))
    wts = [lw["gate_b"], lw["w_br_a"], lw["w_br_b"], lw["w_mix_out"], lw["norm_memq"], lw["wq_mem"],
           lw["wo_mem"]]
    return pl.pallas_call(
        _attn_kernel,
        grid=(n // tm,),
        in_specs=[col(0), col(0), col(0), col(0), col(6), col(7),
                  pl.BlockSpec((MEM_LEN, 2 * D), lambda i: (i // tps, 0))] + [full(w) for w in wts],
        out_specs=col(0),
        out_shape=jax.ShapeDtypeStruct((n, D), F32),
        compiler_params=_cparams(("parallel",)),
        name="attn",
    )(x2, ya, og, yb, p, p, kv, *wts)


def _route(logits):
    lane = lax.broadcasted_iota(jnp.int32, logits.shape, 1).astype(F32)
    neg = jnp.float32(-jnp.inf)
    big = jnp.float32(1 << 20)
    is_g = lane < N_GROUPS
    lg = jnp.where(is_g, logits, neg)
    gmax = jnp.max(lg, axis=-1, keepdims=True)
    ge = jnp.where(is_g, jnp.exp(lg - gmax), 0.0)
    gp = ge / jnp.sum(ge, axis=-1, keepdims=True)
    g_w = jnp.max(gp, axis=-1, keepdims=True)
    g_i = jnp.min(jnp.where(is_g & (gp == g_w), lane, big), axis=-1, keepdims=True)
    lo = N_GROUPS + g_i * EPG
    sel = (lane >= lo) & (lane < lo + EPG)
    le = jnp.where(sel, logits, neg)
    emax = jnp.max(le, axis=-1, keepdims=True)
    ee = jnp.where(sel, jnp.exp(le - emax), 0.0)
    ep = ee / jnp.sum(ee, axis=-1, keepdims=True)
    p1 = jnp.max(ep, axis=-1, keepdims=True)
    i1 = jnp.min(jnp.where(sel & (ep == p1), lane, big), axis=-1, keepdims=True)
    rest = sel & (lane != i1)
    p2 = jnp.max(jnp.where(rest, ep, -1.0), axis=-1, keepdims=True)
    i2 = jnp.min(jnp.where(rest & (ep == p2), lane, big), axis=-1, keepdims=True)
    tot = p1 + p2
    return jnp.where(lane == i1, g_w * (p1 / tot), 0.0) + jnp.where(lane == i2, g_w * (p2 / tot), 0.0)


def _moe_kernel(final_norm, x_ref, g_ref, rw_ref, rb_ref, wg_ref, wu_ref, wd_ref, gf_ref, o_ref,
                hn_ref, comb_ref, acc_ref):
    e = pl.program_id(1)

    @pl.when(e == 0)
    def _():
        hn = _rms(x_ref[...], g_ref[...])
        hi, lo = _split_bf16(hn)
        hn_ref[...] = hi
        w_hi = rw_ref[0]
        w_lo = rw_ref[1]
        logits = _dot(hi, w_hi) + _dot(lo, w_hi) + _dot(hi, w_lo) + rb_ref[...]
        comb_ref[...] = _route(logits)
        acc_ref[...] = jnp.zeros_like(acc_ref)

    hn = hn_ref[...]
    gate = _dot(hn, wg_ref[0])
    he = (gate * _sigmoid(gate) * _dot(hn, wu_ref[0])).astype(BF16)
    lane = lax.broadcasted_iota(jnp.int32, comb_ref.shape, 1)
    cw = jnp.sum(jnp.where(lane == e + N_GROUPS, comb_ref[...], 0.0), axis=-1, keepdims=True)
    acc_ref[...] += cw * _dot(he, wd_ref[0])

    @pl.when(e == N_EXPERTS - 1)
    def _():
        out = x_ref[...] + acc_ref[...]
        if final_norm:
            out = _rms(out, gf_ref[...])
        o_ref[...] = out


def _moe(x2, lw, norm_f, final_norm, tm):
    n = x2.shape[0]
    row = pl.BlockSpec((1, D), lambda i, e: (0, 0))
    return pl.pallas_call(
        functools.partial(_moe_kernel, final_norm),
        grid=(n // tm, N_EXPERTS),
        in_specs=[pl.BlockSpec((tm, D), lambda i, e: (i, 0)), row,
                  pl.BlockSpec((2, D, LANES), lambda i, e: (0, 0, 0)),
                  pl.BlockSpec((1, LANES), lambda i, e: (0, 0)),
                  pl.BlockSpec((1, D, D_EXPERT), lambda i, e: (e, 0, 0)),
                  pl.BlockSpec((1, D, D_EXPERT), lambda i, e: (e, 0, 0)),
                  pl.BlockSpec((1, D_EXPERT, D), lambda i, e: (e, 0, 0)), row],
        out_specs=pl.BlockSpec((tm, D), lambda i, e: (i, 0)),
        out_shape=jax.ShapeDtypeStruct((n, D), F32),
        scratch_shapes=[pltpu.VMEM((tm, D), BF16), pltpu.VMEM((tm, LANES), F32),
                        pltpu.VMEM((tm, D), F32)],
        compiler_params=_cparams(("parallel", "arbitrary")),
        name="moe",
    )(x2, lw["norm_ffn"], lw["router_w"], lw["router_b"], lw["w_e_gate"], lw["w_e_up"],
      lw["w_e_down"], norm_f)


def _layer_weights(l, w):
    rw = 3 * D
    n_lo = LORA_DECAY + LORA_AAA + LORA_GATE
    w_in = w["w_in"][l]
    lora_cols = [w_in[:, rw:rw + n_lo]]
    mu_lora = [w["mu_in"][l][rw:rw + n_lo]]
    if l > 0:
        lora_cols.append(w["w_vres_in"][l - 1])
        mu_lora.append(w["mu_vres"][l - 1])
    lora = jnp.concatenate(lora_cols, axis=1)
    lora = jnp.pad(lora, ((0, 0), (0, LORA_W - lora.shape[1])))
    mu_lora = jnp.concatenate(mu_lora)
    mu_lora = jnp.pad(mu_lora, (0, LORA_W - mu_lora.shape[0]))
    w_all = jnp.concatenate([w_in[:, :rw], w_in[:, rw + n_lo:], lora], axis=1).astype(BF16)

    wup_a = jnp.zeros((LANES, 2 * D), F32)
    wup_a = wup_a.at[:LORA_DECAY, :D].set(w["w_decay_up"][l])
    wup_a = wup_a.at[LORA_DECAY:LORA_DECAY + LORA_AAA, D:].set(w["w_aaa_up"][l])
    wup_b = jnp.zeros((2 * LANES, 2 * D), F32)
    wup_b = wup_b.at[:LORA_GATE, :D].set(w["w_gate_up"][l])
    if l > 0:
        wup_b = wup_b.at[LORA_GATE:LORA_GATE + LORA_VRES, D:].set(w["w_vres_up"][l - 1])

    router = jnp.concatenate([w["router_g"][l], w["router_e"][l]], axis=1)
    router = jnp.pad(router, ((0, 0), (0, LANES - router.shape[1])))
    r_hi = router.astype(BF16)
    r_lo = (router - r_hi.astype(F32)).astype(BF16)
    router_b = jnp.concatenate([w["router_g_b"][l], w["router_e_b"][l]])
    router_b = jnp.pad(router_b, (0, LANES - router_b.shape[0]))[None, :]

    row = lambda a: a.reshape(1, -1)
    out = {
        "norm_mix": row(w["norm_mix"][l]), "w_all": w_all,
        "mu_rkv": row(w["mu_in"][l][:rw]), "mu_lora": row(mu_lora),
        "wup_a": wup_a.astype(BF16), "wup_b": wup_b.astype(BF16),
        "w0": row(w["w0"][l]), "a0": row(w["a0"][l]), "k_k": row(w["k_k"][l]), "k_a": row(w["k_a"][l]),
        "conv_w": w["conv_w"][l],
        "r_k": row(w["r_k"][l]), "lnx_g": row(w["lnx_g"][l]), "lnx_b": row(w["lnx_b"][l]),
        "gate_b": row(w["gate_b"][l]),
        "w_br_a": w["w_br_a"][l].astype(BF16), "w_br_b": w["w_br_b"][l].astype(BF16),
        "w_mix_out": w["w_mix_out"][l].astype(BF16),
        "norm_memq": row(w["norm_memq"][l]), "norm_memkv": row(w["norm_memkv"][l]),
        "wq_mem": w["wq_mem"][l].astype(BF16), "wkv_mem": w["wkv_mem"][l].astype(BF16),
        "wo_mem": w["wo_mem"][l].astype(BF16),
        "norm_ffn": row(w["norm_ffn"][l]),
        "router_w": jnp.stack([r_hi, r_lo]), "router_b": router_b,
        "w_e_gate": w["w_e_gate"][l].astype(BF16), "w_e_up": w["w_e_up"][l].astype(BF16),
        "w_e_down": w["w_e_down"][l].astype(BF16),
    }
    if l > 0:
        out["v0"] = row(w["v0"][l - 1])
    return out


def _tile(total, want):
    t = min(want, total)
    assert total % t == 0, (total, t)
    return t


def kernel(x, mem, norm_mix, w_in, mu_in, w_vres_in, mu_vres, w0, w_decay_up, a0, w_aaa_up, w_gate_up,
           v0, w_vres_up, k_k, k_a, r_k, lnx_g, lnx_b, conv_w, gate_b, w_br_a, w_br_b, w_mix_out,
           norm_memq, norm_memkv, wq_mem, wkv_mem, wo_mem, norm_ffn, router_g, router_g_b, router_e,
           router_e_b, w_e_gate, w_e_up, w_e_down, norm_f):
    w = dict(norm_mix=norm_mix, w_in=w_in, mu_in=mu_in, w_vres_in=w_vres_in, mu_vres=mu_vres, w0=w0,
             w_decay_up=w_decay_up, a0=a0, w_aaa_up=w_aaa_up, w_gate_up=w_gate_up, v0=v0,
             w_vres_up=w_vres_up, k_k=k_k, k_a=k_a, r_k=r_k, lnx_g=lnx_g, lnx_b=lnx_b, conv_w=conv_w,
             gate_b=gate_b, w_br_a=w_br_a, w_br_b=w_br_b, w_mix_out=w_mix_out, norm_memq=norm_memq,
             norm_memkv=norm_memkv, wq_mem=wq_mem, wkv_mem=wkv_mem, wo_mem=wo_mem, norm_ffn=norm_ffn,
             router_g=router_g, router_g_b=router_g_b, router_e=router_e, router_e_b=router_e_b,
             w_e_gate=w_e_gate, w_e_up=w_e_up, w_e_down=w_e_down)
    bsz, seq, _ = x.shape
    depth = norm_mix.shape[0]
    n = bsz * seq
    assert seq % CHUNK == 0 and mem.shape[1] == MEM_LEN
    x2 = x.reshape(n, D)
    mem2 = mem.reshape(bsz * MEM_LEN, D)
    tm_in = _tile(n, 1024)
    tm_prep = _tile(seq, 256)
    tc_wkv = _tile(seq, 256)
    tm_attn = _tile(seq, 512)
    tm_moe = _tile(n, 512)
    norm_f2 = norm_f.reshape(1, D)

    v_first = None
    for l in range(depth):
        lw = _layer_weights(l, w)
        p = _inproj(x2, lw["norm_mix"], lw["w_all"], tm_in, IN_W // 4)
        r, k, v, kk, aa, lwd, og, yb = _prep(p, v_first, lw, tm_prep, seq)
        if l == 0:
            v_first = v
        ya = _wkv(r, k, v, kk, aa, lwd, lw, bsz, seq, tc_wkv)
        kv = _memkv(mem2, lw["norm_memkv"], lw["wkv_mem"])
        x2 = _attn(x2, ya, og, yb, p, kv, lw, tm_attn, seq)
        x2 = _moe(x2, lw, norm_f2, l == depth - 1, tm_moe)
    return x2.reshape(bsz, seq, D)
```

```python
import functools

import jax
import jax.numpy as jnp
from jax import lax
from jax.experimental import pallas as pl
from jax.experimental.pallas import tpu as pltpu

F32 = jnp.float32
BF16 = jnp.bfloat16

D = 1024
HEAD = 64
N_HEADS = 16
LANES = 128
N_PAIRS = D // LANES
CHUNK = 64
LORA_DECAY, LORA_AAA, LORA_GATE, LORA_VRES = 64, 64, 160, 32
LORA_W = 512
CONV_K = 3
MEM_LEN = 256
MEM_HEADS = 4
MEM_HD = D // MEM_HEADS
N_GROUPS, EPG, N_EXPERTS, D_EXPERT = 4, 4, 16, 512
RMS_EPS, LNX_EPS, L2_EPS = 1e-6, 64e-5, 1e-12
IN_W = 8 * D + LORA_W
VMEM_LIMIT = 56 * 1024 * 1024


def _cparams(sem):
    return pltpu.CompilerParams(dimension_semantics=sem, vmem_limit_bytes=VMEM_LIMIT)


def _rms(x, g):
    return x * lax.rsqrt(jnp.mean(x * x, axis=-1, keepdims=True) + RMS_EPS) * g


def _sigmoid(x):
    return 1.0 / (1.0 + jnp.exp(-x))


def _dot(a, b):
    return jnp.dot(a, b, preferred_element_type=F32)


def _dot_nt(a, b):
    return lax.dot_general(a, b, (((1,), (1,)), ((), ())), preferred_element_type=F32)


def _split_bf16(x):
    hi = x.astype(BF16)
    lo = (x - hi.astype(F32)).astype(BF16)
    return hi, lo


def _inproj_kernel(x_ref, g_ref, w_ref, o_ref, xn_ref):
    @pl.when(pl.program_id(1) == 0)
    def _():
        xn_ref[...] = _rms(x_ref[...], g_ref[...]).astype(BF16)

    o_ref[...] = _dot(xn_ref[...], w_ref[...]).astype(BF16)


def _inproj(x2, g, w, tm, tn):
    n = x2.shape[0]
    return pl.pallas_call(
        _inproj_kernel,
        grid=(n // tm, IN_W // tn),
        in_specs=[pl.BlockSpec((tm, D), lambda i, j: (i, 0)),
                  pl.BlockSpec((1, D), lambda i, j: (0, 0)),
                  pl.BlockSpec((D, tn), lambda i, j: (0, j))],
        out_specs=pl.BlockSpec((tm, tn), lambda i, j: (i, j)),
        out_shape=jax.ShapeDtypeStruct((n, IN_W), BF16),
        scratch_shapes=[pltpu.VMEM((tm, D), BF16)],
        compiler_params=_cparams(("parallel", "arbitrary")),
        name="inproj",
    )(x2, g, w)


def _shift_rows(z, prev_rows, k):
    tm = z.shape[0]
    rolled = pltpu.roll(z, k, 0)
    rows = lax.broadcasted_iota(jnp.int32, z.shape, 0)
    out = rolled
    for j in range(k):
        src = prev_rows[8 - k + j:8 - k + j + 1, :]
        out = jnp.where(rows == j, src, out)
    return out


def _prep_kernel(has_vres, tiles_per_seq, *refs):
    if has_vres:
        (r_ref, k_ref, v_ref, cx_ref, cb_ref, cc_ref, lo_ref, vf_ref, mu_ref, mul_ref, wupa_ref,
         wupb_ref, w0_ref, a0_ref, v0_ref, kk_ref, ka_ref, cw_ref,
         ro_ref, ko_ref, vo_ref, kko_ref, aao_ref, lwo_ref, ogo_ref, ybo_ref,
         crkv_ref, clo_ref, cu_ref) = refs
    else:
        (r_ref, k_ref, v_ref, cx_ref, cb_ref, cc_ref, lo_ref, mu_ref, mul_ref, wupa_ref,
         wupb_ref, w0_ref, a0_ref, kk_ref, ka_ref, cw_ref,
         ro_ref, ko_ref, vo_ref, kko_ref, aao_ref, lwo_ref, ogo_ref, ybo_ref,
         crkv_ref, clo_ref, cu_ref) = refs
        vf_ref = v0_ref = None

    @pl.when(pl.program_id(0) % tiles_per_seq == 0)
    def _():
        crkv_ref[...] = jnp.zeros_like(crkv_ref)
        clo_ref[...] = jnp.zeros_like(clo_ref)
        cu_ref[...] = jnp.zeros_like(cu_ref)

    tm = r_ref.shape[0]

    def mixed(z, carry, mu):
        return z + (_shift_rows(z, carry, 1) - z) * mu

    lo = lo_ref[...].astype(F32)
    lom = mixed(lo, clo_ref[...], mul_ref[...])
    clo_ref[...] = lo[tm - 8:, :]
    lane = lax.broadcasted_iota(jnp.int32, lom.shape, 1)
    gate_lo, gate_hi = LORA_DECAY + LORA_AAA, LORA_DECAY + LORA_AAA + LORA_GATE
    act = jnp.where(lane < LORA_DECAY, jnp.tanh(lom),
                    jnp.where((lane >= gate_lo) & (lane < gate_hi), _sigmoid(lom), lom)).astype(BF16)
    up_a = _dot(act[:, :LANES], wupa_ref[...])
    up_b = _dot(act[:, LANES:3 * LANES], wupb_ref[...])
    zdec = -(w0_ref[...] + up_a[:, :D])
    softplus = jnp.maximum(zdec, 0.0) + jnp.log(1.0 + jnp.exp(-jnp.abs(zdec)))
    lwo_ref[...] = -jnp.exp(-softplus - 0.5)
    aa = _sigmoid(a0_ref[...] + up_a[:, D:])
    aao_ref[...] = aa.astype(BF16)
    ogo_ref[...] = up_b[:, :D].astype(BF16)

    for idx, (src, dst) in enumerate(((r_ref, ro_ref), (k_ref, None), (v_ref, None))):
        z = src[...].astype(F32)
        zm = mixed(z, crkv_ref[:, idx * D:(idx + 1) * D], mu_ref[:, idx * D:(idx + 1) * D])
        crkv_ref[:, idx * D:(idx + 1) * D] = z[tm - 8:, :]
        if idx == 0:
            ro_ref[...] = zm.astype(BF16)
        elif idx == 1:
            kko_ref[...] = (zm * kk_ref[...]).astype(BF16)
            ko_ref[...] = (zm * (1.0 + (aa - 1.0) * ka_ref[...])).astype(BF16)
        else:
            if has_vres:
                vgate = _sigmoid(v0_ref[...] + up_b[:, D:])
                zm = zm + (vf_ref[...].astype(F32) - zm) * vgate
            vo_ref[...] = zm.astype(BF16)

    u = cc_ref[...].astype(F32) * cx_ref[...].astype(F32)
    cu = cu_ref[...]
    conv = cw_ref[CONV_K - 1:CONV_K, :] * u
    for j in range(CONV_K - 1):
        conv = conv + cw_ref[j:j + 1, :] * _shift_rows(u, cu, CONV_K - 1 - j)
    cu_ref[...] = u[tm - 8:, :]
    ybo_ref[...] = (cb_ref[...].astype(F32) * conv).astype(BF16)


def _prep(p, vfirst, lw, tm, seq):
    n = p.shape[0]
    has_vres = vfirst is not None
    col = lambda c: pl.BlockSpec((tm, D), lambda i, c=c: (i, c))
    row = lambda w: pl.BlockSpec((1, w), lambda i: (0, 0))
    full = lambda a: pl.BlockSpec(a.shape, lambda i: (0, 0))
    in_specs = [col(0), col(1), col(2), col(3), col(4), col(5),
                pl.BlockSpec((tm, LORA_W), lambda i: (i, 8 * D // LORA_W))]
    args = [p, p, p, p, p, p, p]
    if has_vres:
        in_specs.append(col(0))
        args.append(vfirst)
    in_specs += [row(3 * D), row(LORA_W), full(lw["wup_a"]), full(lw["wup_b"]), row(D), row(D)]
    args += [lw["mu_rkv"], lw["mu_lora"], lw["wup_a"], lw["wup_b"], lw["w0"], lw["a0"]]
    if has_vres:
        in_specs.append(row(D))
        args.append(lw["v0"])
    in_specs += [row(D), row(D), full(lw["conv_w"])]
    args += [lw["k_k"], lw["k_a"], lw["conv_w"]]
    bf = jax.ShapeDtypeStruct((n, D), BF16)
    out_shape = [bf, bf, bf, bf, bf, jax.ShapeDtypeStruct((n, D), F32), bf, bf]
    return pl.pallas_call(
        functools.partial(_prep_kernel, has_vres, seq // tm),
        grid=(n // tm,),
        in_specs=in_specs,
        out_specs=[col(0)] * 8,
        out_shape=out_shape,
        scratch_shapes=[pltpu.VMEM((8, 3 * D), F32), pltpu.VMEM((8, LORA_W), F32),
                        pltpu.VMEM((8, D), F32)],
        compiler_params=_cparams(("arbitrary",)),
        name="prep",
    )(*args)


def _pair_expand(x, m0, m1):
    return jnp.concatenate([jnp.where(m0, x, 0.0), jnp.where(m1, x, 0.0)], axis=0)


def _halves_sum(x, m0):
    s0 = jnp.sum(jnp.where(m0, x, 0.0), axis=-1, keepdims=True)
    s1 = jnp.sum(jnp.where(m0, 0.0, x), axis=-1, keepdims=True)
    return jnp.where(m0, s0, s1)


def _wkv_kernel(r_ref, k_ref, v_ref, kk_ref, aa_ref, lw_ref, rk_ref, g_ref, b_ref, o_ref, s_ref):
    @pl.when(pl.program_id(1) == 0)
    def _():
        s_ref[...] = jnp.zeros_like(s_ref)

    c2 = 2 * CHUNK
    lane = lax.broadcasted_iota(jnp.int32, (CHUNK, LANES), 1)
    m0 = lane < HEAD
    m1 = jnp.logical_not(m0)
    rows = lax.broadcasted_iota(jnp.int32, (c2, c2), 0)
    cols = lax.broadcasted_iota(jnp.int32, (c2, c2), 1)
    strict = cols < rows
    incl = cols <= rows
    eye = cols == rows
    tri = (lax.broadcasted_iota(jnp.int32, (CHUNK, CHUNK), 1)
           <= lax.broadcasted_iota(jnp.int32, (CHUNK, CHUNK), 0)).astype(BF16)
    n_chunks = r_ref.shape[0] // CHUNK

    def chunk_body(ci, carry):
        t0 = pl.multiple_of(ci * CHUNK, CHUNK)
        lw_all = lw_ref[pl.ds(t0, CHUNK), :]
        lw_hi, lw_lo = _split_bf16(lw_all)
        cum_all = _dot(tri, lw_hi) + _dot(tri, lw_lo)
        pairs = range(N_PAIRS)
        sls = [slice(pi * LANES, (pi + 1) * LANES) for pi in pairs]
        cat0 = lambda *xs: jnp.concatenate(xs, axis=0)
        cat1 = lambda *xs: jnp.concatenate(xs, axis=1)
        expand = lambda x: _pair_expand(x, m0, m1)
        ld = lambda ref, sl: ref[pl.ds(t0, CHUNK), sl].astype(F32)

        cum = [cum_all[:, sl] for sl in sls]
        cum_end = [c[CHUNK - 1:CHUNK, :] for c in cum]
        w_inv = [jnp.exp(-c) for c in cum]
        w_end = [jnp.exp(ce - c) for c, ce in zip(cum, cum_end)]
        r = [ld(r_ref, sl) for sl in sls]
        k = [ld(k_ref, sl) for sl in sls]
        kk = [ld(kk_ref, sl) for sl in sls]
        kk = [x / jnp.maximum(jnp.sqrt(_halves_sum(x * x, m0)), L2_EPS) for x in kk]
        b = [x * ld(aa_ref, sl) for x, sl in zip(kk, sls)]
        a_t = [expand(-x * jnp.exp(c - lw_all[:, sl])) for x, c, sl in zip(kk, cum, sls)]
        r_t = [expand(x * jnp.exp(c)) for x, c in zip(r, cum)]
        lhs1 = [cat0(a, rr).astype(BF16) for a, rr in zip(a_t, r_t)]
        rhs1 = [cat0(expand(bb * wi), expand(kx * wi)).astype(BF16) for bb, kx, wi in zip(b, k, w_inv)]
        aa_m = [_dot_nt(x, y) for x, y in zip(lhs1, rhs1)]
        a_ab = [jnp.where(strict, m[:c2, :c2], 0.0) for m in aa_m]
        a_ak = [jnp.where(strict, m[:c2, c2:], 0.0).astype(BF16) for m in aa_m]
        lhs_top = [cat1(jnp.where(incl, m[c2:, :c2], 0.0), jnp.where(incl, m[c2:, c2:], 0.0)).astype(BF16)
                   for m in aa_m]

        t_m = [jnp.where(eye, 1.0, 0.0) + m for m in a_ab]
        a_b = [m.astype(BF16) for m in a_ab]
        a_m = [_dot(x, x) for x in a_b]
        n_steps = CHUNK.bit_length() - 1
        for step in range(1, n_steps):
            a_b = [m.astype(BF16) for m in a_m]
            if step < n_steps - 1:
                both = [_dot(x, cat1(x, t.astype(BF16))) for x, t in zip(a_b, t_m)]
                a_m = [m[:, :c2] for m in both]
                t_m = [t + m[:, c2:] for t, m in zip(t_m, both)]
            else:
                t_m = [t + _dot(x, t.astype(BF16)) for x, t in zip(a_b, t_m)]

        v = [ld(v_ref, sl) for sl in sls]
        v_b = [expand(x).astype(BF16) for x in v]
        akv = [_dot(x, y) for x, y in zip(a_ak, v_b)]
        tx = [_dot(t.astype(BF16), cat1(a, y).astype(BF16)) for t, a, y in zip(t_m, a_t, akv)]
        rhs2 = [cat0(x.astype(BF16), cat1(jnp.zeros((c2, c2), BF16), y)) for x, y in zip(tx, v_b)]
        lhs_bot = [cat1(expand(bb * we).T, expand(kx * we).T).astype(BF16)
                   for bb, kx, we in zip(b, k, w_end)]
        z = [_dot(cat0(x, y), w) for x, y, w in zip(lhs_top, lhs_bot, rhs2)]
        gp = [cat0(rt + m[:c2, :c2], m[c2:, :c2] + jnp.where(eye, jnp.exp(ce), 0.0)).astype(BF16)
              for rt, m, ce in zip(r_t, z, cum_end)]

        zz = [_dot(x, s_ref[pi].astype(BF16)) for pi, x in zip(pairs, gp)]
        for pi in pairs:
            s_ref[pi] = zz[pi][c2:] + z[pi][c2:, c2:]
        y = [(m[:CHUNK] + m[CHUNK:c2]) + (w[:CHUNK, c2:] + w[CHUNK:c2, c2:]) for m, w in zip(zz, z)]

        for pi, sl in zip(pairs, sls):
            mu = _halves_sum(y[pi], m0) * (1.0 / HEAD)
            yc = y[pi] - mu
            var = _halves_sum(yc * yc, m0) * (1.0 / HEAD)
            yn = yc * lax.rsqrt(var + LNX_EPS) * g_ref[:, sl] + b_ref[:, sl]
            bonus = _halves_sum(r[pi] * k[pi] * rk_ref[:, sl], m0) * v[pi]
            o_ref[pl.ds(t0, CHUNK), sl] = (yn + bonus).astype(BF16)
        return carry

    lax.fori_loop(0, n_chunks, chunk_body, 0)


def _wkv(r, k, v, kk, aa, lw, lwts, bsz, seq, tc):
    n = r.shape[0]
    nt = seq // tc
    blk = pl.BlockSpec((tc, D), lambda b, t: (b * nt + t, 0))
    row = pl.BlockSpec((1, D), lambda b, t: (0, 0))
    return pl.pallas_call(
        _wkv_kernel,
        grid=(bsz, nt),
        in_specs=[blk] * 6 + [row] * 3,
        out_specs=blk,
        out_shape=jax.ShapeDtypeStruct((n, D), BF16),
        scratch_shapes=[pltpu.VMEM((N_PAIRS, LANES, LANES), F32)],
        compiler_params=_cparams(("parallel", "arbitrary")),
        name="wkv",
    )(r, k, v, kk, aa, lw, lwts["r_k"], lwts["lnx_g"], lwts["lnx_b"])


def _memkv_kernel(m_ref, g_ref, w_ref, o_ref):
    o_ref[...] = _dot(_rms(m_ref[...], g_ref[...]).astype(BF16), w_ref[...]).astype(BF16)


def _memkv(mem2, g, w):
    n = mem2.shape[0]
    return pl.pallas_call(
        _memkv_kernel,
        grid=(n // MEM_LEN,),
        in_specs=[pl.BlockSpec((MEM_LEN, D), lambda i: (i, 0)),
                  pl.BlockSpec((1, D), lambda i: (0, 0)),
                  pl.BlockSpec((D, 2 * D), lambda i: (0, 0))],
        out_specs=pl.BlockSpec((MEM_LEN, 2 * D), lambda i: (i, 0)),
        out_shape=jax.ShapeDtypeStruct((n, 2 * D), BF16),
        compiler_params=_cparams(("parallel",)),
        name="memkv",
    )(mem2, g, w)


def _attn_kernel(x_ref, ya_ref, og_ref, yb_ref, ga_ref, gb_ref, kv_ref, gbias_ref, wa_ref, wb_ref,
                 wmix_ref, gq_ref, wq_ref, wo_ref, o_ref):
    y_a = (ya_ref[...].astype(F32) * og_ref[...].astype(F32)).astype(BF16)
    g_a = _sigmoid(ga_ref[...].astype(F32) + gbias_ref[:, :D])
    g_b = _sigmoid(gb_ref[...].astype(F32) + gbias_ref[:, D:])
    merged = g_a * _dot(y_a, wa_ref[...]) + g_b * _dot(yb_ref[...], wb_ref[...])
    x1 = x_ref[...] + _dot(merged.astype(BF16), wmix_ref[...])

    hn = _rms(x1, gq_ref[...]).astype(BF16)
    q = _dot(hn, wq_ref[...]).astype(BF16)
    heads = []
    for h in range(MEM_HEADS):
        sl = slice(h * MEM_HD, (h + 1) * MEM_HD)
        s = _dot_nt(q[:, sl], kv_ref[:, sl]) * (MEM_HD ** -0.5)
        s = s - jnp.max(s, axis=-1, keepdims=True)
        e = jnp.exp(s)
        pr = (e / jnp.sum(e, axis=-1, keepdims=True)).astype(BF16)
        heads.append(_dot(pr, kv_ref[:, D + h * MEM_HD:D + (h + 1) * MEM_HD]).astype(BF16))
    o = jnp.concatenate(heads, axis=1)
    o_ref[...] = x1 + _dot(o, wo_ref[...])


def _attn(x2, ya, og, yb, p, kv, lw, tm, seq):
    n = x2.shape[0]
    tps = seq // tm
    col = lambda c: pl.BlockSpec((tm, D), lambda i, c=c: (i, c))
    full = lambda a: pl.BlockSpec(a.shape, lambda i: (0, 0))
    wts = [lw["gate_b"], lw["w_br_a"], lw["w_br_b"], lw["w_mix_out"], lw["norm_memq"], lw["wq_mem"],
           lw["wo_mem"]]
    return pl.pallas_call(
        _attn_kernel,
        grid=(n // tm,),
        in_specs=[col(0), col(0), col(0), col(0), col(6), col(7),
                  pl.BlockSpec((MEM_LEN, 2 * D), lambda i: (i // tps, 0))] + [full(w) for w in wts],
        out_specs=col(0),
        out_shape=jax.ShapeDtypeStruct((n, D), F32),
        compiler_params=_cparams(("parallel",)),
        name="attn",
    )(x2, ya, og, yb, p, p, kv, *wts)


MOE_TM = 512
MOE_TP = MOE_TM + LANES
MOE_T = 512
RUN_ALIGN = 16
POS_ROW = N_GROUPS + N_EXPERTS
RUN_BITS = tuple(range((MOE_TM // RUN_ALIGN).bit_length() - 1, -1, -1))


def _router_kernel(x_ref, g_ref, rw_ref, rb_ref, hn_ref, meta_ref, pos_ref, cnt_ref):
    tm = x_ref.shape[0]
    hn = _rms(x_ref[...], g_ref[...])
    hi, lo = _split_bf16(hn)
    hn_ref[...] = hi
    lt = _dot_nt(rw_ref[0], hi) + _dot_nt(rw_ref[0], lo) + _dot_nt(rw_ref[1], hi) + rb_ref[...]
    row = lax.broadcasted_iota(jnp.int32, lt.shape, 0).astype(F32)
    neg = jnp.float32(-jnp.inf)
    big = jnp.float32(1 << 20)
    is_g = row < N_GROUPS
    lg = jnp.where(is_g, lt, neg)
    gmax = jnp.max(lg, axis=0, keepdims=True)
    ge = jnp.where(is_g, jnp.exp(lg - gmax), 0.0)
    gp = ge / jnp.sum(ge, axis=0, keepdims=True)
    g_w = jnp.max(gp, axis=0, keepdims=True)
    g_i = jnp.min(jnp.where(is_g & (gp == g_w), row, big), axis=0, keepdims=True)
    first = N_GROUPS + g_i * EPG
    sel = (row >= first) & (row < first + EPG)
    le = jnp.where(sel, lt, neg)
    emax = jnp.max(le, axis=0, keepdims=True)
    ee = jnp.where(sel, jnp.exp(le - emax), 0.0)
    ep = ee / jnp.sum(ee, axis=0, keepdims=True)
    p1 = jnp.max(ep, axis=0, keepdims=True)
    i1 = jnp.min(jnp.where(sel & (ep == p1), row, big), axis=0, keepdims=True)
    rest = sel & (row != i1)
    p2 = jnp.max(jnp.where(rest, ep, -1.0), axis=0, keepdims=True)
    i2 = jnp.min(jnp.where(rest & (ep == p2), row, big), axis=0, keepdims=True)
    tot = p1 + p2
    cw = jnp.where(row == i1, g_w * (p1 / tot), 0.0) + jnp.where(row == i2, g_w * (p2 / tot), 0.0)

    grow = lax.broadcasted_iota(jnp.int32, (8, tm), 0).astype(F32)
    onehot = jnp.where(grow == g_i, 1.0, 0.0)
    before = (lax.broadcasted_iota(jnp.int32, (tm, tm), 0)
              < lax.broadcasted_iota(jnp.int32, (tm, tm), 1)).astype(BF16)
    rank = _dot(onehot.astype(BF16), before)
    count = jnp.sum(onehot, axis=1, keepdims=True)
    run = jnp.floor((count + (RUN_ALIGN - 1)) * (1.0 / RUN_ALIGN)) * RUN_ALIGN
    gcol = lax.broadcasted_iota(jnp.int32, (8, 1), 0)
    start = jnp.zeros((8, 1), F32)
    acc = jnp.zeros((1, 1), F32)
    for g in range(1, N_GROUPS):
        acc = acc + run[g - 1:g, :]
        start = start + jnp.where(gcol == g, acc, 0.0)
    pos = jnp.sum(onehot * (start + rank), axis=0, keepdims=True)
    meta_ref[...] = (cw + jnp.where(row == POS_ROW, pos, 0.0)).T
    pos_ref[0] = jnp.broadcast_to(pos, (8, tm))
    cnt_ref[0] = jnp.broadcast_to(run, (8, LANES)).astype(jnp.int32)


def _router(x2, lw):
    n = x2.shape[0]
    nt = n // MOE_TM
    return pl.pallas_call(
        _router_kernel,
        grid=(nt,),
        in_specs=[pl.BlockSpec((MOE_TM, D), lambda i: (i, 0)),
                  pl.BlockSpec((1, D), lambda i: (0, 0)),
                  pl.BlockSpec((2, LANES, D), lambda i: (0, 0, 0)),
                  pl.BlockSpec((LANES, 1), lambda i: (0, 0))],
        out_specs=[pl.BlockSpec((MOE_TM, D), lambda i: (i, 0)),
                   pl.BlockSpec((MOE_TM, LANES), lambda i: (i, 0)),
                   pl.BlockSpec((1, 8, MOE_TM), lambda i: (i, 0, 0)),
                   pl.BlockSpec((1, 8, LANES), lambda i: (i, 0, 0))],
        out_shape=[jax.ShapeDtypeStruct((n, D), BF16), jax.ShapeDtypeStruct((n, LANES), F32),
                   jax.ShapeDtypeStruct((nt, 8, MOE_TM), F32),
                   jax.ShapeDtypeStruct((nt, 8, LANES), jnp.int32)],
        compiler_params=_cparams(("parallel",)),
        name="router",
    )(x2, lw["norm_ffn"], lw["router_wt"], lw["router_bt"])


def _run_copies(lens_ref, dst_ref, tile, make_copy):
    out = []
    local = 0
    for g in range(N_GROUPS):
        length = lens_ref[tile * N_GROUPS + g]
        base = dst_ref[tile * N_GROUPS + g]
        for bit in RUN_BITS:
            rows = RUN_ALIGN << bit
            done = length & (-2 * rows)
            cond = (length & rows) != 0
            out.append((cond, make_copy(pl.multiple_of(local + done, RUN_ALIGN),
                                        pl.multiple_of(base + done, RUN_ALIGN), rows)))
        local = local + length
    return out


def _dispatch_kernel(lens_ref, dst_ref, end_ref, hn_ref, meta_ref, pos_ref, xs_ref, ms_ref,
                     comp_ref, compm_ref, zx_ref, zm_ref, sem):
    i = pl.program_id(0)
    tm = hn_ref.shape[0]
    slot = lax.broadcasted_iota(jnp.int32, (MOE_TP, tm), 0).astype(F32)
    perm = jnp.where(slot == pos_ref[0][0:1, :], 1.0, 0.0).astype(BF16)
    comp_ref[...] = _dot(perm, hn_ref[...]).astype(BF16)
    m = meta_ref[...]
    m1 = m.astype(BF16)
    r1 = m - m1.astype(F32)
    m2 = r1.astype(BF16)
    m3 = (r1 - m2.astype(F32)).astype(BF16)
    compm_ref[...] = _dot(perm, m1) + _dot(perm, m2) + _dot(perm, m3)

    def make_copy(local, sorted_row, rows):
        return (pltpu.make_async_copy(comp_ref.at[pl.ds(local, rows)],
                                      xs_ref.at[pl.ds(sorted_row, rows)], sem.at[0]),
                pltpu.make_async_copy(compm_ref.at[pl.ds(local, rows)],
                                      ms_ref.at[pl.ds(sorted_row, rows)], sem.at[1]))

    copies = _run_copies(lens_ref, dst_ref, i, make_copy)
    for cond, (cx, cm) in copies:
        @pl.when(cond)
        def _(cx=cx, cm=cm):
            cx.start()
            cm.start()
    for cond, (cx, cm) in copies:
        @pl.when(cond)
        def _(cx=cx, cm=cm):
            cx.wait()
            cm.wait()

    @pl.when(i == pl.num_programs(0) - 1)
    def _():
        zx_ref[...] = jnp.zeros_like(zx_ref)
        zm_ref[...] = jnp.zeros_like(zm_ref)
        fills = []
        for g in range(N_GROUPS):
            row0 = pl.multiple_of(end_ref[g], RUN_ALIGN)
            fills.append(pltpu.make_async_copy(zx_ref, xs_ref.at[pl.ds(row0, MOE_T)], sem.at[0]))
            fills.append(pltpu.make_async_copy(zm_ref, ms_ref.at[pl.ds(row0, MOE_T)], sem.at[1]))
        for c in fills:
            c.start()
        for c in fills:
            c.wait()


def _dispatch(hn, meta, pos, lens, dst, end, cap):
    n = hn.shape[0]
    nt = n // MOE_TM
    return pl.pallas_call(
        _dispatch_kernel,
        grid_spec=pltpu.PrefetchScalarGridSpec(
            num_scalar_prefetch=3,
            grid=(nt,),
            in_specs=[pl.BlockSpec((MOE_TM, D), lambda i, *_: (i, 0)),
                      pl.BlockSpec((MOE_TM, LANES), lambda i, *_: (i, 0)),
                      pl.BlockSpec((1, 8, MOE_TM), lambda i, *_: (i, 0, 0))],
            out_specs=[pl.BlockSpec(memory_space=pl.ANY), pl.BlockSpec(memory_space=pl.ANY)],
            scratch_shapes=[pltpu.VMEM((MOE_TP, D), BF16), pltpu.VMEM((MOE_TP, LANES), F32),
                            pltpu.VMEM((MOE_T, D), BF16), pltpu.VMEM((MOE_T, LANES), F32),
                            pltpu.SemaphoreType.DMA((2,))]),
        out_shape=[jax.ShapeDtypeStruct((N_GROUPS * cap, D), BF16),
                   jax.ShapeDtypeStruct((N_GROUPS * cap, LANES), F32)],
        compiler_params=_cparams(("arbitrary",)),
        name="dispatch",
    )(lens, dst, end, hn, meta, pos)


def _experts_kernel(tgrp_ref, tblk_ref, nvalid_ref, xs_ref, ms_ref, wg_ref, wu_ref, wd_ref, ys_ref,
                    acc_ref):
    j = pl.program_id(0)
    e = pl.program_id(1)

    @pl.when(j < nvalid_ref[0])
    def _():
        @pl.when(e == 0)
        def _():
            acc_ref[...] = jnp.zeros_like(acc_ref)

        xs = xs_ref[...]
        gate = _dot(xs, wg_ref[0])
        he = (gate * _sigmoid(gate) * _dot(xs, wu_ref[0])).astype(BF16)
        lane = lax.broadcasted_iota(jnp.int32, ms_ref.shape, 1)
        col = N_GROUPS + tgrp_ref[j] * EPG + e
        cw = jnp.sum(jnp.where(lane == col, ms_ref[...], 0.0), axis=-1, keepdims=True)
        acc_ref[...] += cw * _dot(he, wd_ref[0])

        @pl.when(e == EPG - 1)
        def _():
            ys_ref[...] = acc_ref[...].astype(BF16)


def _experts(xs, ms, tgrp, tblk, nvalid, lw, n_tiles):
    wmap = lambda j, e, tgrp, tblk, nvalid: (tgrp[j] * EPG + e, 0, 0)
    rmap = lambda j, e, tgrp, tblk, nvalid: (tblk[j], 0)
    return pl.pallas_call(
        _experts_kernel,
        grid_spec=pltpu.PrefetchScalarGridSpec(
            num_scalar_prefetch=3,
            grid=(n_tiles, EPG),
            in_specs=[pl.BlockSpec((MOE_T, D), rmap), pl.BlockSpec((MOE_T, LANES), rmap),
                      pl.BlockSpec((1, D, D_EXPERT), wmap), pl.BlockSpec((1, D, D_EXPERT), wmap),
                      pl.BlockSpec((1, D_EXPERT, D), wmap)],
            out_specs=pl.BlockSpec((MOE_T, D), rmap),
            scratch_shapes=[pltpu.VMEM((MOE_T, D), F32)]),
        out_shape=jax.ShapeDtypeStruct(xs.shape, BF16),
        compiler_params=_cparams(("arbitrary", "arbitrary")),
        name="experts",
    )(tgrp, tblk, nvalid, xs, ms, lw["w_e_gate"], lw["w_e_up"], lw["w_e_down"])


def _combine_kernel(final_norm, lens_ref, dst_ref, x_ref, meta_ref, ys_ref, gf_ref, o_ref, comp_ref,
                    sem):
    i = pl.program_id(0)
    tm = x_ref.shape[0]
    comp_ref[...] = jnp.zeros_like(comp_ref)

    def make_copy(local, sorted_row, rows):
        return pltpu.make_async_copy(ys_ref.at[pl.ds(sorted_row, rows)],
                                     comp_ref.at[pl.ds(local, rows)], sem.at[0])

    copies = _run_copies(lens_ref, dst_ref, i, make_copy)
    for cond, c in copies:
        pl.when(cond)(c.start)
    for cond, c in copies:
        pl.when(cond)(c.wait)

    lane = lax.broadcasted_iota(jnp.int32, meta_ref.shape, 1)
    pos = jnp.sum(jnp.where(lane == POS_ROW, meta_ref[...], 0.0), axis=-1, keepdims=True)
    slot = lax.broadcasted_iota(jnp.int32, (tm, MOE_TP), 1).astype(F32)
    unperm = jnp.where(slot == pos, 1.0, 0.0).astype(BF16)
    out = x_ref[...] + _dot(unperm, comp_ref[...])
    if final_norm:
        out = _rms(out, gf_ref[...])
    o_ref[...] = out


def _combine(x2, meta, ys, lens, dst, norm_f, final_norm):
    n = x2.shape[0]
    return pl.pallas_call(
        functools.partial(_combine_kernel, final_norm),
        grid_spec=pltpu.PrefetchScalarGridSpec(
            num_scalar_prefetch=2,
            grid=(n // MOE_TM,),
            in_specs=[pl.BlockSpec((MOE_TM, D), lambda i, *_: (i, 0)),
                      pl.BlockSpec((MOE_TM, LANES), lambda i, *_: (i, 0)),
                      pl.BlockSpec(memory_space=pl.ANY),
                      pl.BlockSpec((1, D), lambda i, *_: (0, 0))],
            out_specs=pl.BlockSpec((MOE_TM, D), lambda i, *_: (i, 0)),
            scratch_shapes=[pltpu.VMEM((MOE_TP, D), BF16), pltpu.SemaphoreType.DMA((1,))]),
        out_shape=jax.ShapeDtypeStruct((n, D), F32),
        compiler_params=_cparams(("arbitrary",)),
        name="combine",
    )(lens, dst, x2, meta, ys, norm_f)


def _moe(x2, lw, norm_f, final_norm):
    n = x2.shape[0]
    nt = n // MOE_TM
    cap = -(-(n + RUN_ALIGN * nt + MOE_T) // MOE_T) * MOE_T
    n_tiles = (n + N_GROUPS * RUN_ALIGN * nt) // MOE_T + N_GROUPS + 1
    hn, meta, pos, cnt = _router(x2, lw)

    lens = cnt[:, :N_GROUPS, 0]
    ends = jnp.cumsum(lens, axis=0)
    dst = (ends - lens) + (jnp.arange(N_GROUPS, dtype=jnp.int32) * cap)[None, :]
    total = ends[-1]
    end = total + jnp.arange(N_GROUPS, dtype=jnp.int32) * cap
    tiles = (total + MOE_T - 1) // MOE_T
    tile_end = jnp.cumsum(tiles)
    tile_start = tile_end - tiles
    nvalid = tile_end[-1]
    jj = jnp.minimum(jnp.arange(n_tiles, dtype=jnp.int32), nvalid - 1)
    tgrp = jnp.sum((jj[:, None] >= tile_end[None, :]).astype(jnp.int32), axis=1)
    tblk = tgrp * (cap // MOE_T) + jj - tile_start[tgrp]
    lens_f = lens.reshape(-1).astype(jnp.int32)
    dst_f = dst.reshape(-1).astype(jnp.int32)

    xs, ms = _dispatch(hn, meta, pos, lens_f, dst_f, end.astype(jnp.int32), cap)
    ys = _experts(xs, ms, tgrp.astype(jnp.int32), tblk.astype(jnp.int32),
                  nvalid.reshape(1).astype(jnp.int32), lw, n_tiles)
    return _combine(x2, meta, ys, lens_f, dst_f, norm_f, final_norm)


def _layer_weights(l, w):
    rw = 3 * D
    n_lo = LORA_DECAY + LORA_AAA + LORA_GATE
    w_in = w["w_in"][l]
    lora_cols = [w_in[:, rw:rw + n_lo]]
    mu_lora = [w["mu_in"][l][rw:rw + n_lo]]
    if l > 0:
        lora_cols.append(w["w_vres_in"][l - 1])
        mu_lora.append(w["mu_vres"][l - 1])
    lora = jnp.concatenate(lora_cols, axis=1)
    lora = jnp.pad(lora, ((0, 0), (0, LORA_W - lora.shape[1])))
    mu_lora = jnp.concatenate(mu_lora)
    mu_lora = jnp.pad(mu_lora, (0, LORA_W - mu_lora.shape[0]))
    w_all = jnp.concatenate([w_in[:, :rw], w_in[:, rw + n_lo:], lora], axis=1).astype(BF16)

    wup_a = jnp.zeros((LANES, 2 * D), F32)
    wup_a = wup_a.at[:LORA_DECAY, :D].set(w["w_decay_up"][l])
    wup_a = wup_a.at[LORA_DECAY:LORA_DECAY + LORA_AAA, D:].set(w["w_aaa_up"][l])
    wup_b = jnp.zeros((2 * LANES, 2 * D), F32)
    wup_b = wup_b.at[:LORA_GATE, :D].set(w["w_gate_up"][l])
    if l > 0:
        wup_b = wup_b.at[LORA_GATE:LORA_GATE + LORA_VRES, D:].set(w["w_vres_up"][l - 1])

    router = jnp.concatenate([w["router_g"][l], w["router_e"][l]], axis=1).T
    router = jnp.pad(router, ((0, LANES - router.shape[0]), (0, 0)))
    r_hi = router.astype(BF16)
    r_lo = (router - r_hi.astype(F32)).astype(BF16)
    router_b = jnp.concatenate([w["router_g_b"][l], w["router_e_b"][l]])
    router_b = jnp.pad(router_b, (0, LANES - router_b.shape[0]))[:, None]

    row = lambda a: a.reshape(1, -1)
    out = {
        "norm_mix": row(w["norm_mix"][l]), "w_all": w_all,
        "mu_rkv": row(w["mu_in"][l][:rw]), "mu_lora": row(mu_lora),
        "wup_a": wup_a.astype(BF16), "wup_b": wup_b.astype(BF16),
        "w0": row(w["w0"][l]), "a0": row(w["a0"][l]), "k_k": row(w["k_k"][l]), "k_a": row(w["k_a"][l]),
        "conv_w": w["conv_w"][l],
        "r_k": row(w["r_k"][l]), "lnx_g": row(w["lnx_g"][l]), "lnx_b": row(w["lnx_b"][l]),
        "gate_b": row(w["gate_b"][l]),
        "w_br_a": w["w_br_a"][l].astype(BF16), "w_br_b": w["w_br_b"][l].astype(BF16),
        "w_mix_out": w["w_mix_out"][l].astype(BF16),
        "norm_memq": row(w["norm_memq"][l]), "norm_memkv": row(w["norm_memkv"][l]),
        "wq_mem": w["wq_mem"][l].astype(BF16), "wkv_mem": w["wkv_mem"][l].astype(BF16),
        "wo_mem": w["wo_mem"][l].astype(BF16),
        "norm_ffn": row(w["norm_ffn"][l]),
        "router_wt": jnp.stack([r_hi, r_lo]), "router_bt": router_b,
        "w_e_gate": w["w_e_gate"][l].astype(BF16), "w_e_up": w["w_e_up"][l].astype(BF16),
        "w_e_down": w["w_e_down"][l].astype(BF16),
    }
    if l > 0:
        out["v0"] = row(w["v0"][l - 1])
    return out


def _tile(total, want):
    t = min(want, total)
    assert total % t == 0, (total, t)
    return t


def kernel(x, mem, norm_mix, w_in, mu_in, w_vres_in, mu_vres, w0, w_decay_up, a0, w_aaa_up, w_gate_up,
           v0, w_vres_up, k_k, k_a, r_k, lnx_g, lnx_b, conv_w, gate_b, w_br_a, w_br_b, w_mix_out,
           norm_memq, norm_memkv, wq_mem, wkv_mem, wo_mem, norm_ffn, router_g, router_g_b, router_e,
           router_e_b, w_e_gate, w_e_up, w_e_down, norm_f):
    w = dict(norm_mix=norm_mix, w_in=w_in, mu_in=mu_in, w_vres_in=w_vres_in, mu_vres=mu_vres, w0=w0,
             w_decay_up=w_decay_up, a0=a0, w_aaa_up=w_aaa_up, w_gate_up=w_gate_up, v0=v0,
             w_vres_up=w_vres_up, k_k=k_k, k_a=k_a, r_k=r_k, lnx_g=lnx_g, lnx_b=lnx_b, conv_w=conv_w,
             gate_b=gate_b, w_br_a=w_br_a, w_br_b=w_br_b, w_mix_out=w_mix_out, norm_memq=norm_memq,
             norm_memkv=norm_memkv, wq_mem=wq_mem, wkv_mem=wkv_mem, wo_mem=wo_mem, norm_ffn=norm_ffn,
             router_g=router_g, router_g_b=router_g_b, router_e=router_e, router_e_b=router_e_b,
             w_e_gate=w_e_gate, w_e_up=w_e_up, w_e_down=w_e_down)
    bsz, seq, _ = x.shape
    depth = norm_mix.shape[0]
    n = bsz * seq
    assert seq % CHUNK == 0 and mem.shape[1] == MEM_LEN
    x2 = x.reshape(n, D)
    mem2 = mem.reshape(bsz * MEM_LEN, D)
    tm_in = _tile(n, 1024)
    tm_prep = _tile(seq, 256)
    tc_wkv = _tile(seq, 256)
    tm_attn = _tile(seq, 512)
    assert n % MOE_TM == 0
    norm_f2 = norm_f.reshape(1, D)

    v_first = None
    for l in range(depth):
        lw = _layer_weights(l, w)
        p = _inproj(x2, lw["norm_mix"], lw["w_all"], tm_in, IN_W // 4)
        r, k, v, kk, aa, lwd, og, yb = _prep(p, v_first, lw, tm_prep, seq)
        if l == 0:
            v_first = v
        ya = _wkv(r, k, v, kk, aa, lwd, lw, bsz, seq, tc_wkv)
        kv = _memkv(mem2, lw["norm_memkv"], lw["wkv_mem"])
        x2 = _attn(x2, ya, og, yb, p, kv, lw, tm_attn, seq)
        x2 = _moe(x2, lw, norm_f2, l == depth - 1)
    return x2.reshape(bsz, seq, D)
```

```python
import functools
import math

import jax
import jax.numpy as jnp
from jax import lax
from jax.experimental import pallas as pl
from jax.experimental.pallas import tpu as pltpu

F32 = jnp.float32
BF16 = jnp.bfloat16

D = 1024
HEAD = 64
N_HEADS = 16
LANES = 128
N_PAIRS = D // LANES
CHUNK = 64
WKV_UNROLL = 2
LORA_DECAY, LORA_AAA, LORA_GATE, LORA_VRES = 64, 64, 160, 32
LORA_W = 512
CONV_K = 3
MEM_LEN = 256
MEM_HEADS = 4
MEM_HD = D // MEM_HEADS
N_GROUPS, EPG, N_EXPERTS, D_EXPERT = 4, 4, 16, 512
RMS_EPS, LNX_EPS, L2_EPS = 1e-6, 64e-5, 1e-12
VMEM_LIMIT = 56 * 1024 * 1024


def _cparams(sem):
    return pltpu.CompilerParams(dimension_semantics=sem, vmem_limit_bytes=VMEM_LIMIT)


def _rms(x, g):
    return x * lax.rsqrt(jnp.mean(x * x, axis=-1, keepdims=True) + RMS_EPS) * g


def _sigmoid(x):
    return 0.5 * jnp.tanh(0.5 * x) + 0.5


def _dot(a, b):
    return jnp.dot(a, b, preferred_element_type=F32)


def _dot_nt(a, b):
    return lax.dot_general(a, b, (((1,), (1,)), ((), ())), preferred_element_type=F32)


def _split_bf16(x):
    hi = x.astype(BF16)
    lo = (x - hi.astype(F32)).astype(BF16)
    return hi, lo


(STEP_LORA, STEP_R, STEP_K, STEP_V, STEP_CX, STEP_CC, STEP_CB, STEP_GA, STEP_GB) = range(9)
N_STEPS = 9


def _mixin_kernel(has_vres, tiles_per_seq, *refs):
    n_in = 17 if has_vres else 14
    ins, outs, scr = refs[:n_in], refs[n_in:n_in + 10], refs[n_in + 10:]
    (x_ref, g_ref, wm_ref, wl_ref, mu_ref, mul_ref, wdec_ref, waaa_ref, wgate_ref, w0_ref, a0_ref,
     kk_ref, ka_ref, cw_ref) = ins[:14]
    ro_ref, ko_ref, vo_ref, kko_ref, aao_ref, lwo_ref, ogo_ref, ybo_ref, gao_ref, gbo_ref = outs
    xn_ref, act_ref, cv_ref, clo_ref, crkv_ref, cu_ref = scr
    if has_vres:
        vf_ref, wvres_ref, v0_ref = ins[14:]
    j = pl.program_id(1)
    tm = x_ref.shape[0]

    def mixed(z, carry, mu):
        return z + (_shift_rows(z, carry, 1) - z) * mu

    def block():
        return _dot(xn_ref[...], wm_ref[...])

    def shifted_block(idx):
        z = block()
        cols = slice(idx * D, (idx + 1) * D)
        zm = mixed(z, crkv_ref[:, cols], mu_ref[:, cols])
        crkv_ref[:, cols] = z[tm - 8:, :]
        return zm

    @pl.when(j == STEP_LORA)
    def _():
        @pl.when(pl.program_id(0) % tiles_per_seq == 0)
        def _():
            crkv_ref[...] = jnp.zeros_like(crkv_ref)
            clo_ref[...] = jnp.zeros_like(clo_ref)
            cu_ref[...] = jnp.zeros_like(cu_ref)

        xn = _rms(x_ref[...], g_ref[...]).astype(BF16)
        xn_ref[...] = xn
        lo = _dot(xn, wl_ref[...])
        lom = mixed(lo, clo_ref[...], mul_ref[...])
        clo_ref[...] = lo[tm - 8:, :]
        lane = lax.broadcasted_iota(jnp.int32, (tm, LANES), 1)
        t0, t1, t2 = (lom[:, t * LANES:(t + 1) * LANES] for t in range(3))
        gate_tail = LORA_DECAY + LORA_AAA + LORA_GATE - 2 * LANES
        act_ref[...] = jnp.concatenate(
            [jnp.where(lane < LORA_DECAY, jnp.tanh(t0), t0), _sigmoid(t1),
             jnp.where(lane < gate_tail, _sigmoid(t2), t2)], axis=1).astype(BF16)

    @pl.when(j == STEP_R)
    def _():
        u = w0_ref[...] + _dot(act_ref[:, :LANES], wdec_ref[...])
        lwo_ref[...] = -math.exp(-0.5) * _sigmoid(u)
        ro_ref[...] = shifted_block(0).astype(BF16)

    @pl.when(j == STEP_K)
    def _():
        aa = _sigmoid(a0_ref[...] + _dot(act_ref[:, :LANES], waaa_ref[...]))
        aao_ref[...] = aa.astype(BF16)
        zm = shifted_block(1)
        kko_ref[...] = (zm * kk_ref[...]).astype(BF16)
        ko_ref[...] = (zm * (1.0 + (aa - 1.0) * ka_ref[...])).astype(BF16)

    @pl.when(j == STEP_V)
    def _():
        if has_vres:
            vgate = _sigmoid(v0_ref[...] + _dot(act_ref[:, LANES:], wvres_ref[...]))
        zm = shifted_block(2)
        if has_vres:
            zm = zm + (vf_ref[...].astype(F32) - zm) * vgate
        vo_ref[...] = zm.astype(BF16)

    @pl.when(j == STEP_CX)
    def _():
        ogo_ref[...] = _dot(act_ref[:, LANES:], wgate_ref[...]).astype(BF16)
        cv_ref[...] = block()

    @pl.when(j == STEP_CC)
    def _():
        u = block() * cv_ref[...]
        cu = cu_ref[...]
        conv = cw_ref[CONV_K - 1:CONV_K, :] * u
        for t in range(CONV_K - 1):
            conv = conv + cw_ref[t:t + 1, :] * _shift_rows(u, cu, CONV_K - 1 - t)
        cu_ref[...] = u[tm - 8:, :]
        cv_ref[...] = conv

    @pl.when(j == STEP_CB)
    def _():
        ybo_ref[...] = (block() * cv_ref[...]).astype(BF16)

    @pl.when(j == STEP_GA)
    def _():
        gao_ref[...] = block().astype(BF16)

    @pl.when(j == STEP_GB)
    def _():
        gbo_ref[...] = block().astype(BF16)


def _mixin(x2, vfirst, lw, tm, seq):
    n = x2.shape[0]
    has_vres = vfirst is not None
    tile = pl.BlockSpec((tm, D), lambda i, j: (i, 0))
    full = lambda a: pl.BlockSpec(a.shape, lambda i, j: (0,) * a.ndim)
    names = ["norm_mix", None, "w_lora", "mu_rkv", "mu_lora", "wup_dec", "wup_aaa", "wup_gate", "w0", "a0",
             "k_k", "k_a", "conv_w"]
    in_specs = [tile]
    args = [x2]
    for name in names:
        if name is None:
            in_specs.append(pl.BlockSpec((D, D), lambda i, j: (0, jnp.maximum(j - 1, 0))))
            args.append(lw["w_main"])
        else:
            in_specs.append(full(lw[name]))
            args.append(lw[name])
    scratch = [pltpu.VMEM((tm, D), BF16), pltpu.VMEM((tm, 3 * LANES), BF16), pltpu.VMEM((tm, D), F32),
               pltpu.VMEM((8, LORA_W), F32), pltpu.VMEM((8, 3 * D), F32), pltpu.VMEM((8, D), F32)]
    if has_vres:
        in_specs += [tile, full(lw["wup_vres"]), full(lw["v0"])]
        args += [vfirst, lw["wup_vres"], lw["v0"]]
    bf = jax.ShapeDtypeStruct((n, D), BF16)
    out_shape = [bf, bf, bf, bf, bf, jax.ShapeDtypeStruct((n, D), F32), bf, bf, bf, bf]
    return pl.pallas_call(
        functools.partial(_mixin_kernel, has_vres, seq // tm),
        grid=(n // tm, N_STEPS),
        in_specs=in_specs,
        out_specs=[tile] * 10,
        out_shape=out_shape,
        scratch_shapes=scratch,
        compiler_params=_cparams(("arbitrary", "arbitrary")),
        name="mixin",
    )(*args)


def _shift_rows(z, prev_rows, k):
    tm = z.shape[0]
    rolled = pltpu.roll(z, k, 0)
    rows = lax.broadcasted_iota(jnp.int32, z.shape, 0)
    out = rolled
    for j in range(k):
        src = prev_rows[8 - k + j:8 - k + j + 1, :]
        out = jnp.where(rows == j, src, out)
    return out


def _pair_expand(x, m0, m1):
    return jnp.concatenate([jnp.where(m0, x, 0.0), jnp.where(m1, x, 0.0)], axis=0)


def _halves_sum(x, m0):
    s0 = jnp.sum(jnp.where(m0, x, 0.0), axis=-1, keepdims=True)
    s1 = jnp.sum(jnp.where(m0, 0.0, x), axis=-1, keepdims=True)
    return jnp.where(m0, s0, s1)


def _wkv_kernel(r_ref, k_ref, v_ref, kk_ref, aa_ref, lw_ref, rk_ref, g_ref, b_ref, o_ref, s_ref):
    @pl.when(pl.program_id(1) == 0)
    def _():
        s_ref[...] = jnp.zeros_like(s_ref)

    c2 = 2 * CHUNK
    lane = lax.broadcasted_iota(jnp.int32, (CHUNK, LANES), 1)
    m0 = lane < HEAD
    m1 = jnp.logical_not(m0)
    rows = lax.broadcasted_iota(jnp.int32, (c2, c2), 0)
    cols = lax.broadcasted_iota(jnp.int32, (c2, c2), 1)
    strict = cols < rows
    incl = cols <= rows
    eye = cols == rows
    tri = (lax.broadcasted_iota(jnp.int32, (CHUNK, CHUNK), 1)
           <= lax.broadcasted_iota(jnp.int32, (CHUNK, CHUNK), 0)).astype(BF16)
    n_chunks = r_ref.shape[0] // CHUNK

    def chunk_body(ci, carry):
        t0s = [pl.multiple_of((ci * WKV_UNROLL + u) * CHUNK, CHUNK) for u in range(WKV_UNROLL)]
        probs = [(t0, slice(pi * LANES, (pi + 1) * LANES)) for t0 in t0s for pi in range(N_PAIRS)]
        cat0 = lambda *xs: jnp.concatenate(xs, axis=0)
        cat1 = lambda *xs: jnp.concatenate(xs, axis=1)
        expand = lambda x: _pair_expand(x, m0, m1)
        ld = lambda ref, p: ref[pl.ds(p[0], CHUNK), p[1]].astype(F32)

        lw_all = [lw_ref[pl.ds(t0, CHUNK), :] for t0 in t0s]
        cum_all = []
        for lw_u in lw_all:
            lw_hi, lw_lo = _split_bf16(lw_u)
            cum_all.append(_dot(tri, lw_hi) + _dot(tri, lw_lo))
        lw = [lw_u[:, pi * LANES:(pi + 1) * LANES] for lw_u in lw_all for pi in range(N_PAIRS)]
        cum = [c_u[:, pi * LANES:(pi + 1) * LANES] for c_u in cum_all for pi in range(N_PAIRS)]
        cum_end = [c[CHUNK - 1:CHUNK, :] for c in cum]
        w_inv = [jnp.exp(-c) for c in cum]
        w_end = [jnp.exp(ce - c) for c, ce in zip(cum, cum_end)]
        r = [ld(r_ref, p) for p in probs]
        k = [ld(k_ref, p) for p in probs]
        kk = [ld(kk_ref, p) for p in probs]
        kk = [x / jnp.maximum(jnp.sqrt(_halves_sum(x * x, m0)), L2_EPS) for x in kk]
        b = [x * ld(aa_ref, p) for x, p in zip(kk, probs)]
        a_t = [expand(-x * jnp.exp(c - l)) for x, c, l in zip(kk, cum, lw)]
        r_t = [expand(x * jnp.exp(c)) for x, c in zip(r, cum)]
        lhs1 = [cat0(a, rr).astype(BF16) for a, rr in zip(a_t, r_t)]
        rhs1 = [cat0(expand(bb * wi), expand(kx * wi)).astype(BF16) for bb, kx, wi in zip(b, k, w_inv)]
        aa_m = [_dot_nt(x, y) for x, y in zip(lhs1, rhs1)]
        a_ab = [jnp.where(strict, m[:c2, :c2], 0.0) for m in aa_m]
        a_ak = [jnp.where(strict, m[:c2, c2:], 0.0).astype(BF16) for m in aa_m]
        lhs_top = [cat1(jnp.where(incl, m[c2:, :c2], 0.0), jnp.where(incl, m[c2:, c2:], 0.0)).astype(BF16)
                   for m in aa_m]

        t_m = [jnp.where(eye, 1.0, 0.0) + m for m in a_ab]
        a_b = [m.astype(BF16) for m in a_ab]
        a_m = [_dot(x, x) for x in a_b]
        n_steps = CHUNK.bit_length() - 1
        for step in range(1, n_steps):
            a_b = [m.astype(BF16) for m in a_m]
            if step < n_steps - 1:
                both = [_dot(x, cat1(x, t.astype(BF16))) for x, t in zip(a_b, t_m)]
                a_m = [m[:, :c2] for m in both]
                t_m = [t + m[:, c2:] for t, m in zip(t_m, both)]
            else:
                t_m = [t + _dot(x, t.astype(BF16)) for x, t in zip(a_b, t_m)]

        v = [ld(v_ref, p) for p in probs]
        v_b = [expand(x).astype(BF16) for x in v]
        akv = [_dot(x, y) for x, y in zip(a_ak, v_b)]
        tx = [_dot(t.astype(BF16), cat1(a, y).astype(BF16)) for t, a, y in zip(t_m, a_t, akv)]
        rhs2 = [cat0(x.astype(BF16), cat1(jnp.zeros((c2, c2), BF16), y)) for x, y in zip(tx, v_b)]
        lhs_bot = [cat1(expand(bb * we).T, expand(kx * we).T).astype(BF16)
                   for bb, kx, we in zip(b, k, w_end)]
        z = [_dot(cat0(x, y), w) for x, y, w in zip(lhs_top, lhs_bot, rhs2)]
        gp = [cat0(rt + m[:c2, :c2], m[c2:, :c2] + jnp.where(eye, jnp.exp(ce), 0.0)).astype(BF16)
              for rt, m, ce in zip(r_t, z, cum_end)]

        state = [s_ref[pi] for pi in range(N_PAIRS)]
        y = []
        for u in range(WKV_UNROLL):
            base = u * N_PAIRS
            zz = [_dot(gp[base + pi], state[pi].astype(BF16)) for pi in range(N_PAIRS)]
            state = [zz[pi][c2:] + z[base + pi][c2:, c2:] for pi in range(N_PAIRS)]
            y += [(zz[pi][:CHUNK] + zz[pi][CHUNK:c2])
                  + (z[base + pi][:CHUNK, c2:] + z[base + pi][CHUNK:c2, c2:]) for pi in range(N_PAIRS)]
        for pi in range(N_PAIRS):
            s_ref[pi] = state[pi]

        for i, (t0, sl) in enumerate(probs):
            mu = _halves_sum(y[i], m0) * (1.0 / HEAD)
            yc = y[i] - mu
            var = _halves_sum(yc * yc, m0) * (1.0 / HEAD)
            yn = yc * lax.rsqrt(var + LNX_EPS) * g_ref[:, sl] + b_ref[:, sl]
            bonus = _halves_sum(r[i] * k[i] * rk_ref[:, sl], m0) * v[i]
            o_ref[pl.ds(t0, CHUNK), sl] = (yn + bonus).astype(BF16)
        return carry

    assert n_chunks % WKV_UNROLL == 0
    lax.fori_loop(0, n_chunks // WKV_UNROLL, chunk_body, 0)


def _wkv(r, k, v, kk, aa, lw, lwts, bsz, seq, tc):
    n = r.shape[0]
    nt = seq // tc
    blk = pl.BlockSpec((tc, D), lambda b, t: (b * nt + t, 0))
    row = pl.BlockSpec((1, D), lambda b, t: (0, 0))
    return pl.pallas_call(
        _wkv_kernel,
        grid=(bsz, nt),
        in_specs=[blk] * 6 + [row] * 3,
        out_specs=blk,
        out_shape=jax.ShapeDtypeStruct((n, D), BF16),
        scratch_shapes=[pltpu.VMEM((N_PAIRS, LANES, LANES), F32)],
        compiler_params=_cparams(("parallel", "arbitrary")),
        name="wkv",
    )(r, k, v, kk, aa, lw, lwts["r_k"], lwts["lnx_g"], lwts["lnx_b"])


def _memkv_kernel(m_ref, g_ref, w_ref, o_ref):
    o_ref[...] = _dot(_rms(m_ref[...], g_ref[...]).astype(BF16), w_ref[...]).astype(BF16)


def _memkv(mem2, g, w):
    n = mem2.shape[0]
    return pl.pallas_call(
        _memkv_kernel,
        grid=(n // MEM_LEN,),
        in_specs=[pl.BlockSpec((MEM_LEN, D), lambda i: (i, 0)),
                  pl.BlockSpec((1, D), lambda i: (0, 0)),
                  pl.BlockSpec((D, 2 * D), lambda i: (0, 0))],
        out_specs=pl.BlockSpec((MEM_LEN, 2 * D), lambda i: (i, 0)),
        out_shape=jax.ShapeDtypeStruct((n, 2 * D), BF16),
        compiler_params=_cparams(("parallel",)),
        name="memkv",
    )(mem2, g, w)


def _attn_kernel(x_ref, ya_ref, og_ref, yb_ref, ga_ref, gb_ref, kv_ref, gbias_ref, wa_ref, wb_ref,
                 wmix_ref, gq_ref, wq_ref, wo_ref, o_ref):
    y_a = (ya_ref[...].astype(F32) * og_ref[...].astype(F32)).astype(BF16)
    g_a = _sigmoid(ga_ref[...].astype(F32) + gbias_ref[:, :D])
    g_b = _sigmoid(gb_ref[...].astype(F32) + gbias_ref[:, D:])
    merged = g_a * _dot(y_a, wa_ref[...]) + g_b * _dot(yb_ref[...], wb_ref[...])
    x1 = x_ref[...] + _dot(merged.astype(BF16), wmix_ref[...])

    hn = _rms(x1, gq_ref[...]).astype(BF16)
    q = _dot(hn, wq_ref[...]).astype(BF16)
    heads = []
    for h in range(MEM_HEADS):
        sl = slice(h * MEM_HD, (h + 1) * MEM_HD)
        s = _dot_nt(q[:, sl], kv_ref[:, sl]) * (MEM_HD ** -0.5)
        s = s - jnp.max(s, axis=-1, keepdims=True)
        e = jnp.exp(s)
        pr = (e / jnp.sum(e, axis=-1, keepdims=True)).astype(BF16)
        heads.append(_dot(pr, kv_ref[:, D + h * MEM_HD:D + (h + 1) * MEM_HD]).astype(BF16))
    o = jnp.concatenate(heads, axis=1)
    o_ref[...] = x1 + _dot(o, wo_ref[...])


def _attn(x2, ya, og, yb, ga, gb, kv, lw, tm, seq):
    n = x2.shape[0]
    tps = seq // tm
    tile = pl.BlockSpec((tm, D), lambda i: (i, 0))
    full = lambda a: pl.BlockSpec(a.shape, lambda i: (0, 0))
    wts = [lw["gate_b"], lw["w_br_a"], lw["w_br_b"], lw["w_mix_out"], lw["norm_memq"], lw["wq_mem"],
           lw["wo_mem"]]
    return pl.pallas_call(
        _attn_kernel,
        grid=(n // tm,),
        in_specs=[tile] * 6 + [pl.BlockSpec((MEM_LEN, 2 * D), lambda i: (i // tps, 0))]
        + [full(w) for w in wts],
        out_specs=tile,
        out_shape=jax.ShapeDtypeStruct((n, D), F32),
        compiler_params=_cparams(("parallel",)),
        name="attn",
    )(x2, ya, og, yb, ga, gb, kv, *wts)


MOE_TM = 512
MOE_TP = MOE_TM + LANES
MOE_T = 512
RUN_ALIGN = 16
POS_ROW = N_GROUPS + N_EXPERTS
RUN_BITS = tuple(range((MOE_TM // RUN_ALIGN).bit_length() - 1, -1, -1))


def _router_kernel(x_ref, g_ref, rw_ref, rb_ref, hn_ref, meta_ref, pos_ref, cnt_ref):
    tm = x_ref.shape[0]
    hn = _rms(x_ref[...], g_ref[...])
    hi, lo = _split_bf16(hn)
    hn_ref[...] = hi
    lt = _dot_nt(rw_ref[0], hi) + _dot_nt(rw_ref[0], lo) + _dot_nt(rw_ref[1], hi) + rb_ref[...]
    row = lax.broadcasted_iota(jnp.int32, lt.shape, 0).astype(F32)
    neg = jnp.float32(-jnp.inf)
    big = jnp.float32(1 << 20)
    is_g = row < N_GROUPS
    lg = jnp.where(is_g, lt, neg)
    gmax = jnp.max(lg, axis=0, keepdims=True)
    ge = jnp.where(is_g, jnp.exp(lg - gmax), 0.0)
    gp = ge / jnp.sum(ge, axis=0, keepdims=True)
    g_w = jnp.max(gp, axis=0, keepdims=True)
    g_i = jnp.min(jnp.where(is_g & (gp == g_w), row, big), axis=0, keepdims=True)
    first = N_GROUPS + g_i * EPG
    sel = (row >= first) & (row < first + EPG)
    le = jnp.where(sel, lt, neg)
    emax = jnp.max(le, axis=0, keepdims=True)
    ee = jnp.where(sel, jnp.exp(le - emax), 0.0)
    ep = ee / jnp.sum(ee, axis=0, keepdims=True)
    p1 = jnp.max(ep, axis=0, keepdims=True)
    i1 = jnp.min(jnp.where(sel & (ep == p1), row, big), axis=0, keepdims=True)
    rest = sel & (row != i1)
    p2 = jnp.max(jnp.where(rest, ep, -1.0), axis=0, keepdims=True)
    i2 = jnp.min(jnp.where(rest & (ep == p2), row, big), axis=0, keepdims=True)
    tot = p1 + p2
    cw = jnp.where(row == i1, g_w * (p1 / tot), 0.0) + jnp.where(row == i2, g_w * (p2 / tot), 0.0)

    grow = lax.broadcasted_iota(jnp.int32, (8, tm), 0).astype(F32)
    onehot = jnp.where(grow == g_i, 1.0, 0.0)
    before = (lax.broadcasted_iota(jnp.int32, (tm, tm), 0)
              < lax.broadcasted_iota(jnp.int32, (tm, tm), 1)).astype(BF16)
    rank = _dot(onehot.astype(BF16), before)
    count = jnp.sum(onehot, axis=1, keepdims=True)
    run = jnp.floor((count + (RUN_ALIGN - 1)) * (1.0 / RUN_ALIGN)) * RUN_ALIGN
    gcol = lax.broadcasted_iota(jnp.int32, (8, 1), 0)
    start = jnp.zeros((8, 1), F32)
    acc = jnp.zeros((1, 1), F32)
    for g in range(1, N_GROUPS):
        acc = acc + run[g - 1:g, :]
        start = start + jnp.where(gcol == g, acc, 0.0)
    pos = jnp.sum(onehot * (start + rank), axis=0, keepdims=True)
    meta_ref[...] = (cw + jnp.where(row == POS_ROW, pos, 0.0)).T
    pos_ref[0] = jnp.broadcast_to(pos, (8, tm))
    cnt_ref[0] = jnp.broadcast_to(run, (8, LANES)).astype(jnp.int32)


def _router(x2, lw):
    n = x2.shape[0]
    nt = n // MOE_TM
    return pl.pallas_call(
        _router_kernel,
        grid=(nt,),
        in_specs=[pl.BlockSpec((MOE_TM, D), lambda i: (i, 0)),
                  pl.BlockSpec((1, D), lambda i: (0, 0)),
                  pl.BlockSpec((2, LANES, D), lambda i: (0, 0, 0)),
                  pl.BlockSpec((LANES, 1), lambda i: (0, 0))],
        out_specs=[pl.BlockSpec((MOE_TM, D), lambda i: (i, 0)),
                   pl.BlockSpec((MOE_TM, LANES), lambda i: (i, 0)),
                   pl.BlockSpec((1, 8, MOE_TM), lambda i: (i, 0, 0)),
                   pl.BlockSpec((1, 8, LANES), lambda i: (i, 0, 0))],
        out_shape=[jax.ShapeDtypeStruct((n, D), BF16), jax.ShapeDtypeStruct((n, LANES), F32),
                   jax.ShapeDtypeStruct((nt, 8, MOE_TM), F32),
                   jax.ShapeDtypeStruct((nt, 8, LANES), jnp.int32)],
        compiler_params=_cparams(("parallel",)),
        name="router",
    )(x2, lw["norm_ffn"], lw["router_wt"], lw["router_bt"])


def _run_copies(lens_ref, dst_ref, tile, make_copy):
    out = []
    local = 0
    for g in range(N_GROUPS):
        length = lens_ref[tile * N_GROUPS + g]
        base = dst_ref[tile * N_GROUPS + g]
        for bit in RUN_BITS:
            rows = RUN_ALIGN << bit
            done = length & (-2 * rows)
            cond = (length & rows) != 0
            out.append((cond, make_copy(pl.multiple_of(local + done, RUN_ALIGN),
                                        pl.multiple_of(base + done, RUN_ALIGN), rows)))
        local = local + length
    return out


def _dispatch_kernel(lens_ref, dst_ref, end_ref, hn_ref, meta_ref, pos_ref, xs_ref, ms_ref,
                     comp_ref, compm_ref, zx_ref, zm_ref, sem):
    i = pl.program_id(0)
    tm = hn_ref.shape[0]
    slot = lax.broadcasted_iota(jnp.int32, (MOE_TP, tm), 0).astype(F32)
    perm = jnp.where(slot == pos_ref[0][0:1, :], 1.0, 0.0).astype(BF16)
    comp_ref[...] = _dot(perm, hn_ref[...]).astype(BF16)
    m = meta_ref[...]
    m1 = m.astype(BF16)
    r1 = m - m1.astype(F32)
    m2 = r1.astype(BF16)
    m3 = (r1 - m2.astype(F32)).astype(BF16)
    compm_ref[...] = _dot(perm, m1) + _dot(perm, m2) + _dot(perm, m3)

    def make_copy(local, sorted_row, rows):
        return (pltpu.make_async_copy(comp_ref.at[pl.ds(local, rows)],
                                      xs_ref.at[pl.ds(sorted_row, rows)], sem.at[0]),
                pltpu.make_async_copy(compm_ref.at[pl.ds(local, rows)],
                                      ms_ref.at[pl.ds(sorted_row, rows)], sem.at[1]))

    copies = _run_copies(lens_ref, dst_ref, i, make_copy)
    for cond, (cx, cm) in copies:
        @pl.when(cond)
        def _(cx=cx, cm=cm):
            cx.start()
            cm.start()
    for cond, (cx, cm) in copies:
        @pl.when(cond)
        def _(cx=cx, cm=cm):
            cx.wait()
            cm.wait()

    @pl.when(i == pl.num_programs(0) - 1)
    def _():
        zx_ref[...] = jnp.zeros_like(zx_ref)
        zm_ref[...] = jnp.zeros_like(zm_ref)
        fills = []
        for g in range(N_GROUPS):
            row0 = pl.multiple_of(end_ref[g], RUN_ALIGN)
            fills.append(pltpu.make_async_copy(zx_ref, xs_ref.at[pl.ds(row0, MOE_T)], sem.at[0]))
            fills.append(pltpu.make_async_copy(zm_ref, ms_ref.at[pl.ds(row0, MOE_T)], sem.at[1]))
        for c in fills:
            c.start()
        for c in fills:
            c.wait()


def _dispatch(hn, meta, pos, lens, dst, end, cap):
    n = hn.shape[0]
    nt = n // MOE_TM
    return pl.pallas_call(
        _dispatch_kernel,
        grid_spec=pltpu.PrefetchScalarGridSpec(
            num_scalar_prefetch=3,
            grid=(nt,),
            in_specs=[pl.BlockSpec((MOE_TM, D), lambda i, *_: (i, 0)),
                      pl.BlockSpec((MOE_TM, LANES), lambda i, *_: (i, 0)),
                      pl.BlockSpec((1, 8, MOE_TM), lambda i, *_: (i, 0, 0))],
            out_specs=[pl.BlockSpec(memory_space=pl.ANY), pl.BlockSpec(memory_space=pl.ANY)],
            scratch_shapes=[pltpu.VMEM((MOE_TP, D), BF16), pltpu.VMEM((MOE_TP, LANES), F32),
                            pltpu.VMEM((MOE_T, D), BF16), pltpu.VMEM((MOE_T, LANES), F32),
                            pltpu.SemaphoreType.DMA((2,))]),
        out_shape=[jax.ShapeDtypeStruct((N_GROUPS * cap, D), BF16),
                   jax.ShapeDtypeStruct((N_GROUPS * cap, LANES), F32)],
        compiler_params=_cparams(("arbitrary",)),
        name="dispatch",
    )(lens, dst, end, hn, meta, pos)


def _experts_kernel(tgrp_ref, tblk_ref, nvalid_ref, xs_ref, ms_ref, wg_ref, wu_ref, wd_ref, ys_ref,
                    acc_ref):
    j = pl.program_id(0)
    e = pl.program_id(1)

    @pl.when(j < nvalid_ref[0])
    def _():
        @pl.when(e == 0)
        def _():
            acc_ref[...] = jnp.zeros_like(acc_ref)

        xs = xs_ref[...]
        gate = _dot(xs, wg_ref[0])
        he = (gate * _sigmoid(gate) * _dot(xs, wu_ref[0])).astype(BF16)
        lane = lax.broadcasted_iota(jnp.int32, ms_ref.shape, 1)
        col = N_GROUPS + tgrp_ref[j] * EPG + e
        cw = jnp.sum(jnp.where(lane == col, ms_ref[...], 0.0), axis=-1, keepdims=True)
        acc_ref[...] += cw * _dot(he, wd_ref[0])

        @pl.when(e == EPG - 1)
        def _():
            ys_ref[...] = acc_ref[...].astype(BF16)


def _experts(xs, ms, tgrp, tblk, nvalid, lw, n_tiles):
    wmap = lambda j, e, tgrp, tblk, nvalid: (tgrp[j] * EPG + e, 0, 0)
    rmap = lambda j, e, tgrp, tblk, nvalid: (tblk[j], 0)
    return pl.pallas_call(
        _experts_kernel,
        grid_spec=pltpu.PrefetchScalarGridSpec(
            num_scalar_prefetch=3,
            grid=(n_tiles, EPG),
            in_specs=[pl.BlockSpec((MOE_T, D), rmap), pl.BlockSpec((MOE_T, LANES), rmap),
                      pl.BlockSpec((1, D, D_EXPERT), wmap), pl.BlockSpec((1, D, D_EXPERT), wmap),
                      pl.BlockSpec((1, D_EXPERT, D), wmap)],
            out_specs=pl.BlockSpec((MOE_T, D), rmap),
            scratch_shapes=[pltpu.VMEM((MOE_T, D), F32)]),
        out_shape=jax.ShapeDtypeStruct(xs.shape, BF16),
        compiler_params=_cparams(("arbitrary", "arbitrary")),
        name="experts",
    )(tgrp, tblk, nvalid, xs, ms, lw["w_e_gate"], lw["w_e_up"], lw["w_e_down"])


def _combine_kernel(final_norm, lens_ref, dst_ref, x_ref, meta_ref, ys_ref, gf_ref, o_ref, comp_ref,
                    sem):
    i = pl.program_id(0)
    tm = x_ref.shape[0]
    comp_ref[...] = jnp.zeros_like(comp_ref)

    def make_copy(local, sorted_row, rows):
        return pltpu.make_async_copy(ys_ref.at[pl.ds(sorted_row, rows)],
                                     comp_ref.at[pl.ds(local, rows)], sem.at[0])

    copies = _run_copies(lens_ref, dst_ref, i, make_copy)
    for cond, c in copies:
        pl.when(cond)(c.start)
    for cond, c in copies:
        pl.when(cond)(c.wait)

    lane = lax.broadcasted_iota(jnp.int32, meta_ref.shape, 1)
    pos = jnp.sum(jnp.where(lane == POS_ROW, meta_ref[...], 0.0), axis=-1, keepdims=True)
    slot = lax.broadcasted_iota(jnp.int32, (tm, MOE_TP), 1).astype(F32)
    unperm = jnp.where(slot == pos, 1.0, 0.0).astype(BF16)
    out = x_ref[...] + _dot(unperm, comp_ref[...])
    if final_norm:
        out = _rms(out, gf_ref[...])
    o_ref[...] = out


def _combine(x2, meta, ys, lens, dst, norm_f, final_norm):
    n = x2.shape[0]
    return pl.pallas_call(
        functools.partial(_combine_kernel, final_norm),
        grid_spec=pltpu.PrefetchScalarGridSpec(
            num_scalar_prefetch=2,
            grid=(n // MOE_TM,),
            in_specs=[pl.BlockSpec((MOE_TM, D), lambda i, *_: (i, 0)),
                      pl.BlockSpec((MOE_TM, LANES), lambda i, *_: (i, 0)),
                      pl.BlockSpec(memory_space=pl.ANY),
                      pl.BlockSpec((1, D), lambda i, *_: (0, 0))],
            out_specs=pl.BlockSpec((MOE_TM, D), lambda i, *_: (i, 0)),
            scratch_shapes=[pltpu.VMEM((MOE_TP, D), BF16), pltpu.SemaphoreType.DMA((1,))]),
        out_shape=jax.ShapeDtypeStruct((n, D), F32),
        compiler_params=_cparams(("arbitrary",)),
        name="combine",
    )(lens, dst, x2, meta, ys, norm_f)


def _moe(x2, lw, norm_f, final_norm):
    n = x2.shape[0]
    nt = n // MOE_TM
    cap = -(-(n + RUN_ALIGN * nt + MOE_T) // MOE_T) * MOE_T
    n_tiles = (n + N_GROUPS * RUN_ALIGN * nt) // MOE_T + N_GROUPS + 1
    hn, meta, pos, cnt = _router(x2, lw)

    lens = cnt[:, :N_GROUPS, 0]
    ends = jnp.cumsum(lens, axis=0)
    dst = (ends - lens) + (jnp.arange(N_GROUPS, dtype=jnp.int32) * cap)[None, :]
    total = ends[-1]
    end = total + jnp.arange(N_GROUPS, dtype=jnp.int32) * cap
    tiles = (total + MOE_T - 1) // MOE_T
    tile_end = jnp.cumsum(tiles)
    tile_start = tile_end - tiles
    nvalid = tile_end[-1]
    jj = jnp.minimum(jnp.arange(n_tiles, dtype=jnp.int32), nvalid - 1)
    tgrp = jnp.sum((jj[:, None] >= tile_end[None, :]).astype(jnp.int32), axis=1)
    tblk = tgrp * (cap // MOE_T) + jj - tile_start[tgrp]
    lens_f = lens.reshape(-1).astype(jnp.int32)
    dst_f = dst.reshape(-1).astype(jnp.int32)

    xs, ms = _dispatch(hn, meta, pos, lens_f, dst_f, end.astype(jnp.int32), cap)
    ys = _experts(xs, ms, tgrp.astype(jnp.int32), tblk.astype(jnp.int32),
                  nvalid.reshape(1).astype(jnp.int32), lw, n_tiles)
    return _combine(x2, meta, ys, lens_f, dst_f, norm_f, final_norm)


def _layer_weights(l, w):
    rw = 3 * D
    n_lo = LORA_DECAY + LORA_AAA + LORA_GATE
    w_in = w["w_in"][l]
    lora_cols = [w_in[:, rw:rw + n_lo]]
    mu_lora = [w["mu_in"][l][rw:rw + n_lo]]
    if l > 0:
        lora_cols.append(w["w_vres_in"][l - 1])
        mu_lora.append(w["mu_vres"][l - 1])
    lora = jnp.concatenate(lora_cols, axis=1)
    lora = jnp.pad(lora, ((0, 0), (0, LORA_W - lora.shape[1])))
    mu_lora = jnp.concatenate(mu_lora)
    mu_lora = jnp.pad(mu_lora, (0, LORA_W - mu_lora.shape[0]))
    c0 = rw + n_lo
    blocks = [w_in[:, :rw], w_in[:, c0:c0 + D], w_in[:, c0 + 2 * D:c0 + 3 * D], w_in[:, c0 + D:c0 + 2 * D],
              w_in[:, c0 + 3 * D:]]
    w_main = jnp.concatenate(blocks, axis=1).astype(BF16)

    pad_rows = lambda a, before, total: jnp.pad(a, ((before, total - before - a.shape[0]), (0, 0))).astype(BF16)
    wup_dec = pad_rows(w["w_decay_up"][l], 0, LANES)
    wup_aaa = pad_rows(w["w_aaa_up"][l], LORA_DECAY, LANES)
    wup_gate = pad_rows(w["w_gate_up"][l], 0, 2 * LANES)

    router = jnp.concatenate([w["router_g"][l], w["router_e"][l]], axis=1).T
    router = jnp.pad(router, ((0, LANES - router.shape[0]), (0, 0)))
    r_hi = router.astype(BF16)
    r_lo = (router - r_hi.astype(F32)).astype(BF16)
    router_b = jnp.concatenate([w["router_g_b"][l], w["router_e_b"][l]])
    router_b = jnp.pad(router_b, (0, LANES - router_b.shape[0]))[:, None]

    row = lambda a: a.reshape(1, -1)
    out = {
        "norm_mix": row(w["norm_mix"][l]), "w_main": w_main, "w_lora": lora.astype(BF16),
        "mu_rkv": row(w["mu_in"][l][:rw]), "mu_lora": row(mu_lora),
        "wup_dec": wup_dec, "wup_aaa": wup_aaa, "wup_gate": wup_gate,
        "w0": row(w["w0"][l]), "a0": row(w["a0"][l]), "k_k": row(w["k_k"][l]), "k_a": row(w["k_a"][l]),
        "conv_w": w["conv_w"][l],
        "r_k": row(w["r_k"][l]), "lnx_g": row(w["lnx_g"][l]), "lnx_b": row(w["lnx_b"][l]),
        "gate_b": row(w["gate_b"][l]),
        "w_br_a": w["w_br_a"][l].astype(BF16), "w_br_b": w["w_br_b"][l].astype(BF16),
        "w_mix_out": w["w_mix_out"][l].astype(BF16),
        "norm_memq": row(w["norm_memq"][l]), "norm_memkv": row(w["norm_memkv"][l]),
        "wq_mem": w["wq_mem"][l].astype(BF16), "wkv_mem": w["wkv_mem"][l].astype(BF16),
        "wo_mem": w["wo_mem"][l].astype(BF16),
        "norm_ffn": row(w["norm_ffn"][l]),
        "router_wt": jnp.stack([r_hi, r_lo]), "router_bt": router_b,
        "w_e_gate": w["w_e_gate"][l].astype(BF16), "w_e_up": w["w_e_up"][l].astype(BF16),
        "w_e_down": w["w_e_down"][l].astype(BF16),
    }
    if l > 0:
        out["v0"] = row(w["v0"][l - 1])
        out["wup_vres"] = pad_rows(w["w_vres_up"][l - 1], LORA_GATE, 2 * LANES)
    return out


def _tile(total, want):
    t = min(want, total)
    assert total % t == 0, (total, t)
    return t


def kernel(x, mem, norm_mix, w_in, mu_in, w_vres_in, mu_vres, w0, w_decay_up, a0, w_aaa_up, w_gate_up,
           v0, w_vres_up, k_k, k_a, r_k, lnx_g, lnx_b, conv_w, gate_b, w_br_a, w_br_b, w_mix_out,
           norm_memq, norm_memkv, wq_mem, wkv_mem, wo_mem, norm_ffn, router_g, router_g_b, router_e,
           router_e_b, w_e_gate, w_e_up, w_e_down, norm_f):
    w = dict(norm_mix=norm_mix, w_in=w_in, mu_in=mu_in, w_vres_in=w_vres_in, mu_vres=mu_vres, w0=w0,
             w_decay_up=w_decay_up, a0=a0, w_aaa_up=w_aaa_up, w_gate_up=w_gate_up, v0=v0,
             w_vres_up=w_vres_up, k_k=k_k, k_a=k_a, r_k=r_k, lnx_g=lnx_g, lnx_b=lnx_b, conv_w=conv_w,
             gate_b=gate_b, w_br_a=w_br_a, w_br_b=w_br_b, w_mix_out=w_mix_out, norm_memq=norm_memq,
             norm_memkv=norm_memkv, wq_mem=wq_mem, wkv_mem=wkv_mem, wo_mem=wo_mem, norm_ffn=norm_ffn,
             router_g=router_g, router_g_b=router_g_b, router_e=router_e, router_e_b=router_e_b,
             w_e_gate=w_e_gate, w_e_up=w_e_up, w_e_down=w_e_down)
    bsz, seq, _ = x.shape
    depth = norm_mix.shape[0]
    n = bsz * seq
    assert seq % CHUNK == 0 and mem.shape[1] == MEM_LEN
    x2 = x.reshape(n, D)
    mem2 = mem.reshape(bsz * MEM_LEN, D)
    tm_in = _tile(seq, 512)
    tc_wkv = _tile(seq, 256)
    tm_attn = _tile(seq, 512)
    assert n % MOE_TM == 0
    norm_f2 = norm_f.reshape(1, D)

    v_first = None
    for l in range(depth):
        lw = _layer_weights(l, w)
        r, k, v, kk, aa, lwd, og, yb, ga, gb = _mixin(x2, v_first, lw, tm_in, seq)
        if l == 0:
            v_first = v
        ya = _wkv(r, k, v, kk, aa, lwd, lw, bsz, seq, tc_wkv)
        kv = _memkv(mem2, lw["norm_memkv"], lw["wkv_mem"])
        x2 = _attn(x2, ya, og, yb, ga, gb, kv, lw, tm_attn, seq)
        x2 = _moe(x2, lw, norm_f2, l == depth - 1)
    return x2.reshape(bsz, seq, D)
```

```python
import functools
import math

import jax
import jax.numpy as jnp
from jax import lax
from jax.experimental import pallas as pl
from jax.experimental.pallas import tpu as pltpu

F32 = jnp.float32
BF16 = jnp.bfloat16

D = 1024
HEAD = 64
N_HEADS = 16
LANES = 128
N_PAIRS = D // LANES
CHUNK = 64
WKV_UNROLL = 2
LORA_DECAY, LORA_AAA, LORA_GATE, LORA_VRES = 64, 64, 160, 32
LORA_W = 512
CONV_K = 3
MEM_LEN = 256
MEM_HEADS = 4
MEM_HD = D // MEM_HEADS
N_GROUPS, EPG, N_EXPERTS, D_EXPERT = 4, 4, 16, 512
RMS_EPS, LNX_EPS, L2_EPS = 1e-6, 64e-5, 1e-12
VMEM_LIMIT = 56 * 1024 * 1024


def _cparams(sem):
    return pltpu.CompilerParams(dimension_semantics=sem, vmem_limit_bytes=VMEM_LIMIT)


def _rms(x, g):
    return x * lax.rsqrt(jnp.mean(x * x, axis=-1, keepdims=True) + RMS_EPS) * g


def _sigmoid(x):
    return 0.5 * jnp.tanh(0.5 * x) + 0.5


def _dot(a, b):
    return jnp.dot(a, b, preferred_element_type=F32)


def _dot_nt(a, b):
    return lax.dot_general(a, b, (((1,), (1,)), ((), ())), preferred_element_type=F32)


def _split_bf16(x):
    hi = x.astype(BF16)
    lo = (x - hi.astype(F32)).astype(BF16)
    return hi, lo


(BLK_R, BLK_K, BLK_V, BLK_CX, BLK_CC, BLK_CB, BLK_GA, BLK_GB) = range(8)


def _mixin_kernel(has_vres, tiles_per_seq, *refs):
    n_in = 17 if has_vres else 14
    ins, outs, scr = refs[:n_in], refs[n_in:n_in + 10], refs[n_in + 10:]
    (x_ref, g_ref, wm_ref, wl_ref, mu_ref, mul_ref, wdec_ref, waaa_ref, wgate_ref, w0_ref, a0_ref,
     kk_ref, ka_ref, cw_ref) = ins[:14]
    ro_ref, ko_ref, vo_ref, kko_ref, aao_ref, lwo_ref, ogo_ref, ybo_ref, gao_ref, gbo_ref = outs
    clo_ref, crkv_ref, cu_ref = scr
    if has_vres:
        vf_ref, wvres_ref, v0_ref = ins[14:]
    tm = x_ref.shape[0]

    @pl.when(pl.program_id(0) % tiles_per_seq == 0)
    def _():
        crkv_ref[...] = jnp.zeros_like(crkv_ref)
        clo_ref[...] = jnp.zeros_like(clo_ref)
        cu_ref[...] = jnp.zeros_like(cu_ref)

    xn = _rms(x_ref[...], g_ref[...]).astype(BF16)

    def mixed(z, carry, mu):
        return z + (_shift_rows(z, carry, 1) - z) * mu

    def block(b):
        return _dot(xn, wm_ref[b])

    def shifted_block(idx):
        z = block(idx)
        cols = slice(idx * D, (idx + 1) * D)
        zm = mixed(z, crkv_ref[:, cols], mu_ref[:, cols])
        crkv_ref[:, cols] = z[tm - 8:, :]
        return zm

    lo = _dot(xn, wl_ref[...])
    lom = mixed(lo, clo_ref[...], mul_ref[...])
    clo_ref[...] = lo[tm - 8:, :]
    lane = lax.broadcasted_iota(jnp.int32, (tm, LANES), 1)
    t0, t1, t2 = (lom[:, t * LANES:(t + 1) * LANES] for t in range(3))
    gate_tail = LORA_DECAY + LORA_AAA + LORA_GATE - 2 * LANES
    act_a = jnp.where(lane < LORA_DECAY, jnp.tanh(t0), t0).astype(BF16)
    act_b = jnp.concatenate([_sigmoid(t1), jnp.where(lane < gate_tail, _sigmoid(t2), t2)],
                            axis=1).astype(BF16)

    u = w0_ref[...] + _dot(act_a, wdec_ref[...])
    lwo_ref[...] = -math.exp(-0.5) * _sigmoid(u)
    ro_ref[...] = shifted_block(BLK_R).astype(BF16)

    aa = _sigmoid(a0_ref[...] + _dot(act_a, waaa_ref[...]))
    aao_ref[...] = aa.astype(BF16)
    zm = shifted_block(BLK_K)
    kko_ref[...] = (zm * kk_ref[...]).astype(BF16)
    ko_ref[...] = (zm * (1.0 + (aa - 1.0) * ka_ref[...])).astype(BF16)

    if has_vres:
        vgate = _sigmoid(v0_ref[...] + _dot(act_b, wvres_ref[...]))
    zm = shifted_block(BLK_V)
    if has_vres:
        zm = zm + (vf_ref[...].astype(F32) - zm) * vgate
    vo_ref[...] = zm.astype(BF16)

    ogo_ref[...] = _dot(act_b, wgate_ref[...]).astype(BF16)

    u = block(BLK_CC) * block(BLK_CX)
    cu = cu_ref[...]
    conv = cw_ref[CONV_K - 1:CONV_K, :] * u
    for t in range(CONV_K - 1):
        conv = conv + cw_ref[t:t + 1, :] * _shift_rows(u, cu, CONV_K - 1 - t)
    cu_ref[...] = u[tm - 8:, :]
    ybo_ref[...] = (block(BLK_CB) * conv).astype(BF16)

    gao_ref[...] = block(BLK_GA).astype(BF16)
    gbo_ref[...] = block(BLK_GB).astype(BF16)


def _mixin(x2, vfirst, lw, tm, seq):
    n = x2.shape[0]
    has_vres = vfirst is not None
    tile = pl.BlockSpec((tm, D), lambda i: (i, 0))
    full = lambda a: pl.BlockSpec(a.shape, lambda i: (0,) * a.ndim, pipeline_mode=pl.Buffered(1))
    names = ["norm_mix", "w_main", "w_lora", "mu_rkv", "mu_lora", "wup_dec", "wup_aaa", "wup_gate", "w0",
             "a0", "k_k", "k_a", "conv_w"]
    in_specs = [tile] + [full(lw[name]) for name in names]
    args = [x2] + [lw[name] for name in names]
    if has_vres:
        in_specs += [tile, full(lw["wup_vres"]), full(lw["v0"])]
        args += [vfirst, lw["wup_vres"], lw["v0"]]
    bf = jax.ShapeDtypeStruct((n, D), BF16)
    out_shape = [bf, bf, bf, bf, bf, jax.ShapeDtypeStruct((n, D), F32), bf, bf, bf, bf]
    return pl.pallas_call(
        functools.partial(_mixin_kernel, has_vres, seq // tm),
        grid=(n // tm,),
        in_specs=in_specs,
        out_specs=[tile] * 10,
        out_shape=out_shape,
        scratch_shapes=[pltpu.VMEM((8, LORA_W), F32), pltpu.VMEM((8, 3 * D), F32), pltpu.VMEM((8, D), F32)],
        compiler_params=_cparams(("arbitrary",)),
        name="mixin",
    )(*args)


def _shift_rows(z, prev_rows, k):
    tm = z.shape[0]
    rolled = pltpu.roll(z, k, 0)
    rows = lax.broadcasted_iota(jnp.int32, z.shape, 0)
    out = rolled
    for j in range(k):
        src = prev_rows[8 - k + j:8 - k + j + 1, :]
        out = jnp.where(rows == j, src, out)
    return out


def _pair_expand(x, m0, m1):
    return jnp.concatenate([jnp.where(m0, x, 0.0), jnp.where(m1, x, 0.0)], axis=0)


def _halves_sum(x, m0):
    s0 = jnp.sum(jnp.where(m0, x, 0.0), axis=-1, keepdims=True)
    s1 = jnp.sum(jnp.where(m0, 0.0, x), axis=-1, keepdims=True)
    return jnp.where(m0, s0, s1)


def _wkv_kernel(r_ref, k_ref, v_ref, kk_ref, aa_ref, lw_ref, rk_ref, g_ref, b_ref, o_ref, s_ref):
    @pl.when(pl.program_id(1) == 0)
    def _():
        s_ref[...] = jnp.zeros_like(s_ref)

    c2 = 2 * CHUNK
    lane = lax.broadcasted_iota(jnp.int32, (CHUNK, LANES), 1)
    m0 = lane < HEAD
    m1 = jnp.logical_not(m0)
    rows = lax.broadcasted_iota(jnp.int32, (c2, c2), 0)
    cols = lax.broadcasted_iota(jnp.int32, (c2, c2), 1)
    strict = cols < rows
    incl = cols <= rows
    eye = cols == rows
    tri = (lax.broadcasted_iota(jnp.int32, (CHUNK, CHUNK), 1)
           <= lax.broadcasted_iota(jnp.int32, (CHUNK, CHUNK), 0)).astype(BF16)
    n_chunks = r_ref.shape[0] // CHUNK

    def chunk_body(ci, carry):
        t0s = [pl.multiple_of((ci * WKV_UNROLL + u) * CHUNK, CHUNK) for u in range(WKV_UNROLL)]
        probs = [(t0, slice(pi * LANES, (pi + 1) * LANES)) for t0 in t0s for pi in range(N_PAIRS)]
        cat0 = lambda *xs: jnp.concatenate(xs, axis=0)
        cat1 = lambda *xs: jnp.concatenate(xs, axis=1)
        expand = lambda x: _pair_expand(x, m0, m1)
        ld = lambda ref, p: ref[pl.ds(p[0], CHUNK), p[1]].astype(F32)

        lw_all = [lw_ref[pl.ds(t0, CHUNK), :] for t0 in t0s]
        cum_all = []
        for lw_u in lw_all:
            lw_hi, lw_lo = _split_bf16(lw_u)
            cum_all.append(_dot(tri, lw_hi) + _dot(tri, lw_lo))
        lw = [lw_u[:, pi * LANES:(pi + 1) * LANES] for lw_u in lw_all for pi in range(N_PAIRS)]
        cum = [c_u[:, pi * LANES:(pi + 1) * LANES] for c_u in cum_all for pi in range(N_PAIRS)]
        cum_end = [c[CHUNK - 1:CHUNK, :] for c in cum]
        w_inv = [jnp.exp(-c) for c in cum]
        w_end = [jnp.exp(ce - c) for c, ce in zip(cum, cum_end)]
        r = [ld(r_ref, p) for p in probs]
        k = [ld(k_ref, p) for p in probs]
        kk = [ld(kk_ref, p) for p in probs]
        kk = [x / jnp.maximum(jnp.sqrt(_halves_sum(x * x, m0)), L2_EPS) for x in kk]
        b = [x * ld(aa_ref, p) for x, p in zip(kk, probs)]
        a_t = [expand(-x * jnp.exp(c - l)) for x, c, l in zip(kk, cum, lw)]
        r_t = [expand(x * jnp.exp(c)) for x, c in zip(r, cum)]
        lhs1 = [cat0(a, rr).astype(BF16) for a, rr in zip(a_t, r_t)]
        rhs1 = [cat0(expand(bb * wi), expand(kx * wi)).astype(BF16) for bb, kx, wi in zip(b, k, w_inv)]
        aa_m = [_dot_nt(x, y) for x, y in zip(lhs1, rhs1)]
        a_ab = [jnp.where(strict, m[:c2, :c2], 0.0) for m in aa_m]
        a_ak = [jnp.where(strict, m[:c2, c2:], 0.0).astype(BF16) for m in aa_m]
        lhs_top = [cat1(jnp.where(incl, m[c2:, :c2], 0.0), jnp.where(incl, m[c2:, c2:], 0.0)).astype(BF16)
                   for m in aa_m]

        t_m = [jnp.where(eye, 1.0, 0.0) + m for m in a_ab]
        a_b = [m.astype(BF16) for m in a_ab]
        a_m = [_dot(x, x) for x in a_b]
        n_steps = CHUNK.bit_length() - 1
        for step in range(1, n_steps):
            a_b = [m.astype(BF16) for m in a_m]
            if step < n_steps - 1:
                both = [_dot(x, cat1(x, t.astype(BF16))) for x, t in zip(a_b, t_m)]
                a_m = [m[:, :c2] for m in both]
                t_m = [t + m[:, c2:] for t, m in zip(t_m, both)]
            else:
                t_m = [t + _dot(x, t.astype(BF16)) for x, t in zip(a_b, t_m)]

        v = [ld(v_ref, p) for p in probs]
        v_b = [expand(x).astype(BF16) for x in v]
        akv = [_dot(x, y) for x, y in zip(a_ak, v_b)]
        tx = [_dot(t.astype(BF16), cat1(a, y).astype(BF16)) for t, a, y in zip(t_m, a_t, akv)]
        rhs2 = [cat0(x.astype(BF16), cat1(jnp.zeros((c2, c2), BF16), y)) for x, y in zip(tx, v_b)]
        lhs_bot = [cat1(expand(bb * we).T, expand(kx * we).T).astype(BF16)
                   for bb, kx, we in zip(b, k, w_end)]
        z = [_dot(cat0(x, y), w) for x, y, w in zip(lhs_top, lhs_bot, rhs2)]
        gp = [cat0(rt + m[:c2, :c2], m[c2:, :c2] + jnp.where(eye, jnp.exp(ce), 0.0)).astype(BF16)
              for rt, m, ce in zip(r_t, z, cum_end)]

        state = [s_ref[pi] for pi in range(N_PAIRS)]
        y = []
        for u in range(WKV_UNROLL):
            base = u * N_PAIRS
            zz = [_dot(gp[base + pi], state[pi].astype(BF16)) for pi in range(N_PAIRS)]
            state = [zz[pi][c2:] + z[base + pi][c2:, c2:] for pi in range(N_PAIRS)]
            y += [(zz[pi][:CHUNK] + zz[pi][CHUNK:c2])
                  + (z[base + pi][:CHUNK, c2:] + z[base + pi][CHUNK:c2, c2:]) for pi in range(N_PAIRS)]
        for pi in range(N_PAIRS):
            s_ref[pi] = state[pi]

        for i, (t0, sl) in enumerate(probs):
            mu = _halves_sum(y[i], m0) * (1.0 / HEAD)
            yc = y[i] - mu
            var = _halves_sum(yc * yc, m0) * (1.0 / HEAD)
            yn = yc * lax.rsqrt(var + LNX_EPS) * g_ref[:, sl] + b_ref[:, sl]
            bonus = _halves_sum(r[i] * k[i] * rk_ref[:, sl], m0) * v[i]
            o_ref[pl.ds(t0, CHUNK), sl] = (yn + bonus).astype(BF16)
        return carry

    assert n_chunks % WKV_UNROLL == 0
    lax.fori_loop(0, n_chunks // WKV_UNROLL, chunk_body, 0)


def _wkv(r, k, v, kk, aa, lw, lwts, bsz, seq, tc):
    n = r.shape[0]
    nt = seq // tc
    blk = pl.BlockSpec((tc, D), lambda b, t: (b * nt + t, 0))
    row = pl.BlockSpec((1, D), lambda b, t: (0, 0))
    return pl.pallas_call(
        _wkv_kernel,
        grid=(bsz, nt),
        in_specs=[blk] * 6 + [row] * 3,
        out_specs=blk,
        out_shape=jax.ShapeDtypeStruct((n, D), BF16),
        scratch_shapes=[pltpu.VMEM((N_PAIRS, LANES, LANES), F32)],
        compiler_params=_cparams(("parallel", "arbitrary")),
        name="wkv",
    )(r, k, v, kk, aa, lw, lwts["r_k"], lwts["lnx_g"], lwts["lnx_b"])


def _memkv_kernel(m_ref, g_ref, w_ref, o_ref):
    o_ref[...] = _dot(_rms(m_ref[...], g_ref[...]).astype(BF16), w_ref[...]).astype(BF16)


def _memkv(mem2, g, w):
    n = mem2.shape[0]
    return pl.pallas_call(
        _memkv_kernel,
        grid=(n // MEM_LEN,),
        in_specs=[pl.BlockSpec((MEM_LEN, D), lambda i: (i, 0)),
                  pl.BlockSpec((1, D), lambda i: (0, 0)),
                  pl.BlockSpec((D, 2 * D), lambda i: (0, 0))],
        out_specs=pl.BlockSpec((MEM_LEN, 2 * D), lambda i: (i, 0)),
        out_shape=jax.ShapeDtypeStruct((n, 2 * D), BF16),
        compiler_params=_cparams(("parallel",)),
        name="memkv",
    )(mem2, g, w)


def _attn_kernel(x_ref, ya_ref, og_ref, yb_ref, ga_ref, gb_ref, kv_ref, gbias_ref, wa_ref, wb_ref,
                 wmix_ref, gq_ref, wq_ref, wo_ref, o_ref):
    y_a = (ya_ref[...].astype(F32) * og_ref[...].astype(F32)).astype(BF16)
    g_a = _sigmoid(ga_ref[...].astype(F32) + gbias_ref[:, :D])
    g_b = _sigmoid(gb_ref[...].astype(F32) + gbias_ref[:, D:])
    merged = g_a * _dot(y_a, wa_ref[...]) + g_b * _dot(yb_ref[...], wb_ref[...])
    x1 = x_ref[...] + _dot(merged.astype(BF16), wmix_ref[...])

    hn = _rms(x1, gq_ref[...]).astype(BF16)
    q = _dot(hn, wq_ref[...]).astype(BF16)
    heads = []
    for h in range(MEM_HEADS):
        sl = slice(h * MEM_HD, (h + 1) * MEM_HD)
        s = _dot_nt(q[:, sl], kv_ref[:, sl]) * (MEM_HD ** -0.5)
        s = s - jnp.max(s, axis=-1, keepdims=True)
        e = jnp.exp(s)
        pr = (e / jnp.sum(e, axis=-1, keepdims=True)).astype(BF16)
        heads.append(_dot(pr, kv_ref[:, D + h * MEM_HD:D + (h + 1) * MEM_HD]).astype(BF16))
    o = jnp.concatenate(heads, axis=1)
    o_ref[...] = x1 + _dot(o, wo_ref[...])


def _attn(x2, ya, og, yb, ga, gb, kv, lw, tm, seq):
    n = x2.shape[0]
    tps = seq // tm
    tile = pl.BlockSpec((tm, D), lambda i: (i, 0))
    full = lambda a: pl.BlockSpec(a.shape, lambda i: (0, 0))
    wts = [lw["gate_b"], lw["w_br_a"], lw["w_br_b"], lw["w_mix_out"], lw["norm_memq"], lw["wq_mem"],
           lw["wo_mem"]]
    return pl.pallas_call(
        _attn_kernel,
        grid=(n // tm,),
        in_specs=[tile] * 6 + [pl.BlockSpec((MEM_LEN, 2 * D), lambda i: (i // tps, 0))]
        + [full(w) for w in wts],
        out_specs=tile,
        out_shape=jax.ShapeDtypeStruct((n, D), F32),
        compiler_params=_cparams(("parallel",)),
        name="attn",
    )(x2, ya, og, yb, ga, gb, kv, *wts)


MOE_TM = 512
MOE_TP = MOE_TM + LANES
MOE_T = 512
RUN_ALIGN = 16
POS_ROW = N_GROUPS + N_EXPERTS
RUN_BITS = tuple(range((MOE_TM // RUN_ALIGN).bit_length() - 1, -1, -1))


def _router_kernel(x_ref, g_ref, rw_ref, rb_ref, hn_ref, meta_ref, pos_ref, cnt_ref):
    tm = x_ref.shape[0]
    hn = _rms(x_ref[...], g_ref[...])
    hi, lo = _split_bf16(hn)
    hn_ref[...] = hi
    lt = _dot_nt(rw_ref[0], hi) + _dot_nt(rw_ref[0], lo) + _dot_nt(rw_ref[1], hi) + rb_ref[...]
    row = lax.broadcasted_iota(jnp.int32, lt.shape, 0).astype(F32)
    neg = jnp.float32(-jnp.inf)
    big = jnp.float32(1 << 20)
    is_g = row < N_GROUPS
    lg = jnp.where(is_g, lt, neg)
    gmax = jnp.max(lg, axis=0, keepdims=True)
    ge = jnp.where(is_g, jnp.exp(lg - gmax), 0.0)
    gp = ge / jnp.sum(ge, axis=0, keepdims=True)
    g_w = jnp.max(gp, axis=0, keepdims=True)
    g_i = jnp.min(jnp.where(is_g & (gp == g_w), row, big), axis=0, keepdims=True)
    first = N_GROUPS + g_i * EPG
    sel = (row >= first) & (row < first + EPG)
    le = jnp.where(sel, lt, neg)
    emax = jnp.max(le, axis=0, keepdims=True)
    ee = jnp.where(sel, jnp.exp(le - emax), 0.0)
    ep = ee / jnp.sum(ee, axis=0, keepdims=True)
    p1 = jnp.max(ep, axis=0, keepdims=True)
    i1 = jnp.min(jnp.where(sel & (ep == p1), row, big), axis=0, keepdims=True)
    rest = sel & (row != i1)
    p2 = jnp.max(jnp.where(rest, ep, -1.0), axis=0, keepdims=True)
    i2 = jnp.min(jnp.where(rest & (ep == p2), row, big), axis=0, keepdims=True)
    tot = p1 + p2
    cw = jnp.where(row == i1, g_w * (p1 / tot), 0.0) + jnp.where(row == i2, g_w * (p2 / tot), 0.0)

    grow = lax.broadcasted_iota(jnp.int32, (8, tm), 0).astype(F32)
    onehot = jnp.where(grow == g_i, 1.0, 0.0)
    before = (lax.broadcasted_iota(jnp.int32, (tm, tm), 0)
              < lax.broadcasted_iota(jnp.int32, (tm, tm), 1)).astype(BF16)
    rank = _dot(onehot.astype(BF16), before)
    count = jnp.sum(onehot, axis=1, keepdims=True)
    run = jnp.floor((count + (RUN_ALIGN - 1)) * (1.0 / RUN_ALIGN)) * RUN_ALIGN
    gcol = lax.broadcasted_iota(jnp.int32, (8, 1), 0)
    start = jnp.zeros((8, 1), F32)
    acc = jnp.zeros((1, 1), F32)
    for g in range(1, N_GROUPS):
        acc = acc + run[g - 1:g, :]
        start = start + jnp.where(gcol == g, acc, 0.0)
    pos = jnp.sum(onehot * (start + rank), axis=0, keepdims=True)
    meta_ref[...] = (cw + jnp.where(row == POS_ROW, pos, 0.0)).T
    pos_ref[0] = jnp.broadcast_to(pos, (8, tm))
    cnt_ref[0] = jnp.broadcast_to(run, (8, LANES)).astype(jnp.int32)


def _router(x2, lw):
    n = x2.shape[0]
    nt = n // MOE_TM
    return pl.pallas_call(
        _router_kernel,
        grid=(nt,),
        in_specs=[pl.BlockSpec((MOE_TM, D), lambda i: (i, 0)),
                  pl.BlockSpec((1, D), lambda i: (0, 0)),
                  pl.BlockSpec((2, LANES, D), lambda i: (0, 0, 0)),
                  pl.BlockSpec((LANES, 1), lambda i: (0, 0))],
        out_specs=[pl.BlockSpec((MOE_TM, D), lambda i: (i, 0)),
                   pl.BlockSpec((MOE_TM, LANES), lambda i: (i, 0)),
                   pl.BlockSpec((1, 8, MOE_TM), lambda i: (i, 0, 0)),
                   pl.BlockSpec((1, 8, LANES), lambda i: (i, 0, 0))],
        out_shape=[jax.ShapeDtypeStruct((n, D), BF16), jax.ShapeDtypeStruct((n, LANES), F32),
                   jax.ShapeDtypeStruct((nt, 8, MOE_TM), F32),
                   jax.ShapeDtypeStruct((nt, 8, LANES), jnp.int32)],
        compiler_params=_cparams(("parallel",)),
        name="router",
    )(x2, lw["norm_ffn"], lw["router_wt"], lw["router_bt"])


def _run_copies(lens_ref, dst_ref, tile, make_copy):
    out = []
    local = 0
    for g in range(N_GROUPS):
        length = lens_ref[tile * N_GROUPS + g]
        base = dst_ref[tile * N_GROUPS + g]
        for bit in RUN_BITS:
            rows = RUN_ALIGN << bit
            done = length & (-2 * rows)
            cond = (length & rows) != 0
            out.append((cond, make_copy(pl.multiple_of(local + done, RUN_ALIGN),
                                        pl.multiple_of(base + done, RUN_ALIGN), rows)))
        local = local + length
    return out


def _dispatch_kernel(lens_ref, dst_ref, end_ref, hn_ref, meta_ref, pos_ref, xs_ref, ms_ref,
                     comp_ref, compm_ref, zx_ref, zm_ref, sem):
    i = pl.program_id(0)
    last = pl.num_programs(0) - 1
    tm = hn_ref.shape[0]

    def copies_of(tile):
        buf = tile % 2

        def make_copy(local, sorted_row, rows):
            return (pltpu.make_async_copy(comp_ref.at[buf, pl.ds(local, rows)],
                                          xs_ref.at[pl.ds(sorted_row, rows)], sem.at[0, buf]),
                    pltpu.make_async_copy(compm_ref.at[buf, pl.ds(local, rows)],
                                          ms_ref.at[pl.ds(sorted_row, rows)], sem.at[1, buf]))

        return _run_copies(lens_ref, dst_ref, tile, make_copy)

    def wait_copies(tile):
        for cond, (cx, cm) in copies_of(tile):
            @pl.when(cond)
            def _(cx=cx, cm=cm):
                cx.wait()
                cm.wait()

    slot = lax.broadcasted_iota(jnp.int32, (MOE_TP, tm), 0).astype(F32)
    perm = jnp.where(slot == pos_ref[0][0:1, :], 1.0, 0.0).astype(BF16)
    comp_ref[i % 2] = _dot(perm, hn_ref[...]).astype(BF16)
    m = meta_ref[...]
    m1 = m.astype(BF16)
    r1 = m - m1.astype(F32)
    m2 = r1.astype(BF16)
    m3 = (r1 - m2.astype(F32)).astype(BF16)
    compm_ref[i % 2] = _dot(perm, m1) + _dot(perm, m2) + _dot(perm, m3)

    for cond, (cx, cm) in copies_of(i):
        @pl.when(cond)
        def _(cx=cx, cm=cm):
            cx.start()
            cm.start()

    @pl.when(i > 0)
    def _():
        wait_copies(i - 1)

    @pl.when(i == last)
    def _():
        wait_copies(i)
        zx_ref[...] = jnp.zeros_like(zx_ref)
        zm_ref[...] = jnp.zeros_like(zm_ref)
        fills = []
        for g in range(N_GROUPS):
            row0 = pl.multiple_of(end_ref[g], RUN_ALIGN)
            fills.append(pltpu.make_async_copy(zx_ref, xs_ref.at[pl.ds(row0, MOE_T)], sem.at[0, 0]))
            fills.append(pltpu.make_async_copy(zm_ref, ms_ref.at[pl.ds(row0, MOE_T)], sem.at[1, 0]))
        for c in fills:
            c.start()
        for c in fills:
            c.wait()


def _dispatch(hn, meta, pos, lens, dst, end, cap):
    n = hn.shape[0]
    nt = n // MOE_TM
    return pl.pallas_call(
        _dispatch_kernel,
        grid_spec=pltpu.PrefetchScalarGridSpec(
            num_scalar_prefetch=3,
            grid=(nt,),
            in_specs=[pl.BlockSpec((MOE_TM, D), lambda i, *_: (i, 0)),
                      pl.BlockSpec((MOE_TM, LANES), lambda i, *_: (i, 0)),
                      pl.BlockSpec((1, 8, MOE_TM), lambda i, *_: (i, 0, 0))],
            out_specs=[pl.BlockSpec(memory_space=pl.ANY), pl.BlockSpec(memory_space=pl.ANY)],
            scratch_shapes=[pltpu.VMEM((2, MOE_TP, D), BF16), pltpu.VMEM((2, MOE_TP, LANES), F32),
                            pltpu.VMEM((MOE_T, D), BF16), pltpu.VMEM((MOE_T, LANES), F32),
                            pltpu.SemaphoreType.DMA((2, 2))]),
        out_shape=[jax.ShapeDtypeStruct((N_GROUPS * cap, D), BF16),
                   jax.ShapeDtypeStruct((N_GROUPS * cap, LANES), F32)],
        compiler_params=_cparams(("arbitrary",)),
        name="dispatch",
    )(lens, dst, end, hn, meta, pos)


def _experts_kernel(tgrp_ref, tblk_ref, nvalid_ref, xs_ref, ms_ref, wg_ref, wu_ref, wd_ref, ys_ref):
    j = pl.program_id(0)

    @pl.when(j < nvalid_ref[0])
    def _():
        xs = xs_ref[...]
        ms = ms_ref[...]
        lane = lax.broadcasted_iota(jnp.int32, ms.shape, 1)
        first = N_GROUPS + tgrp_ref[j] * EPG
        acc = None
        for e in range(EPG):
            gate = _dot(xs, wg_ref[e])
            he = (gate * _sigmoid(gate) * _dot(xs, wu_ref[e])).astype(BF16)
            cw = jnp.sum(jnp.where(lane == first + e, ms, 0.0), axis=-1, keepdims=True)
            term = cw * _dot(he, wd_ref[e])
            acc = term if acc is None else acc + term
        ys_ref[...] = acc.astype(BF16)


def _experts(xs, ms, tgrp, tblk, nvalid, lw, n_tiles):
    wmap = lambda j, tgrp, tblk, nvalid: (tgrp[j], 0, 0)
    rmap = lambda j, tgrp, tblk, nvalid: (tblk[j], 0)
    return pl.pallas_call(
        _experts_kernel,
        grid_spec=pltpu.PrefetchScalarGridSpec(
            num_scalar_prefetch=3,
            grid=(n_tiles,),
            in_specs=[pl.BlockSpec((MOE_T, D), rmap), pl.BlockSpec((MOE_T, LANES), rmap),
                      pl.BlockSpec((EPG, D, D_EXPERT), wmap), pl.BlockSpec((EPG, D, D_EXPERT), wmap),
                      pl.BlockSpec((EPG, D_EXPERT, D), wmap)],
            out_specs=pl.BlockSpec((MOE_T, D), rmap)),
        out_shape=jax.ShapeDtypeStruct(xs.shape, BF16),
        compiler_params=_cparams(("arbitrary",)),
        name="experts",
    )(tgrp, tblk, nvalid, xs, ms, lw["w_e_gate"], lw["w_e_up"], lw["w_e_down"])


def _combine_kernel(final_norm, lens_ref, dst_ref, x_ref, meta_ref, ys_ref, gf_ref, o_ref, comp_ref,
                    sem):
    i = pl.program_id(0)
    tm = x_ref.shape[0]

    def copies_of(tile):
        buf = tile % 2

        def make_copy(local, sorted_row, rows):
            return pltpu.make_async_copy(ys_ref.at[pl.ds(sorted_row, rows)],
                                         comp_ref.at[buf, pl.ds(local, rows)], sem.at[buf])

        return _run_copies(lens_ref, dst_ref, tile, make_copy)

    def start_copies(tile):
        for cond, c in copies_of(tile):
            pl.when(cond)(c.start)

    @pl.when(i == 0)
    def _():
        comp_ref[...] = jnp.zeros_like(comp_ref)
        start_copies(i)

    @pl.when(i < pl.num_programs(0) - 1)
    def _():
        start_copies(i + 1)

    for cond, c in copies_of(i):
        pl.when(cond)(c.wait)

    lane = lax.broadcasted_iota(jnp.int32, meta_ref.shape, 1)
    pos = jnp.sum(jnp.where(lane == POS_ROW, meta_ref[...], 0.0), axis=-1, keepdims=True)
    slot = lax.broadcasted_iota(jnp.int32, (tm, MOE_TP), 1).astype(F32)
    unperm = jnp.where(slot == pos, 1.0, 0.0).astype(BF16)
    out = x_ref[...] + _dot(unperm, comp_ref[i % 2])
    if final_norm:
        out = _rms(out, gf_ref[...])
    o_ref[...] = out


def _combine(x2, meta, ys, lens, dst, norm_f, final_norm):
    n = x2.shape[0]
    return pl.pallas_call(
        functools.partial(_combine_kernel, final_norm),
        grid_spec=pltpu.PrefetchScalarGridSpec(
            num_scalar_prefetch=2,
            grid=(n // MOE_TM,),
            in_specs=[pl.BlockSpec((MOE_TM, D), lambda i, *_: (i, 0)),
                      pl.BlockSpec((MOE_TM, LANES), lambda i, *_: (i, 0)),
                      pl.BlockSpec(memory_space=pl.ANY),
                      pl.BlockSpec((1, D), lambda i, *_: (0, 0))],
            out_specs=pl.BlockSpec((MOE_TM, D), lambda i, *_: (i, 0)),
            scratch_shapes=[pltpu.VMEM((2, MOE_TP, D), BF16), pltpu.SemaphoreType.DMA((2,))]),
        out_shape=jax.ShapeDtypeStruct((n, D), F32),
        compiler_params=_cparams(("arbitrary",)),
        name="combine",
    )(lens, dst, x2, meta, ys, norm_f)


def _moe(x2, lw, norm_f, final_norm):
    n = x2.shape[0]
    nt = n // MOE_TM
    cap = -(-(n + RUN_ALIGN * nt + MOE_T) // MOE_T) * MOE_T
    n_tiles = (n + N_GROUPS * RUN_ALIGN * nt) // MOE_T + N_GROUPS + 1
    hn, meta, pos, cnt = _router(x2, lw)

    lens = cnt[:, :N_GROUPS, 0]
    ends = jnp.cumsum(lens, axis=0)
    dst = (ends - lens) + (jnp.arange(N_GROUPS, dtype=jnp.int32) * cap)[None, :]
    total = ends[-1]
    end = total + jnp.arange(N_GROUPS, dtype=jnp.int32) * cap
    tiles = (total + MOE_T - 1) // MOE_T
    tile_end = jnp.cumsum(tiles)
    tile_start = tile_end - tiles
    nvalid = tile_end[-1]
    jj = jnp.clip(jnp.arange(n_tiles, dtype=jnp.int32), 0, jnp.maximum(nvalid - 1, 0))
    tgrp = jnp.minimum(jnp.sum((jj[:, None] >= tile_end[None, :]).astype(jnp.int32), axis=1), N_GROUPS - 1)
    tblk = tgrp * (cap // MOE_T) + jj - tile_start[tgrp]
    lens_f = lens.reshape(-1).astype(jnp.int32)
    dst_f = dst.reshape(-1).astype(jnp.int32)

    xs, ms = _dispatch(hn, meta, pos, lens_f, dst_f, end.astype(jnp.int32), cap)
    ys = _experts(xs, ms, tgrp.astype(jnp.int32), tblk.astype(jnp.int32),
                  nvalid.reshape(1).astype(jnp.int32), lw, n_tiles)
    return _combine(x2, meta, ys, lens_f, dst_f, norm_f, final_norm)


def _layer_weights(l, w):
    rw = 3 * D
    n_lo = LORA_DECAY + LORA_AAA + LORA_GATE
    w_in = w["w_in"][l]
    lora_cols = [w_in[:, rw:rw + n_lo]]
    mu_lora = [w["mu_in"][l][rw:rw + n_lo]]
    if l > 0:
        lora_cols.append(w["w_vres_in"][l - 1])
        mu_lora.append(w["mu_vres"][l - 1])
    lora = jnp.concatenate(lora_cols, axis=1)
    lora = jnp.pad(lora, ((0, 0), (0, LORA_W - lora.shape[1])))
    mu_lora = jnp.concatenate(mu_lora)
    mu_lora = jnp.pad(mu_lora, (0, LORA_W - mu_lora.shape[0]))
    c0 = rw + n_lo
    starts = [0, D, 2 * D, c0, c0 + 2 * D, c0 + D, c0 + 3 * D, c0 + 4 * D]
    w_main = jnp.stack([w_in[:, s:s + D] for s in starts]).astype(BF16)

    pad_rows = lambda a, before, total: jnp.pad(a, ((before, total - before - a.shape[0]), (0, 0))).astype(BF16)
    wup_dec = pad_rows(w["w_decay_up"][l], 0, LANES)
    wup_aaa = pad_rows(w["w_aaa_up"][l], LORA_DECAY, LANES)
    wup_gate = pad_rows(w["w_gate_up"][l], 0, 2 * LANES)

    router = jnp.concatenate([w["router_g"][l], w["router_e"][l]], axis=1).T
    router = jnp.pad(router, ((0, LANES - router.shape[0]), (0, 0)))
    r_hi = router.astype(BF16)
    r_lo = (router - r_hi.astype(F32)).astype(BF16)
    router_b = jnp.concatenate([w["router_g_b"][l], w["router_e_b"][l]])
    router_b = jnp.pad(router_b, (0, LANES - router_b.shape[0]))[:, None]

    row = lambda a: a.reshape(1, -1)
    out = {
        "norm_mix": row(w["norm_mix"][l]), "w_main": w_main, "w_lora": lora.astype(BF16),
        "mu_rkv": row(w["mu_in"][l][:rw]), "mu_lora": row(mu_lora),
        "wup_dec": wup_dec, "wup_aaa": wup_aaa, "wup_gate": wup_gate,
        "w0": row(w["w0"][l]), "a0": row(w["a0"][l]), "k_k": row(w["k_k"][l]), "k_a": row(w["k_a"][l]),
        "conv_w": w["conv_w"][l],
        "r_k": row(w["r_k"][l]), "lnx_g": row(w["lnx_g"][l]), "lnx_b": row(w["lnx_b"][l]),
        "gate_b": row(w["gate_b"][l]),
        "w_br_a": w["w_br_a"][l].astype(BF16), "w_br_b": w["w_br_b"][l].astype(BF16),
        "w_mix_out": w["w_mix_out"][l].astype(BF16),
        "norm_memq": row(w["norm_memq"][l]), "norm_memkv": row(w["norm_memkv"][l]),
        "wq_mem": w["wq_mem"][l].astype(BF16), "wkv_mem": w["wkv_mem"][l].astype(BF16),
        "wo_mem": w["wo_mem"][l].astype(BF16),
        "norm_ffn": row(w["norm_ffn"][l]),
        "router_wt": jnp.stack([r_hi, r_lo]), "router_bt": router_b,
        "w_e_gate": w["w_e_gate"][l].astype(BF16), "w_e_up": w["w_e_up"][l].astype(BF16),
        "w_e_down": w["w_e_down"][l].astype(BF16),
    }
    if l > 0:
        out["v0"] = row(w["v0"][l - 1])
        out["wup_vres"] = pad_rows(w["w_vres_up"][l - 1], LORA_GATE, 2 * LANES)
    return out


def _tile(total, want):
    t = min(want, total)
    assert total % t == 0, (total, t)
    return t


def kernel(x, mem, norm_mix, w_in, mu_in, w_vres_in, mu_vres, w0, w_decay_up, a0, w_aaa_up, w_gate_up,
           v0, w_vres_up, k_k, k_a, r_k, lnx_g, lnx_b, conv_w, gate_b, w_br_a, w_br_b, w_mix_out,
           norm_memq, norm_memkv, wq_mem, wkv_mem, wo_mem, norm_ffn, router_g, router_g_b, router_e,
           router_e_b, w_e_gate, w_e_up, w_e_down, norm_f):
    w = dict(norm_mix=norm_mix, w_in=w_in, mu_in=mu_in, w_vres_in=w_vres_in, mu_vres=mu_vres, w0=w0,
             w_decay_up=w_decay_up, a0=a0, w_aaa_up=w_aaa_up, w_gate_up=w_gate_up, v0=v0,
             w_vres_up=w_vres_up, k_k=k_k, k_a=k_a, r_k=r_k, lnx_g=lnx_g, lnx_b=lnx_b, conv_w=conv_w,
             gate_b=gate_b, w_br_a=w_br_a, w_br_b=w_br_b, w_mix_out=w_mix_out, norm_memq=norm_memq,
             norm_memkv=norm_memkv, wq_mem=wq_mem, wkv_mem=wkv_mem, wo_mem=wo_mem, norm_ffn=norm_ffn,
             router_g=router_g, router_g_b=router_g_b, router_e=router_e, router_e_b=router_e_b,
             w_e_gate=w_e_gate, w_e_up=w_e_up, w_e_down=w_e_down)
    bsz, seq, _ = x.shape
    depth = norm_mix.shape[0]
    n = bsz * seq
    assert seq % CHUNK == 0 and mem.shape[1] == MEM_LEN
    x2 = x.reshape(n, D)
    mem2 = mem.reshape(bsz * MEM_LEN, D)
    tm_in = _tile(seq, 512)
    tc_wkv = _tile(seq, 256)
    tm_attn = _tile(seq, 512)
    assert n % MOE_TM == 0
    norm_f2 = norm_f.reshape(1, D)

    v_first = None
    for l in range(depth):
        lw = _layer_weights(l, w)
        r, k, v, kk, aa, lwd, og, yb, ga, gb = _mixin(x2, v_first, lw, tm_in, seq)
        if l == 0:
            v_first = v
        ya = _wkv(r, k, v, kk, aa, lwd, lw, bsz, seq, tc_wkv)
        kv = _memkv(mem2, lw["norm_memkv"], lw["wkv_mem"])
        x2 = _attn(x2, ya, og, yb, ga, gb, kv, lw, tm_attn, seq)
        x2 = _moe(x2, lw, norm_f2, l == depth - 1)
    return x2.reshape(bsz, seq, D)
```

```python
import functools
import math

import jax
import jax.numpy as jnp
from jax import lax
from jax.experimental import pallas as pl
from jax.experimental.pallas import tpu as pltpu

F32 = jnp.float32
BF16 = jnp.bfloat16

D = 1024
HEAD = 64
N_HEADS = 16
LANES = 128
N_PAIRS = D // LANES
CHUNK = 64
WKV_UNROLL = 2
WKV_WAVES = 2
LORA_DECAY, LORA_AAA, LORA_GATE, LORA_VRES = 64, 64, 160, 32
LORA_W = 512
CONV_K = 3
MEM_LEN = 256
MEM_HEADS = 4
MEM_HD = D // MEM_HEADS
N_GROUPS, EPG, N_EXPERTS, D_EXPERT = 4, 4, 16, 512
RMS_EPS, LNX_EPS, L2_EPS = 1e-6, 64e-5, 1e-12
VMEM_LIMIT = 56 * 1024 * 1024


def _cparams(sem):
    return pltpu.CompilerParams(dimension_semantics=sem, vmem_limit_bytes=VMEM_LIMIT)


def _rms(x, g):
    return x * lax.rsqrt(jnp.mean(x * x, axis=-1, keepdims=True) + RMS_EPS) * g


def _sigmoid(x):
    return 0.5 * jnp.tanh(0.5 * x) + 0.5


def _dot(a, b):
    return jnp.dot(a, b, preferred_element_type=F32)


def _dot_nt(a, b):
    return lax.dot_general(a, b, (((1,), (1,)), ((), ())), preferred_element_type=F32)


def _split_bf16(x):
    hi = x.astype(BF16)
    lo = (x - hi.astype(F32)).astype(BF16)
    return hi, lo


(BLK_R, BLK_K, BLK_V, BLK_CX, BLK_CC, BLK_CB, BLK_GA, BLK_GB) = range(8)


def _mixin_kernel(has_vres, tiles_per_seq, *refs):
    n_in = 17 if has_vres else 14
    ins, outs, scr = refs[:n_in], refs[n_in:n_in + 10], refs[n_in + 10:]
    (x_ref, g_ref, wm_ref, wl_ref, mu_ref, mul_ref, wdec_ref, waaa_ref, wgate_ref, w0_ref, a0_ref,
     kk_ref, ka_ref, cw_ref) = ins[:14]
    ro_ref, ko_ref, vo_ref, kko_ref, aao_ref, lwo_ref, ogo_ref, ybo_ref, gao_ref, gbo_ref = outs
    clo_ref, crkv_ref, cu_ref = scr
    if has_vres:
        vf_ref, wvres_ref, v0_ref = ins[14:]
    tm = x_ref.shape[0]

    @pl.when(pl.program_id(0) % tiles_per_seq == 0)
    def _():
        crkv_ref[...] = jnp.zeros_like(crkv_ref)
        clo_ref[...] = jnp.zeros_like(clo_ref)
        cu_ref[...] = jnp.zeros_like(cu_ref)

    xn = _rms(x_ref[...], g_ref[...]).astype(BF16)

    def mixed(z, carry, mu):
        return z + (_shift_rows(z, carry, 1) - z) * mu

    def block(b):
        return _dot(xn, wm_ref[b])

    def shifted_block(idx):
        z = block(idx)
        cols = slice(idx * D, (idx + 1) * D)
        zm = mixed(z, crkv_ref[:, cols], mu_ref[:, cols])
        crkv_ref[:, cols] = z[tm - 8:, :]
        return zm

    lo = _dot(xn, wl_ref[...])
    lom = mixed(lo, clo_ref[...], mul_ref[...])
    clo_ref[...] = lo[tm - 8:, :]
    lane = lax.broadcasted_iota(jnp.int32, (tm, LANES), 1)
    t0, t1, t2 = (lom[:, t * LANES:(t + 1) * LANES] for t in range(3))
    gate_tail = LORA_DECAY + LORA_AAA + LORA_GATE - 2 * LANES
    act_a = jnp.where(lane < LORA_DECAY, jnp.tanh(t0), t0).astype(BF16)
    act_b = jnp.concatenate([_sigmoid(t1), jnp.where(lane < gate_tail, _sigmoid(t2), t2)],
                            axis=1).astype(BF16)

    u = w0_ref[...] + _dot(act_a, wdec_ref[...])
    lwo_ref[...] = -math.exp(-0.5) * _sigmoid(u)
    ro_ref[...] = shifted_block(BLK_R).astype(BF16)

    aa = _sigmoid(a0_ref[...] + _dot(act_a, waaa_ref[...]))
    aao_ref[...] = aa.astype(BF16)
    zm = shifted_block(BLK_K)
    kko_ref[...] = (zm * kk_ref[...]).astype(BF16)
    ko_ref[...] = (zm * (1.0 + (aa - 1.0) * ka_ref[...])).astype(BF16)

    if has_vres:
        vgate = _sigmoid(v0_ref[...] + _dot(act_b, wvres_ref[...]))
    zm = shifted_block(BLK_V)
    if has_vres:
        zm = zm + (vf_ref[...].astype(F32) - zm) * vgate
    vo_ref[...] = zm.astype(BF16)

    ogo_ref[...] = _dot(act_b, wgate_ref[...]).astype(BF16)

    u = block(BLK_CC) * block(BLK_CX)
    cu = cu_ref[...]
    conv = cw_ref[CONV_K - 1:CONV_K, :] * u
    for t in range(CONV_K - 1):
        conv = conv + cw_ref[t:t + 1, :] * _shift_rows(u, cu, CONV_K - 1 - t)
    cu_ref[...] = u[tm - 8:, :]
    ybo_ref[...] = (block(BLK_CB) * conv).astype(BF16)

    gao_ref[...] = block(BLK_GA).astype(BF16)
    gbo_ref[...] = block(BLK_GB).astype(BF16)


def _mixin(x2, vfirst, lw, tm, seq):
    n = x2.shape[0]
    has_vres = vfirst is not None
    tile = pl.BlockSpec((tm, D), lambda i: (i, 0))
    full = lambda a: pl.BlockSpec(a.shape, lambda i: (0,) * a.ndim, pipeline_mode=pl.Buffered(1))
    names = ["norm_mix", "w_main", "w_lora", "mu_rkv", "mu_lora", "wup_dec", "wup_aaa", "wup_gate", "w0",
             "a0", "k_k", "k_a", "conv_w"]
    in_specs = [tile] + [full(lw[name]) for name in names]
    args = [x2] + [lw[name] for name in names]
    if has_vres:
        in_specs += [tile, full(lw["wup_vres"]), full(lw["v0"])]
        args += [vfirst, lw["wup_vres"], lw["v0"]]
    bf = jax.ShapeDtypeStruct((n, D), BF16)
    out_shape = [bf, bf, bf, bf, bf, jax.ShapeDtypeStruct((n, D), F32), bf, bf, bf, bf]
    return pl.pallas_call(
        functools.partial(_mixin_kernel, has_vres, seq // tm),
        grid=(n // tm,),
        in_specs=in_specs,
        out_specs=[tile] * 10,
        out_shape=out_shape,
        scratch_shapes=[pltpu.VMEM((8, LORA_W), F32), pltpu.VMEM((8, 3 * D), F32), pltpu.VMEM((8, D), F32)],
        compiler_params=_cparams(("arbitrary",)),
        name="mixin",
    )(*args)


def _shift_rows(z, prev_rows, k):
    rolled = pltpu.roll(z, k, 0)
    rows = lax.broadcasted_iota(jnp.int32, z.shape, 0)
    out = rolled
    for j in range(k):
        src = prev_rows[8 - k + j:8 - k + j + 1, :]
        out = jnp.where(rows == j, src, out)
    return out


def _pair_expand(x, m0, m1):
    return jnp.concatenate([jnp.where(m0, x, 0.0), jnp.where(m1, x, 0.0)], axis=0)


def _halves_sum(x, m0):
    s0 = jnp.sum(jnp.where(m0, x, 0.0), axis=-1, keepdims=True)
    s1 = jnp.sum(jnp.where(m0, 0.0, x), axis=-1, keepdims=True)
    return jnp.where(m0, s0, s1)


def _wkv_kernel(r_ref, k_ref, v_ref, kk_ref, aa_ref, lw_ref, rk_ref, g_ref, b_ref, o_ref, s_ref):
    @pl.when(pl.program_id(1) == 0)
    def _():
        s_ref[...] = jnp.zeros_like(s_ref)

    c2 = 2 * CHUNK
    lane = lax.broadcasted_iota(jnp.int32, (CHUNK, LANES), 1)
    m0 = lane < HEAD
    m1 = jnp.logical_not(m0)
    rows = lax.broadcasted_iota(jnp.int32, (c2, c2), 0)
    cols = lax.broadcasted_iota(jnp.int32, (c2, c2), 1)
    strict = cols < rows
    incl = cols <= rows
    eye = cols == rows
    tri = (lax.broadcasted_iota(jnp.int32, (CHUNK, CHUNK), 1)
           <= lax.broadcasted_iota(jnp.int32, (CHUNK, CHUNK), 0)).astype(BF16)
    n_chunks = r_ref.shape[0] // CHUNK

    def wave(first_chunk):
        t0s = [pl.multiple_of((first_chunk + u) * CHUNK, CHUNK) for u in range(WKV_UNROLL)]
        probs = [(t0, slice(pi * LANES, (pi + 1) * LANES)) for t0 in t0s for pi in range(N_PAIRS)]
        cat0 = lambda *xs: jnp.concatenate(xs, axis=0)
        cat1 = lambda *xs: jnp.concatenate(xs, axis=1)
        expand = lambda x: _pair_expand(x, m0, m1)
        ld = lambda ref, p: ref[pl.ds(p[0], CHUNK), p[1]].astype(F32)

        lw_all = [lw_ref[pl.ds(t0, CHUNK), :] for t0 in t0s]
        cum_all = []
        for lw_u in lw_all:
            lw_hi, lw_lo = _split_bf16(lw_u)
            cum_all.append(_dot(tri, lw_hi) + _dot(tri, lw_lo))
        lw = [lw_u[:, pi * LANES:(pi + 1) * LANES] for lw_u in lw_all for pi in range(N_PAIRS)]
        cum = [c_u[:, pi * LANES:(pi + 1) * LANES] for c_u in cum_all for pi in range(N_PAIRS)]
        cum_end = [c[CHUNK - 1:CHUNK, :] for c in cum]
        w_inv = [jnp.exp(-c) for c in cum]
        w_end = [jnp.exp(ce - c) for c, ce in zip(cum, cum_end)]
        r = [ld(r_ref, p) for p in probs]
        k = [ld(k_ref, p) for p in probs]
        kk = [ld(kk_ref, p) for p in probs]
        kk = [x / jnp.maximum(jnp.sqrt(_halves_sum(x * x, m0)), L2_EPS) for x in kk]
        b = [x * ld(aa_ref, p) for x, p in zip(kk, probs)]
        a_t = [expand(-x * jnp.exp(c - l)) for x, c, l in zip(kk, cum, lw)]
        r_t = [expand(x * jnp.exp(c)) for x, c in zip(r, cum)]
        lhs1 = [cat0(a, rr).astype(BF16) for a, rr in zip(a_t, r_t)]
        rhs1 = [cat0(expand(bb * wi), expand(kx * wi)).astype(BF16) for bb, kx, wi in zip(b, k, w_inv)]
        aa_m = [_dot_nt(x, y) for x, y in zip(lhs1, rhs1)]
        a_ab = [jnp.where(strict, m[:c2, :c2], 0.0) for m in aa_m]
        a_ak = [jnp.where(strict, m[:c2, c2:], 0.0).astype(BF16) for m in aa_m]
        lhs_top = [cat1(jnp.where(incl, m[c2:, :c2], 0.0), jnp.where(incl, m[c2:, c2:], 0.0)).astype(BF16)
                   for m in aa_m]

        t_m = [jnp.where(eye, 1.0, 0.0) + m for m in a_ab]
        a_b = [m.astype(BF16) for m in a_ab]
        a_m = [_dot(x, x) for x in a_b]
        n_steps = CHUNK.bit_length() - 1
        for step in range(1, n_steps):
            a_b = [m.astype(BF16) for m in a_m]
            if step < n_steps - 1:
                both = [_dot(x, cat1(x, t.astype(BF16))) for x, t in zip(a_b, t_m)]
                a_m = [m[:, :c2] for m in both]
                t_m = [t + m[:, c2:] for t, m in zip(t_m, both)]
            else:
                t_m = [t + _dot(x, t.astype(BF16)) for x, t in zip(a_b, t_m)]

        v = [ld(v_ref, p) for p in probs]
        v_b = [expand(x).astype(BF16) for x in v]
        akv = [_dot(x, y) for x, y in zip(a_ak, v_b)]
        tx = [_dot(t.astype(BF16), cat1(a, y).astype(BF16)) for t, a, y in zip(t_m, a_t, akv)]
        rhs2 = [cat0(x.astype(BF16), cat1(jnp.zeros((c2, c2), BF16), y)) for x, y in zip(tx, v_b)]
        lhs_bot = [cat1(expand(bb * we).T, expand(kx * we).T).astype(BF16)
                   for bb, kx, we in zip(b, k, w_end)]
        z = [_dot(cat0(x, y), w) for x, y, w in zip(lhs_top, lhs_bot, rhs2)]
        gp = [cat0(rt + m[:c2, :c2], m[c2:, :c2] + jnp.where(eye, jnp.exp(ce), 0.0)).astype(BF16)
              for rt, m, ce in zip(r_t, z, cum_end)]

        state = [s_ref[pi] for pi in range(N_PAIRS)]
        y = []
        for u in range(WKV_UNROLL):
            base = u * N_PAIRS
            zz = [_dot(gp[base + pi], state[pi].astype(BF16)) for pi in range(N_PAIRS)]
            state = [zz[pi][c2:] + z[base + pi][c2:, c2:] for pi in range(N_PAIRS)]
            y += [(zz[pi][:CHUNK] + zz[pi][CHUNK:c2])
                  + (z[base + pi][:CHUNK, c2:] + z[base + pi][CHUNK:c2, c2:]) for pi in range(N_PAIRS)]
        for pi in range(N_PAIRS):
            s_ref[pi] = state[pi]

        for i, (t0, sl) in enumerate(probs):
            mu = _halves_sum(y[i], m0) * (1.0 / HEAD)
            yc = y[i] - mu
            var = _halves_sum(yc * yc, m0) * (1.0 / HEAD)
            yn = yc * lax.rsqrt(var + LNX_EPS) * g_ref[:, sl] + b_ref[:, sl]
            bonus = _halves_sum(r[i] * k[i] * rk_ref[:, sl], m0) * v[i]
            o_ref[pl.ds(t0, CHUNK), sl] = (yn + bonus).astype(BF16)

    per_trip = WKV_UNROLL * WKV_WAVES
    assert n_chunks % per_trip == 0

    def trip(ti, carry):
        for w in range(WKV_WAVES):
            wave(ti * per_trip + w * WKV_UNROLL)
        return carry

    lax.fori_loop(0, n_chunks // per_trip, trip, 0)


def _wkv(r, k, v, kk, aa, lw, lwts, bsz, seq, tc):
    n = r.shape[0]
    nt = seq // tc
    blk = pl.BlockSpec((tc, D), lambda b, t: (b * nt + t, 0))
    row = pl.BlockSpec((1, D), lambda b, t: (0, 0))
    return pl.pallas_call(
        _wkv_kernel,
        grid=(bsz, nt),
        in_specs=[blk] * 6 + [row] * 3,
        out_specs=blk,
        out_shape=jax.ShapeDtypeStruct((n, D), BF16),
        scratch_shapes=[pltpu.VMEM((N_PAIRS, LANES, LANES), F32)],
        compiler_params=_cparams(("parallel", "arbitrary")),
        name="wkv",
    )(r, k, v, kk, aa, lw, lwts["r_k"], lwts["lnx_g"], lwts["lnx_b"])


def _memkv_kernel(m_ref, g_ref, w_ref, o_ref):
    o_ref[...] = _dot(_rms(m_ref[...], g_ref[...]).astype(BF16), w_ref[...]).astype(BF16)


def _memkv(mem2, g, w):
    n = mem2.shape[0]
    return pl.pallas_call(
        _memkv_kernel,
        grid=(n // MEM_LEN,),
        in_specs=[pl.BlockSpec((MEM_LEN, D), lambda i: (i, 0)),
                  pl.BlockSpec((1, D), lambda i: (0, 0)),
                  pl.BlockSpec((D, 2 * D), lambda i: (0, 0))],
        out_specs=pl.BlockSpec((MEM_LEN, 2 * D), lambda i: (i, 0)),
        out_shape=jax.ShapeDtypeStruct((n, 2 * D), BF16),
        compiler_params=_cparams(("parallel",)),
        name="memkv",
    )(mem2, g, w)


def _attn_kernel(x_ref, ya_ref, og_ref, yb_ref, ga_ref, gb_ref, kv_ref, gbias_ref, wa_ref, wb_ref,
                 wmix_ref, gq_ref, wq_ref, wo_ref, o_ref):
    y_a = (ya_ref[...].astype(F32) * og_ref[...].astype(F32)).astype(BF16)
    g_a = _sigmoid(ga_ref[...].astype(F32) + gbias_ref[:, :D])
    g_b = _sigmoid(gb_ref[...].astype(F32) + gbias_ref[:, D:])
    merged = g_a * _dot(y_a, wa_ref[...]) + g_b * _dot(yb_ref[...], wb_ref[...])
    x1 = x_ref[...] + _dot(merged.astype(BF16), wmix_ref[...])

    hn = _rms(x1, gq_ref[...]).astype(BF16)
    q = _dot(hn, wq_ref[...]).astype(BF16)
    heads = []
    for h in range(MEM_HEADS):
        sl = slice(h * MEM_HD, (h + 1) * MEM_HD)
        s = _dot_nt(q[:, sl], kv_ref[:, sl]) * (MEM_HD ** -0.5)
        s = s - jnp.max(s, axis=-1, keepdims=True)
        e = jnp.exp(s)
        pr = (e / jnp.sum(e, axis=-1, keepdims=True)).astype(BF16)
        heads.append(_dot(pr, kv_ref[:, D + h * MEM_HD:D + (h + 1) * MEM_HD]).astype(BF16))
    o = jnp.concatenate(heads, axis=1)
    o_ref[...] = x1 + _dot(o, wo_ref[...])


def _attn(x2, ya, og, yb, ga, gb, kv, lw, tm, seq):
    n = x2.shape[0]
    tps = seq // tm
    tile = pl.BlockSpec((tm, D), lambda i: (i, 0))
    full = lambda a: pl.BlockSpec(a.shape, lambda i: (0, 0), pipeline_mode=pl.Buffered(1))
    wts = [lw["gate_b"], lw["w_br_a"], lw["w_br_b"], lw["w_mix_out"], lw["norm_memq"], lw["wq_mem"],
           lw["wo_mem"]]
    return pl.pallas_call(
        _attn_kernel,
        grid=(n // tm,),
        in_specs=[tile] * 6 + [pl.BlockSpec((MEM_LEN, 2 * D), lambda i: (i // tps, 0))]
        + [full(w) for w in wts],
        out_specs=tile,
        out_shape=jax.ShapeDtypeStruct((n, D), F32),
        compiler_params=_cparams(("parallel",)),
        name="attn",
    )(x2, ya, og, yb, ga, gb, kv, *wts)


MOE_TM = 512
MOE_TP = MOE_TM + LANES
MOE_T = 512
RUN_ALIGN = 16
POS_ROW = N_GROUPS + N_EXPERTS
RUN_BITS = tuple(range((MOE_TM // RUN_ALIGN).bit_length() - 1, -1, -1))


def _router_kernel(x_ref, g_ref, rw_ref, rb_ref, hn_ref, meta_ref, pos_ref, cnt_ref):
    tm = x_ref.shape[0]
    hn = _rms(x_ref[...], g_ref[...])
    hi, lo = _split_bf16(hn)
    hn_ref[...] = hi
    lt = _dot_nt(rw_ref[0], hi) + _dot_nt(rw_ref[0], lo) + _dot_nt(rw_ref[1], hi) + rb_ref[...]
    row = lax.broadcasted_iota(jnp.int32, lt.shape, 0).astype(F32)
    neg = jnp.float32(-jnp.inf)
    big = jnp.float32(1 << 20)
    is_g = row < N_GROUPS
    lg = jnp.where(is_g, lt, neg)
    gmax = jnp.max(lg, axis=0, keepdims=True)
    ge = jnp.where(is_g, jnp.exp(lg - gmax), 0.0)
    gp = ge / jnp.sum(ge, axis=0, keepdims=True)
    g_w = jnp.max(gp, axis=0, keepdims=True)
    g_i = jnp.min(jnp.where(is_g & (gp == g_w), row, big), axis=0, keepdims=True)
    first = N_GROUPS + g_i * EPG
    sel = (row >= first) & (row < first + EPG)
    le = jnp.where(sel, lt, neg)
    emax = jnp.max(le, axis=0, keepdims=True)
    ee = jnp.where(sel, jnp.exp(le - emax), 0.0)
    ep = ee / jnp.sum(ee, axis=0, keepdims=True)
    p1 = jnp.max(ep, axis=0, keepdims=True)
    i1 = jnp.min(jnp.where(sel & (ep == p1), row, big), axis=0, keepdims=True)
    rest = sel & (row != i1)
    p2 = jnp.max(jnp.where(rest, ep, -1.0), axis=0, keepdims=True)
    i2 = jnp.min(jnp.where(rest & (ep == p2), row, big), axis=0, keepdims=True)
    tot = p1 + p2
    cw = jnp.where(row == i1, g_w * (p1 / tot), 0.0) + jnp.where(row == i2, g_w * (p2 / tot), 0.0)

    grow = lax.broadcasted_iota(jnp.int32, (8, tm), 0).astype(F32)
    onehot = jnp.where(grow == g_i, 1.0, 0.0)
    before = (lax.broadcasted_iota(jnp.int32, (tm, tm), 0)
              < lax.broadcasted_iota(jnp.int32, (tm, tm), 1)).astype(BF16)
    rank = _dot(onehot.astype(BF16), before)
    count = jnp.sum(onehot, axis=1, keepdims=True)
    run = jnp.floor((count + (RUN_ALIGN - 1)) * (1.0 / RUN_ALIGN)) * RUN_ALIGN
    gcol = lax.broadcasted_iota(jnp.int32, (8, 1), 0)
    start = jnp.zeros((8, 1), F32)
    acc = jnp.zeros((1, 1), F32)
    for g in range(1, N_GROUPS):
        acc = acc + run[g - 1:g, :]
        start = start + jnp.where(gcol == g, acc, 0.0)
    pos = jnp.sum(onehot * (start + rank), axis=0, keepdims=True)
    meta_ref[...] = (cw + jnp.where(row == POS_ROW, pos, 0.0)).T
    pos_ref[0] = jnp.broadcast_to(pos, (8, tm))
    cnt_ref[0] = jnp.broadcast_to(run, (8, LANES)).astype(jnp.int32)


def _router(x2, lw):
    n = x2.shape[0]
    nt = n // MOE_TM
    return pl.pallas_call(
        _router_kernel,
        grid=(nt,),
        in_specs=[pl.BlockSpec((MOE_TM, D), lambda i: (i, 0)),
                  pl.BlockSpec((1, D), lambda i: (0, 0)),
                  pl.BlockSpec((2, LANES, D), lambda i: (0, 0, 0)),
                  pl.BlockSpec((LANES, 1), lambda i: (0, 0))],
        out_specs=[pl.BlockSpec((MOE_TM, D), lambda i: (i, 0)),
                   pl.BlockSpec((MOE_TM, LANES), lambda i: (i, 0)),
                   pl.BlockSpec((1, 8, MOE_TM), lambda i: (i, 0, 0)),
                   pl.BlockSpec((1, 8, LANES), lambda i: (i, 0, 0))],
        out_shape=[jax.ShapeDtypeStruct((n, D), BF16), jax.ShapeDtypeStruct((n, LANES), F32),
                   jax.ShapeDtypeStruct((nt, 8, MOE_TM), F32),
                   jax.ShapeDtypeStruct((nt, 8, LANES), jnp.int32)],
        compiler_params=_cparams(("parallel",)),
        name="router",
    )(x2, lw["norm_ffn"], lw["router_wt"], lw["router_bt"])


def _run_copies(lens_ref, dst_ref, tile, make_copy):
    out = []
    local = 0
    for g in range(N_GROUPS):
        length = lens_ref[tile * N_GROUPS + g]
        base = dst_ref[tile * N_GROUPS + g]
        for bit in RUN_BITS:
            rows = RUN_ALIGN << bit
            done = length & (-2 * rows)
            cond = (length & rows) != 0
            out.append((cond, make_copy(pl.multiple_of(local + done, RUN_ALIGN),
                                        pl.multiple_of(base + done, RUN_ALIGN), rows)))
        local = local + length
    return out


def _dispatch_kernel(lens_ref, dst_ref, end_ref, hn_ref, meta_ref, pos_ref, xs_ref, ms_ref,
                     comp_ref, compm_ref, zx_ref, zm_ref, sem):
    i = pl.program_id(0)
    last = pl.num_programs(0) - 1
    tm = hn_ref.shape[0]

    def copies_of(tile):
        buf = tile % 2

        def make_copy(local, sorted_row, rows):
            return (pltpu.make_async_copy(comp_ref.at[buf, pl.ds(local, rows)],
                                          xs_ref.at[pl.ds(sorted_row, rows)], sem.at[0, buf]),
                    pltpu.make_async_copy(compm_ref.at[buf, pl.ds(local, rows)],
                                          ms_ref.at[pl.ds(sorted_row, rows)], sem.at[1, buf]))

        return _run_copies(lens_ref, dst_ref, tile, make_copy)

    def wait_copies(tile):
        for cond, (cx, cm) in copies_of(tile):
            @pl.when(cond)
            def _(cx=cx, cm=cm):
                cx.wait()
                cm.wait()

    slot = lax.broadcasted_iota(jnp.int32, (MOE_TP, tm), 0).astype(F32)
    perm = jnp.where(slot == pos_ref[0][0:1, :], 1.0, 0.0).astype(BF16)
    comp_ref[i % 2] = _dot(perm, hn_ref[...]).astype(BF16)
    m = meta_ref[...]
    m1 = m.astype(BF16)
    r1 = m - m1.astype(F32)
    m2 = r1.astype(BF16)
    m3 = (r1 - m2.astype(F32)).astype(BF16)
    compm_ref[i % 2] = _dot(perm, m1) + _dot(perm, m2) + _dot(perm, m3)

    for cond, (cx, cm) in copies_of(i):
        @pl.when(cond)
        def _(cx=cx, cm=cm):
            cx.start()
            cm.start()

    @pl.when(i > 0)
    def _():
        wait_copies(i - 1)

    @pl.when(i == last)
    def _():
        wait_copies(i)
        zx_ref[...] = jnp.zeros_like(zx_ref)
        zm_ref[...] = jnp.zeros_like(zm_ref)
        fills = []
        for g in range(N_GROUPS):
            row0 = pl.multiple_of(end_ref[g], RUN_ALIGN)
            fills.append(pltpu.make_async_copy(zx_ref, xs_ref.at[pl.ds(row0, MOE_T)], sem.at[0, 0]))
            fills.append(pltpu.make_async_copy(zm_ref, ms_ref.at[pl.ds(row0, MOE_T)], sem.at[1, 0]))
        for c in fills:
            c.start()
        for c in fills:
            c.wait()


def _dispatch(hn, meta, pos, lens, dst, end, cap):
    n = hn.shape[0]
    nt = n // MOE_TM
    return pl.pallas_call(
        _dispatch_kernel,
        grid_spec=pltpu.PrefetchScalarGridSpec(
            num_scalar_prefetch=3,
            grid=(nt,),
            in_specs=[pl.BlockSpec((MOE_TM, D), lambda i, *_: (i, 0)),
                      pl.BlockSpec((MOE_TM, LANES), lambda i, *_: (i, 0)),
                      pl.BlockSpec((1, 8, MOE_TM), lambda i, *_: (i, 0, 0))],
            out_specs=[pl.BlockSpec(memory_space=pl.ANY), pl.BlockSpec(memory_space=pl.ANY)],
            scratch_shapes=[pltpu.VMEM((2, MOE_TP, D), BF16), pltpu.VMEM((2, MOE_TP, LANES), F32),
                            pltpu.VMEM((MOE_T, D), BF16), pltpu.VMEM((MOE_T, LANES), F32),
                            pltpu.SemaphoreType.DMA((2, 2))]),
        out_shape=[jax.ShapeDtypeStruct((N_GROUPS * cap, D), BF16),
                   jax.ShapeDtypeStruct((N_GROUPS * cap, LANES), F32)],
        compiler_params=_cparams(("arbitrary",)),
        name="dispatch",
    )(lens, dst, end, hn, meta, pos)


def _experts_kernel(tgrp_ref, tblk_ref, nvalid_ref, xs_ref, ms_ref, wg_ref, wu_ref, wd_ref, ys_ref):
    j = pl.program_id(0)

    @pl.when(j < nvalid_ref[0])
    def _():
        xs = xs_ref[...]
        ms = ms_ref[...]
        lane = lax.broadcasted_iota(jnp.int32, ms.shape, 1)
        first = N_GROUPS + tgrp_ref[j] * EPG
        acc = None
        for e in range(EPG):
            gate = _dot(xs, wg_ref[e])
            he = (gate * _sigmoid(gate) * _dot(xs, wu_ref[e])).astype(BF16)
            cw = jnp.sum(jnp.where(lane == first + e, ms, 0.0), axis=-1, keepdims=True)
            term = cw * _dot(he, wd_ref[e])
            acc = term if acc is None else acc + term
        ys_ref[...] = acc.astype(BF16)


def _experts(xs, ms, tgrp, tblk, nvalid, lw, n_tiles):
    wmap = lambda j, tgrp, tblk, nvalid: (tgrp[j], 0, 0)
    rmap = lambda j, tgrp, tblk, nvalid: (tblk[j], 0)
    return pl.pallas_call(
        _experts_kernel,
        grid_spec=pltpu.PrefetchScalarGridSpec(
            num_scalar_prefetch=3,
            grid=(n_tiles,),
            in_specs=[pl.BlockSpec((MOE_T, D), rmap), pl.BlockSpec((MOE_T, LANES), rmap),
                      pl.BlockSpec((EPG, D, D_EXPERT), wmap), pl.BlockSpec((EPG, D, D_EXPERT), wmap),
                      pl.BlockSpec((EPG, D_EXPERT, D), wmap)],
            out_specs=pl.BlockSpec((MOE_T, D), rmap)),
        out_shape=jax.ShapeDtypeStruct(xs.shape, BF16),
        compiler_params=_cparams(("arbitrary",)),
        name="experts",
    )(tgrp, tblk, nvalid, xs, ms, lw["w_e_gate"], lw["w_e_up"], lw["w_e_down"])


def _combine_kernel(final_norm, lens_ref, dst_ref, x_ref, meta_ref, ys_ref, gf_ref, o_ref, comp_ref,
                    sem):
    i = pl.program_id(0)
    tm = x_ref.shape[0]

    def copies_of(tile):
        buf = tile % 2

        def make_copy(local, sorted_row, rows):
            return pltpu.make_async_copy(ys_ref.at[pl.ds(sorted_row, rows)],
                                         comp_ref.at[buf, pl.ds(local, rows)], sem.at[buf])

        return _run_copies(lens_ref, dst_ref, tile, make_copy)

    def start_copies(tile):
        for cond, c in copies_of(tile):
            pl.when(cond)(c.start)

    @pl.when(i == 0)
    def _():
        comp_ref[...] = jnp.zeros_like(comp_ref)
        start_copies(i)

    @pl.when(i < pl.num_programs(0) - 1)
    def _():
        start_copies(i + 1)

    for cond, c in copies_of(i):
        pl.when(cond)(c.wait)

    lane = lax.broadcasted_iota(jnp.int32, meta_ref.shape, 1)
    pos = jnp.sum(jnp.where(lane == POS_ROW, meta_ref[...], 0.0), axis=-1, keepdims=True)
    slot = lax.broadcasted_iota(jnp.int32, (tm, MOE_TP), 1).astype(F32)
    unperm = jnp.where(slot == pos, 1.0, 0.0).astype(BF16)
    out = x_ref[...] + _dot(unperm, comp_ref[i % 2])
    if final_norm:
        out = _rms(out, gf_ref[...])
    o_ref[...] = out


def _combine(x2, meta, ys, lens, dst, norm_f, final_norm):
    n = x2.shape[0]
    return pl.pallas_call(
        functools.partial(_combine_kernel, final_norm),
        grid_spec=pltpu.PrefetchScalarGridSpec(
            num_scalar_prefetch=2,
            grid=(n // MOE_TM,),
            in_specs=[pl.BlockSpec((MOE_TM, D), lambda i, *_: (i, 0)),
                      pl.BlockSpec((MOE_TM, LANES), lambda i, *_: (i, 0)),
                      pl.BlockSpec(memory_space=pl.ANY),
                      pl.BlockSpec((1, D), lambda i, *_: (0, 0))],
            out_specs=pl.BlockSpec((MOE_TM, D), lambda i, *_: (i, 0)),
            scratch_shapes=[pltpu.VMEM((2, MOE_TP, D), BF16), pltpu.SemaphoreType.DMA((2,))]),
        out_shape=jax.ShapeDtypeStruct((n, D), F32),
        compiler_params=_cparams(("arbitrary",)),
        name="combine",
    )(lens, dst, x2, meta, ys, norm_f)


def _moe(x2, lw, norm_f, final_norm):
    n = x2.shape[0]
    nt = n // MOE_TM
    cap = -(-(n + RUN_ALIGN * nt + MOE_T) // MOE_T) * MOE_T
    n_tiles = (n + N_GROUPS * RUN_ALIGN * nt) // MOE_T + N_GROUPS + 1
    hn, meta, pos, cnt = _router(x2, lw)

    lens = cnt[:, :N_GROUPS, 0]
    ends = jnp.cumsum(lens, axis=0)
    dst = (ends - lens) + (jnp.arange(N_GROUPS, dtype=jnp.int32) * cap)[None, :]
    total = ends[-1]
    end = total + jnp.arange(N_GROUPS, dtype=jnp.int32) * cap
    tiles = (total + MOE_T - 1) // MOE_T
    tile_end = jnp.cumsum(tiles)
    tile_start = tile_end - tiles
    nvalid = tile_end[-1]
    jj = jnp.clip(jnp.arange(n_tiles, dtype=jnp.int32), 0, jnp.maximum(nvalid - 1, 0))
    tgrp = jnp.minimum(jnp.sum((jj[:, None] >= tile_end[None, :]).astype(jnp.int32), axis=1), N_GROUPS - 1)
    tblk = tgrp * (cap // MOE_T) + jj - tile_start[tgrp]
    lens_f = lens.reshape(-1).astype(jnp.int32)
    dst_f = dst.reshape(-1).astype(jnp.int32)

    xs, ms = _dispatch(hn, meta, pos, lens_f, dst_f, end.astype(jnp.int32), cap)
    ys = _experts(xs, ms, tgrp.astype(jnp.int32), tblk.astype(jnp.int32),
                  nvalid.reshape(1).astype(jnp.int32), lw, n_tiles)
    return _combine(x2, meta, ys, lens_f, dst_f, norm_f, final_norm)


def _layer_weights(l, w):
    rw = 3 * D
    n_lo = LORA_DECAY + LORA_AAA + LORA_GATE
    w_in = w["w_in"][l].astype(BF16)
    lora_cols = [w_in[:, rw:rw + n_lo]]
    mu_lora = [w["mu_in"][l][rw:rw + n_lo]]
    if l > 0:
        lora_cols.append(w["w_vres_in"][l - 1].astype(BF16))
        mu_lora.append(w["mu_vres"][l - 1])
    lora = jnp.concatenate(lora_cols, axis=1)
    lora = jnp.pad(lora, ((0, 0), (0, LORA_W - lora.shape[1])))
    mu_lora = jnp.concatenate(mu_lora)
    mu_lora = jnp.pad(mu_lora, (0, LORA_W - mu_lora.shape[0]))
    c0 = rw + n_lo
    starts = [0, D, 2 * D, c0, c0 + 2 * D, c0 + D, c0 + 3 * D, c0 + 4 * D]
    w_main = jnp.stack([w_in[:, s:s + D] for s in starts])

    pad_rows = lambda a, before, total: jnp.pad(a, ((before, total - before - a.shape[0]), (0, 0))).astype(BF16)
    wup_dec = pad_rows(w["w_decay_up"][l], 0, LANES)
    wup_aaa = pad_rows(w["w_aaa_up"][l], LORA_DECAY, LANES)
    wup_gate = pad_rows(w["w_gate_up"][l], 0, 2 * LANES)

    router = jnp.concatenate([w["router_g"][l], w["router_e"][l]], axis=1).T
    router = jnp.pad(router, ((0, LANES - router.shape[0]), (0, 0)))
    r_hi = router.astype(BF16)
    r_lo = (router - r_hi.astype(F32)).astype(BF16)
    router_b = jnp.concatenate([w["router_g_b"][l], w["router_e_b"][l]])
    router_b = jnp.pad(router_b, (0, LANES - router_b.shape[0]))[:, None]

    row = lambda a: a.reshape(1, -1)
    out = {
        "norm_mix": row(w["norm_mix"][l]), "w_main": w_main, "w_lora": lora,
        "mu_rkv": row(w["mu_in"][l][:rw]), "mu_lora": row(mu_lora),
        "wup_dec": wup_dec, "wup_aaa": wup_aaa, "wup_gate": wup_gate,
        "w0": row(w["w0"][l]), "a0": row(w["a0"][l]), "k_k": row(w["k_k"][l]), "k_a": row(w["k_a"][l]),
        "conv_w": w["conv_w"][l],
        "r_k": row(w["r_k"][l]), "lnx_g": row(w["lnx_g"][l]), "lnx_b": row(w["lnx_b"][l]),
        "gate_b": row(w["gate_b"][l]),
        "w_br_a": w["w_br_a"][l].astype(BF16), "w_br_b": w["w_br_b"][l].astype(BF16),
        "w_mix_out": w["w_mix_out"][l].astype(BF16),
        "norm_memq": row(w["norm_memq"][l]), "norm_memkv": row(w["norm_memkv"][l]),
        "wq_mem": w["wq_mem"][l].astype(BF16), "wkv_mem": w["wkv_mem"][l].astype(BF16),
        "wo_mem": w["wo_mem"][l].astype(BF16),
        "norm_ffn": row(w["norm_ffn"][l]),
        "router_wt": jnp.stack([r_hi, r_lo]), "router_bt": router_b,
        "w_e_gate": w["w_e_gate"][l].astype(BF16), "w_e_up": w["w_e_up"][l].astype(BF16),
        "w_e_down": w["w_e_down"][l].astype(BF16),
    }
    if l > 0:
        out["v0"] = row(w["v0"][l - 1])
        out["wup_vres"] = pad_rows(w["w_vres_up"][l - 1], LORA_GATE, 2 * LANES)
    return out


def _tile(total, want):
    t = min(want, total)
    assert total % t == 0, (total, t)
    return t


def kernel(x, mem, norm_mix, w_in, mu_in, w_vres_in, mu_vres, w0, w_decay_up, a0, w_aaa_up, w_gate_up,
           v0, w_vres_up, k_k, k_a, r_k, lnx_g, lnx_b, conv_w, gate_b, w_br_a, w_br_b, w_mix_out,
           norm_memq, norm_memkv, wq_mem, wkv_mem, wo_mem, norm_ffn, router_g, router_g_b, router_e,
           router_e_b, w_e_gate, w_e_up, w_e_down, norm_f):
    w = dict(norm_mix=norm_mix, w_in=w_in, mu_in=mu_in, w_vres_in=w_vres_in, mu_vres=mu_vres, w0=w0,
             w_decay_up=w_decay_up, a0=a0, w_aaa_up=w_aaa_up, w_gate_up=w_gate_up, v0=v0,
             w_vres_up=w_vres_up, k_k=k_k, k_a=k_a, r_k=r_k, lnx_g=lnx_g, lnx_b=lnx_b, conv_w=conv_w,
             gate_b=gate_b, w_br_a=w_br_a, w_br_b=w_br_b, w_mix_out=w_mix_out, norm_memq=norm_memq,
             norm_memkv=norm_memkv, wq_mem=wq_mem, wkv_mem=wkv_mem, wo_mem=wo_mem, norm_ffn=norm_ffn,
             router_g=router_g, router_g_b=router_g_b, router_e=router_e, router_e_b=router_e_b,
             w_e_gate=w_e_gate, w_e_up=w_e_up, w_e_down=w_e_down)
    bsz, seq, _ = x.shape
    depth = norm_mix.shape[0]
    n = bsz * seq
    assert seq % CHUNK == 0 and mem.shape[1] == MEM_LEN
    x2 = x.reshape(n, D)
    mem2 = mem.reshape(bsz * MEM_LEN, D)
    tm_in = _tile(seq, 512)
    tc_wkv = _tile(seq, 256)
    tm_attn = _tile(seq, 512)
    assert n % MOE_TM == 0
    norm_f2 = norm_f.reshape(1, D)

    v_first = None
    for l in range(depth):
        lw = _layer_weights(l, w)
        r, k, v, kk, aa, lwd, og, yb, ga, gb = _mixin(x2, v_first, lw, tm_in, seq)
        if l == 0:
            v_first = v
        ya = _wkv(r, k, v, kk, aa, lwd, lw, bsz, seq, tc_wkv)
        kv = _memkv(mem2, lw["norm_memkv"], lw["wkv_mem"])
        x2 = _attn(x2, ya, og, yb, ga, gb, kv, lw, tm_attn, seq)
        x2 = _moe(x2, lw, norm_f2, l == depth - 1)
    return x2.reshape(bsz, seq, D)
```

```python
import functools
import math

import jax
import jax.numpy as jnp
from jax import lax
from jax.experimental import pallas as pl
from jax.experimental.pallas import tpu as pltpu

F32 = jnp.float32
BF16 = jnp.bfloat16

D = 1024
HEAD = 64
N_HEADS = 16
LANES = 128
N_PAIRS = D // LANES
CHUNK = 64
WKV_UNROLL = 2
WKV_WAVES = 2
LORA_DECAY, LORA_AAA, LORA_GATE, LORA_VRES = 64, 64, 160, 32
LORA_W = 512
CONV_K = 3
MEM_LEN = 256
MEM_HEADS = 4
MEM_HD = D // MEM_HEADS
N_GROUPS, EPG, N_EXPERTS, D_EXPERT = 4, 4, 16, 512
RMS_EPS, LNX_EPS, L2_EPS = 1e-6, 64e-5, 1e-12
VMEM_LIMIT = 56 * 1024 * 1024


def _cparams(sem):
    return pltpu.CompilerParams(dimension_semantics=sem, vmem_limit_bytes=VMEM_LIMIT)


def _rms(x, g):
    return x * lax.rsqrt(jnp.mean(x * x, axis=-1, keepdims=True) + RMS_EPS) * g


def _sigmoid(x):
    return 0.5 * jnp.tanh(0.5 * x) + 0.5


def _dot(a, b):
    return jnp.dot(a, b, preferred_element_type=F32)


def _dot_nt(a, b):
    return lax.dot_general(a, b, (((1,), (1,)), ((), ())), preferred_element_type=F32)


def _split_bf16(x):
    hi = x.astype(BF16)
    lo = (x - hi.astype(F32)).astype(BF16)
    return hi, lo


(BLK_R, BLK_K, BLK_V, BLK_CX, BLK_CC, BLK_CB, BLK_GA, BLK_GB) = range(8)


def _mixin_kernel(has_vres, tiles_per_seq, *refs):
    n_in = 17 if has_vres else 14
    ins, outs, scr = refs[:n_in], refs[n_in:n_in + 10], refs[n_in + 10:]
    (x_ref, g_ref, wm_ref, wl_ref, mu_ref, mul_ref, wdec_ref, waaa_ref, wgate_ref, w0_ref, a0_ref,
     kk_ref, ka_ref, cw_ref) = ins[:14]
    ro_ref, ko_ref, vo_ref, kko_ref, aao_ref, lwo_ref, ogo_ref, ybo_ref, gao_ref, gbo_ref = outs
    clo_ref, crkv_ref, cu_ref = scr
    if has_vres:
        vf_ref, wvres_ref, v0_ref = ins[14:]
    tm = x_ref.shape[0]

    @pl.when(pl.program_id(0) % tiles_per_seq == 0)
    def _():
        crkv_ref[...] = jnp.zeros_like(crkv_ref)
        clo_ref[...] = jnp.zeros_like(clo_ref)
        cu_ref[...] = jnp.zeros_like(cu_ref)

    xn = _rms(x_ref[...], g_ref[...]).astype(BF16)

    def mixed(z, carry, mu):
        return z + (_shift_rows(z, carry, 1) - z) * mu

    def block(b):
        return _dot(xn, wm_ref[b])

    def shifted_block(idx):
        z = block(idx)
        cols = slice(idx * D, (idx + 1) * D)
        zm = mixed(z, crkv_ref[:, cols], mu_ref[:, cols])
        crkv_ref[:, cols] = z[tm - 8:, :]
        return zm

    lo = _dot(xn, wl_ref[...])
    lom = mixed(lo, clo_ref[...], mul_ref[...])
    clo_ref[...] = lo[tm - 8:, :]
    lane = lax.broadcasted_iota(jnp.int32, (tm, LANES), 1)
    t0, t1, t2 = (lom[:, t * LANES:(t + 1) * LANES] for t in range(3))
    gate_tail = LORA_DECAY + LORA_AAA + LORA_GATE - 2 * LANES
    act_a = jnp.where(lane < LORA_DECAY, jnp.tanh(t0), t0).astype(BF16)
    act_b = jnp.concatenate([_sigmoid(t1), jnp.where(lane < gate_tail, _sigmoid(t2), t2)],
                            axis=1).astype(BF16)

    u = w0_ref[...] + _dot(act_a, wdec_ref[...])
    lwo_ref[...] = -math.exp(-0.5) * _sigmoid(u)
    ro_ref[...] = shifted_block(BLK_R).astype(BF16)

    aa = _sigmoid(a0_ref[...] + _dot(act_a, waaa_ref[...]))
    aao_ref[...] = aa.astype(BF16)
    zm = shifted_block(BLK_K)
    kko_ref[...] = (zm * kk_ref[...]).astype(BF16)
    ko_ref[...] = (zm * (1.0 + (aa - 1.0) * ka_ref[...])).astype(BF16)

    if has_vres:
        vgate = _sigmoid(v0_ref[...] + _dot(act_b, wvres_ref[...]))
    zm = shifted_block(BLK_V)
    if has_vres:
        zm = zm + (vf_ref[...].astype(F32) - zm) * vgate
    vo_ref[...] = zm.astype(BF16)

    ogo_ref[...] = _dot(act_b, wgate_ref[...]).astype(BF16)

    u = block(BLK_CC) * block(BLK_CX)
    cu = cu_ref[...]
    conv = cw_ref[CONV_K - 1:CONV_K, :] * u
    for t in range(CONV_K - 1):
        conv = conv + cw_ref[t:t + 1, :] * _shift_rows(u, cu, CONV_K - 1 - t)
    cu_ref[...] = u[tm - 8:, :]
    ybo_ref[...] = (block(BLK_CB) * conv).astype(BF16)

    gao_ref[...] = block(BLK_GA).astype(BF16)
    gbo_ref[...] = block(BLK_GB).astype(BF16)


def _mixin(x2, vfirst, wts, l, tm, seq):
    n = x2.shape[0]
    has_vres = vfirst is not None
    tile = pl.BlockSpec((tm, D), lambda i: (i, 0))
    names = ["norm_mix", "w_main", "w_lora", "mu_rkv", "mu_lora", "wup_dec", "wup_aaa", "wup_gate", "w0",
             "a0", "k_k", "k_a", "conv_w"]
    in_specs = [tile] + [_layer_block(wts[name], l) for name in names]
    args = [x2] + [wts[name] for name in names]
    if has_vres:
        in_specs += [tile, _layer_block(wts["wup_vres"], l - 1), _layer_block(wts["v0"], l - 1)]
        args += [vfirst, wts["wup_vres"], wts["v0"]]
    bf = jax.ShapeDtypeStruct((n, D), BF16)
    out_shape = [bf, bf, bf, bf, bf, jax.ShapeDtypeStruct((n, D), F32), bf, bf, bf, bf]
    return pl.pallas_call(
        functools.partial(_mixin_kernel, has_vres, seq // tm),
        grid=(n // tm,),
        in_specs=in_specs,
        out_specs=[tile] * 10,
        out_shape=out_shape,
        scratch_shapes=[pltpu.VMEM((8, LORA_W), F32), pltpu.VMEM((8, 3 * D), F32), pltpu.VMEM((8, D), F32)],
        compiler_params=_cparams(("arbitrary",)),
        name="mixin",
    )(*args)


def _shift_rows(z, prev_rows, k):
    rolled = pltpu.roll(z, k, 0)
    rows = lax.broadcasted_iota(jnp.int32, z.shape, 0)
    out = rolled
    for j in range(k):
        src = prev_rows[8 - k + j:8 - k + j + 1, :]
        out = jnp.where(rows == j, src, out)
    return out


def _pair_expand(x, m0, m1):
    return jnp.concatenate([jnp.where(m0, x, 0.0), jnp.where(m1, x, 0.0)], axis=0)


def _halves_sum(x, m0):
    s0 = jnp.sum(jnp.where(m0, x, 0.0), axis=-1, keepdims=True)
    s1 = jnp.sum(jnp.where(m0, 0.0, x), axis=-1, keepdims=True)
    return jnp.where(m0, s0, s1)


def _wkv_kernel(r_ref, k_ref, v_ref, kk_ref, aa_ref, lw_ref, rk_ref, g_ref, b_ref, o_ref, s_ref):
    @pl.when(pl.program_id(1) == 0)
    def _():
        s_ref[...] = jnp.zeros_like(s_ref)

    c2 = 2 * CHUNK
    lane = lax.broadcasted_iota(jnp.int32, (CHUNK, LANES), 1)
    m0 = lane < HEAD
    m1 = jnp.logical_not(m0)
    rows = lax.broadcasted_iota(jnp.int32, (c2, c2), 0)
    cols = lax.broadcasted_iota(jnp.int32, (c2, c2), 1)
    strict = cols < rows
    incl = cols <= rows
    eye = cols == rows
    tri = (lax.broadcasted_iota(jnp.int32, (CHUNK, CHUNK), 1)
           <= lax.broadcasted_iota(jnp.int32, (CHUNK, CHUNK), 0)).astype(BF16)
    n_chunks = r_ref.shape[0] // CHUNK

    def wave(first_chunk):
        t0s = [pl.multiple_of((first_chunk + u) * CHUNK, CHUNK) for u in range(WKV_UNROLL)]
        probs = [(t0, slice(pi * LANES, (pi + 1) * LANES)) for t0 in t0s for pi in range(N_PAIRS)]
        cat0 = lambda *xs: jnp.concatenate(xs, axis=0)
        cat1 = lambda *xs: jnp.concatenate(xs, axis=1)
        expand = lambda x: _pair_expand(x, m0, m1)
        ld = lambda ref, p: ref[pl.ds(p[0], CHUNK), p[1]].astype(F32)

        lw_all = [lw_ref[pl.ds(t0, CHUNK), :] for t0 in t0s]
        cum_all = []
        for lw_u in lw_all:
            lw_hi, lw_lo = _split_bf16(lw_u)
            cum_all.append(_dot(tri, lw_hi) + _dot(tri, lw_lo))
        lw = [lw_u[:, pi * LANES:(pi + 1) * LANES] for lw_u in lw_all for pi in range(N_PAIRS)]
        cum = [c_u[:, pi * LANES:(pi + 1) * LANES] for c_u in cum_all for pi in range(N_PAIRS)]
        cum_end = [c[CHUNK - 1:CHUNK, :] for c in cum]
        w_inv = [jnp.exp(-c) for c in cum]
        w_end = [jnp.exp(ce - c) for c, ce in zip(cum, cum_end)]
        r = [ld(r_ref, p) for p in probs]
        k = [ld(k_ref, p) for p in probs]
        kk = [ld(kk_ref, p) for p in probs]
        kk = [x / jnp.maximum(jnp.sqrt(_halves_sum(x * x, m0)), L2_EPS) for x in kk]
        b = [x * ld(aa_ref, p) for x, p in zip(kk, probs)]
        a_t = [expand(-x * jnp.exp(c - l)) for x, c, l in zip(kk, cum, lw)]
        r_t = [expand(x * jnp.exp(c)) for x, c in zip(r, cum)]
        lhs1 = [cat0(a, rr).astype(BF16) for a, rr in zip(a_t, r_t)]
        rhs1 = [cat0(expand(bb * wi), expand(kx * wi)).astype(BF16) for bb, kx, wi in zip(b, k, w_inv)]
        aa_m = [_dot_nt(x, y) for x, y in zip(lhs1, rhs1)]
        a_ab = [jnp.where(strict, m[:c2, :c2], 0.0) for m in aa_m]
        a_ak = [jnp.where(strict, m[:c2, c2:], 0.0).astype(BF16) for m in aa_m]
        lhs_top = [cat1(jnp.where(incl, m[c2:, :c2], 0.0), jnp.where(incl, m[c2:, c2:], 0.0)).astype(BF16)
                   for m in aa_m]

        t_m = [jnp.where(eye, 1.0, 0.0) + m for m in a_ab]
        a_b = [m.astype(BF16) for m in a_ab]
        a_m = [_dot(x, x) for x in a_b]
        n_steps = CHUNK.bit_length() - 1
        for step in range(1, n_steps):
            a_b = [m.astype(BF16) for m in a_m]
            if step < n_steps - 1:
                both = [_dot(x, cat1(x, t.astype(BF16))) for x, t in zip(a_b, t_m)]
                a_m = [m[:, :c2] for m in both]
                t_m = [t + m[:, c2:] for t, m in zip(t_m, both)]
            else:
                t_m = [t + _dot(x, t.astype(BF16)) for x, t in zip(a_b, t_m)]

        v = [ld(v_ref, p) for p in probs]
        v_b = [expand(x).astype(BF16) for x in v]
        akv = [_dot(x, y) for x, y in zip(a_ak, v_b)]
        tx = [_dot(t.astype(BF16), cat1(a, y).astype(BF16)) for t, a, y in zip(t_m, a_t, akv)]
        rhs2 = [cat0(x.astype(BF16), cat1(jnp.zeros((c2, c2), BF16), y)) for x, y in zip(tx, v_b)]
        lhs_bot = [cat1(expand(bb * we).T, expand(kx * we).T).astype(BF16)
                   for bb, kx, we in zip(b, k, w_end)]
        z = [_dot(cat0(x, y), w) for x, y, w in zip(lhs_top, lhs_bot, rhs2)]
        gp = [cat0(rt + m[:c2, :c2], m[c2:, :c2] + jnp.where(eye, jnp.exp(ce), 0.0)).astype(BF16)
              for rt, m, ce in zip(r_t, z, cum_end)]

        state = [s_ref[pi] for pi in range(N_PAIRS)]
        y = []
        for u in range(WKV_UNROLL):
            base = u * N_PAIRS
            zz = [_dot(gp[base + pi], state[pi].astype(BF16)) for pi in range(N_PAIRS)]
            state = [zz[pi][c2:] + z[base + pi][c2:, c2:] for pi in range(N_PAIRS)]
            y += [(zz[pi][:CHUNK] + zz[pi][CHUNK:c2])
                  + (z[base + pi][:CHUNK, c2:] + z[base + pi][CHUNK:c2, c2:]) for pi in range(N_PAIRS)]
        for pi in range(N_PAIRS):
            s_ref[pi] = state[pi]

        for i, (t0, sl) in enumerate(probs):
            mu = _halves_sum(y[i], m0) * (1.0 / HEAD)
            yc = y[i] - mu
            var = _halves_sum(yc * yc, m0) * (1.0 / HEAD)
            yn = yc * lax.rsqrt(var + LNX_EPS) * g_ref[:, sl] + b_ref[:, sl]
            bonus = _halves_sum(r[i] * k[i] * rk_ref[:, sl], m0) * v[i]
            o_ref[pl.ds(t0, CHUNK), sl] = (yn + bonus).astype(BF16)

    per_trip = WKV_UNROLL * WKV_WAVES
    assert n_chunks % per_trip == 0

    def trip(ti, carry):
        for w in range(WKV_WAVES):
            wave(ti * per_trip + w * WKV_UNROLL)
        return carry

    lax.fori_loop(0, n_chunks // per_trip, trip, 0)


def _wkv(r, k, v, kk, aa, lw, wts, l, bsz, seq, tc):
    n = r.shape[0]
    nt = seq // tc
    blk = pl.BlockSpec((tc, D), lambda b, t: (b * nt + t, 0))
    rows = [wts["r_k"], wts["lnx_g"], wts["lnx_b"]]
    return pl.pallas_call(
        _wkv_kernel,
        grid=(bsz, nt),
        in_specs=[blk] * 6 + [_layer_block(a, l) for a in rows],
        out_specs=blk,
        out_shape=jax.ShapeDtypeStruct((n, D), BF16),
        scratch_shapes=[pltpu.VMEM((N_PAIRS, LANES, LANES), F32)],
        compiler_params=_cparams(("parallel", "arbitrary")),
        name="wkv",
    )(r, k, v, kk, aa, lw, *rows)


def _memkv_kernel(m_ref, g_ref, w_ref, o_ref):
    o_ref[...] = _dot(_rms(m_ref[...], g_ref[...]).astype(BF16), w_ref[...]).astype(BF16)


def _memkv(mem2, wts, l):
    n = mem2.shape[0]
    g, w = wts["norm_memkv"], wts["wkv_mem"]
    return pl.pallas_call(
        _memkv_kernel,
        grid=(n // MEM_LEN,),
        in_specs=[pl.BlockSpec((MEM_LEN, D), lambda i: (i, 0)), _layer_block(g, l), _layer_block(w, l)],
        out_specs=pl.BlockSpec((MEM_LEN, 2 * D), lambda i: (i, 0)),
        out_shape=jax.ShapeDtypeStruct((n, 2 * D), BF16),
        compiler_params=_cparams(("parallel",)),
        name="memkv",
    )(mem2, g, w)


def _attn_kernel(x_ref, ya_ref, og_ref, yb_ref, ga_ref, gb_ref, kv_ref, gbias_ref, wa_ref, wb_ref,
                 wmix_ref, gq_ref, wq_ref, wo_ref, o_ref):
    y_a = (ya_ref[...].astype(F32) * og_ref[...].astype(F32)).astype(BF16)
    g_a = _sigmoid(ga_ref[...].astype(F32) + gbias_ref[:, :D])
    g_b = _sigmoid(gb_ref[...].astype(F32) + gbias_ref[:, D:])
    merged = g_a * _dot(y_a, wa_ref[...]) + g_b * _dot(yb_ref[...], wb_ref[...])
    x1 = x_ref[...] + _dot(merged.astype(BF16), wmix_ref[...])

    hn = _rms(x1, gq_ref[...]).astype(BF16)
    q = _dot(hn, wq_ref[...]).astype(BF16)
    heads = []
    for h in range(MEM_HEADS):
        sl = slice(h * MEM_HD, (h + 1) * MEM_HD)
        s = _dot_nt(q[:, sl], kv_ref[:, sl]) * (MEM_HD ** -0.5)
        s = s - jnp.max(s, axis=-1, keepdims=True)
        e = jnp.exp(s)
        pr = (e / jnp.sum(e, axis=-1, keepdims=True)).astype(BF16)
        heads.append(_dot(pr, kv_ref[:, D + h * MEM_HD:D + (h + 1) * MEM_HD]).astype(BF16))
    o = jnp.concatenate(heads, axis=1)
    o_ref[...] = x1 + _dot(o, wo_ref[...])


def _attn(x2, ya, og, yb, ga, gb, kv, wts, l, tm, seq):
    n = x2.shape[0]
    tps = seq // tm
    tile = pl.BlockSpec((tm, D), lambda i: (i, 0))
    params = [wts[name] for name in ("gate_b", "w_br_a", "w_br_b", "w_mix_out", "norm_memq", "wq_mem",
                                     "wo_mem")]
    return pl.pallas_call(
        _attn_kernel,
        grid=(n // tm,),
        in_specs=[tile] * 6 + [pl.BlockSpec((MEM_LEN, 2 * D), lambda i: (i // tps, 0))]
        + [_layer_block(a, l) for a in params],
        out_specs=tile,
        out_shape=jax.ShapeDtypeStruct((n, D), F32),
        compiler_params=_cparams(("parallel",)),
        name="attn",
    )(x2, ya, og, yb, ga, gb, kv, *params)


MOE_TM = 512
MOE_TP = MOE_TM + LANES
MOE_T = 512
RUN_ALIGN = 16
POS_ROW = N_GROUPS + N_EXPERTS
RUN_BITS = tuple(range((MOE_TM // RUN_ALIGN).bit_length() - 1, -1, -1))


def _router_kernel(x_ref, g_ref, rw_ref, rb_ref, hn_ref, meta_ref, pos_ref, cnt_ref):
    tm = x_ref.shape[0]
    hn = _rms(x_ref[...], g_ref[...])
    hi, lo = _split_bf16(hn)
    hn_ref[...] = hi
    lt = _dot_nt(rw_ref[0], hi) + _dot_nt(rw_ref[0], lo) + _dot_nt(rw_ref[1], hi) + rb_ref[...]
    row = lax.broadcasted_iota(jnp.int32, lt.shape, 0).astype(F32)
    neg = jnp.float32(-jnp.inf)
    big = jnp.float32(1 << 20)
    is_g = row < N_GROUPS
    lg = jnp.where(is_g, lt, neg)
    gmax = jnp.max(lg, axis=0, keepdims=True)
    ge = jnp.where(is_g, jnp.exp(lg - gmax), 0.0)
    gp = ge / jnp.sum(ge, axis=0, keepdims=True)
    g_w = jnp.max(gp, axis=0, keepdims=True)
    g_i = jnp.min(jnp.where(is_g & (gp == g_w), row, big), axis=0, keepdims=True)
    first = N_GROUPS + g_i * EPG
    sel = (row >= first) & (row < first + EPG)
    le = jnp.where(sel, lt, neg)
    emax = jnp.max(le, axis=0, keepdims=True)
    ee = jnp.where(sel, jnp.exp(le - emax), 0.0)
    ep = ee / jnp.sum(ee, axis=0, keepdims=True)
    p1 = jnp.max(ep, axis=0, keepdims=True)
    i1 = jnp.min(jnp.where(sel & (ep == p1), row, big), axis=0, keepdims=True)
    rest = sel & (row != i1)
    p2 = jnp.max(jnp.where(rest, ep, -1.0), axis=0, keepdims=True)
    i2 = jnp.min(jnp.where(rest & (ep == p2), row, big), axis=0, keepdims=True)
    tot = p1 + p2
    cw = jnp.where(row == i1, g_w * (p1 / tot), 0.0) + jnp.where(row == i2, g_w * (p2 / tot), 0.0)

    grow = lax.broadcasted_iota(jnp.int32, (8, tm), 0).astype(F32)
    onehot = jnp.where(grow == g_i, 1.0, 0.0)
    before = (lax.broadcasted_iota(jnp.int32, (tm, tm), 0)
              < lax.broadcasted_iota(jnp.int32, (tm, tm), 1)).astype(BF16)
    rank = _dot(onehot.astype(BF16), before)
    count = jnp.sum(onehot, axis=1, keepdims=True)
    run = jnp.floor((count + (RUN_ALIGN - 1)) * (1.0 / RUN_ALIGN)) * RUN_ALIGN
    gcol = lax.broadcasted_iota(jnp.int32, (8, 1), 0)
    start = jnp.zeros((8, 1), F32)
    acc = jnp.zeros((1, 1), F32)
    for g in range(1, N_GROUPS):
        acc = acc + run[g - 1:g, :]
        start = start + jnp.where(gcol == g, acc, 0.0)
    pos = jnp.sum(onehot * (start + rank), axis=0, keepdims=True)
    meta_ref[...] = (cw + jnp.where(row == POS_ROW, pos, 0.0)).T
    pos_ref[0] = jnp.broadcast_to(pos, (8, tm))
    cnt_ref[0] = jnp.broadcast_to(run, (8, LANES)).astype(jnp.int32)


def _router(x2, wts, l):
    n = x2.shape[0]
    nt = n // MOE_TM
    params = [wts["norm_ffn"], wts["router_wt"], wts["router_bt"]]
    return pl.pallas_call(
        _router_kernel,
        grid=(nt,),
        in_specs=[pl.BlockSpec((MOE_TM, D), lambda i: (i, 0))] + [_layer_block(a, l) for a in params],
        out_specs=[pl.BlockSpec((MOE_TM, D), lambda i: (i, 0)),
                   pl.BlockSpec((MOE_TM, LANES), lambda i: (i, 0)),
                   pl.BlockSpec((1, 8, MOE_TM), lambda i: (i, 0, 0)),
                   pl.BlockSpec((1, 8, LANES), lambda i: (i, 0, 0))],
        out_shape=[jax.ShapeDtypeStruct((n, D), BF16), jax.ShapeDtypeStruct((n, LANES), F32),
                   jax.ShapeDtypeStruct((nt, 8, MOE_TM), F32),
                   jax.ShapeDtypeStruct((nt, 8, LANES), jnp.int32)],
        compiler_params=_cparams(("parallel",)),
        name="router",
    )(x2, *params)


def _advance_offsets(lens_ref, off_ref, tile, cap):
    @pl.when(tile == 0)
    def _():
        for g in range(N_GROUPS):
            off_ref[0, g] = g * cap

    @pl.when(tile > 0)
    def _():
        for g in range(N_GROUPS):
            off_ref[tile % 2, g] = off_ref[(tile + 1) % 2, g] + lens_ref[(tile - 1) * N_GROUPS + g]


def _run_copies(lens_ref, off_ref, tile, make_copy):
    out = []
    local = 0
    for g in range(N_GROUPS):
        length = lens_ref[tile * N_GROUPS + g]
        base = off_ref[tile % 2, g]
        for bit in RUN_BITS:
            rows = RUN_ALIGN << bit
            done = length & (-2 * rows)
            cond = (length & rows) != 0
            out.append((cond, make_copy(pl.multiple_of(local + done, RUN_ALIGN),
                                        pl.multiple_of(base + done, RUN_ALIGN), rows)))
        local = local + length
    return out


def _dispatch_kernel(cap, lens_ref, hn_ref, meta_ref, pos_ref, xs_ref, ms_ref,
                     comp_ref, compm_ref, zx_ref, zm_ref, off_ref, sem):
    i = pl.program_id(0)
    last = pl.num_programs(0) - 1
    tm = hn_ref.shape[0]
    _advance_offsets(lens_ref, off_ref, i, cap)

    def copies_of(tile):
        buf = tile % 2

        def make_copy(local, sorted_row, rows):
            return (pltpu.make_async_copy(comp_ref.at[buf, pl.ds(local, rows)],
                                          xs_ref.at[pl.ds(sorted_row, rows)], sem.at[0, buf]),
                    pltpu.make_async_copy(compm_ref.at[buf, pl.ds(local, rows)],
                                          ms_ref.at[pl.ds(sorted_row, rows)], sem.at[1, buf]))

        return _run_copies(lens_ref, off_ref, tile, make_copy)

    def wait_copies(tile):
        for cond, (cx, cm) in copies_of(tile):
            @pl.when(cond)
            def _(cx=cx, cm=cm):
                cx.wait()
                cm.wait()

    slot = lax.broadcasted_iota(jnp.int32, (MOE_TP, tm), 0).astype(F32)
    perm = jnp.where(slot == pos_ref[0][0:1, :], 1.0, 0.0).astype(BF16)
    comp_ref[i % 2] = _dot(perm, hn_ref[...]).astype(BF16)
    m = meta_ref[...]
    m1 = m.astype(BF16)
    r1 = m - m1.astype(F32)
    m2 = r1.astype(BF16)
    m3 = (r1 - m2.astype(F32)).astype(BF16)
    compm_ref[i % 2] = _dot(perm, m1) + _dot(perm, m2) + _dot(perm, m3)

    for cond, (cx, cm) in copies_of(i):
        @pl.when(cond)
        def _(cx=cx, cm=cm):
            cx.start()
            cm.start()

    @pl.when(i > 0)
    def _():
        wait_copies(i - 1)

    @pl.when(i == last)
    def _():
        wait_copies(i)
        zx_ref[...] = jnp.zeros_like(zx_ref)
        zm_ref[...] = jnp.zeros_like(zm_ref)
        fills = []
        for g in range(N_GROUPS):
            row0 = pl.multiple_of(off_ref[i % 2, g] + lens_ref[i * N_GROUPS + g], RUN_ALIGN)
            fills.append(pltpu.make_async_copy(zx_ref, xs_ref.at[pl.ds(row0, MOE_T)], sem.at[0, 0]))
            fills.append(pltpu.make_async_copy(zm_ref, ms_ref.at[pl.ds(row0, MOE_T)], sem.at[1, 0]))
        for c in fills:
            c.start()
        for c in fills:
            c.wait()


def _dispatch(hn, meta, pos, lens, cap):
    n = hn.shape[0]
    nt = n // MOE_TM
    return pl.pallas_call(
        functools.partial(_dispatch_kernel, cap),
        grid_spec=pltpu.PrefetchScalarGridSpec(
            num_scalar_prefetch=1,
            grid=(nt,),
            in_specs=[pl.BlockSpec((MOE_TM, D), lambda i, *_: (i, 0)),
                      pl.BlockSpec((MOE_TM, LANES), lambda i, *_: (i, 0)),
                      pl.BlockSpec((1, 8, MOE_TM), lambda i, *_: (i, 0, 0))],
            out_specs=[pl.BlockSpec(memory_space=pl.ANY), pl.BlockSpec(memory_space=pl.ANY)],
            scratch_shapes=[pltpu.VMEM((2, MOE_TP, D), BF16), pltpu.VMEM((2, MOE_TP, LANES), F32),
                            pltpu.VMEM((MOE_T, D), BF16), pltpu.VMEM((MOE_T, LANES), F32),
                            pltpu.SMEM((2, N_GROUPS), jnp.int32), pltpu.SemaphoreType.DMA((2, 2))]),
        out_shape=[jax.ShapeDtypeStruct((N_GROUPS * cap, D), BF16),
                   jax.ShapeDtypeStruct((N_GROUPS * cap, LANES), F32)],
        compiler_params=_cparams(("arbitrary",)),
        name="dispatch",
    )(lens, hn, meta, pos)


def _tile_group(j, tend_ref):
    jc = jnp.minimum(j, jnp.maximum(tend_ref[N_GROUPS - 1] - 1, 0))
    g = sum((jc >= tend_ref[t]).astype(jnp.int32) for t in range(N_GROUPS - 1))
    return jc, g


def _experts_kernel(tend_ref, tstart_ref, xs_ref, ms_ref, wg_ref, wu_ref, wd_ref, ys_ref):
    j = pl.program_id(0)

    @pl.when(j < tend_ref[N_GROUPS - 1])
    def _():
        xs = xs_ref[...]
        ms = ms_ref[...]
        lane = lax.broadcasted_iota(jnp.int32, ms.shape, 1)
        first = N_GROUPS + _tile_group(j, tend_ref)[1] * EPG
        acc = None
        for e in range(EPG):
            gate = _dot(xs, wg_ref[e])
            he = (gate * _sigmoid(gate) * _dot(xs, wu_ref[e])).astype(BF16)
            cw = jnp.sum(jnp.where(lane == first + e, ms, 0.0), axis=-1, keepdims=True)
            term = cw * _dot(he, wd_ref[e])
            acc = term if acc is None else acc + term
        ys_ref[...] = acc.astype(BF16)


def _experts(xs, ms, tend, tstart, wts, l, n_tiles, cap):
    def wmap(j, tend, tstart):
        return (l * N_GROUPS + _tile_group(j, tend)[1], 0, 0)

    def rmap(j, tend, tstart):
        jc, g = _tile_group(j, tend)
        return (g * (cap // MOE_T) + jc - tstart[g], 0)

    return pl.pallas_call(
        _experts_kernel,
        grid_spec=pltpu.PrefetchScalarGridSpec(
            num_scalar_prefetch=2,
            grid=(n_tiles,),
            in_specs=[pl.BlockSpec((MOE_T, D), rmap), pl.BlockSpec((MOE_T, LANES), rmap),
                      pl.BlockSpec((EPG, D, D_EXPERT), wmap), pl.BlockSpec((EPG, D, D_EXPERT), wmap),
                      pl.BlockSpec((EPG, D_EXPERT, D), wmap)],
            out_specs=pl.BlockSpec((MOE_T, D), rmap)),
        out_shape=jax.ShapeDtypeStruct(xs.shape, BF16),
        compiler_params=_cparams(("arbitrary",)),
        name="experts",
    )(tend, tstart, xs, ms, wts["w_e_gate"], wts["w_e_up"], wts["w_e_down"])


def _combine_kernel(final_norm, cap, lens_ref, x_ref, meta_ref, ys_ref, gf_ref, o_ref, comp_ref,
                    off_ref, sem):
    i = pl.program_id(0)
    tm = x_ref.shape[0]

    def copies_of(tile):
        buf = tile % 2

        def make_copy(local, sorted_row, rows):
            return pltpu.make_async_copy(ys_ref.at[pl.ds(sorted_row, rows)],
                                         comp_ref.at[buf, pl.ds(local, rows)], sem.at[buf])

        return _run_copies(lens_ref, off_ref, tile, make_copy)

    def start_copies(tile):
        _advance_offsets(lens_ref, off_ref, tile, cap)
        for cond, c in copies_of(tile):
            pl.when(cond)(c.start)

    @pl.when(i == 0)
    def _():
        comp_ref[...] = jnp.zeros_like(comp_ref)
        start_copies(i)

    @pl.when(i < pl.num_programs(0) - 1)
    def _():
        start_copies(i + 1)

    for cond, c in copies_of(i):
        pl.when(cond)(c.wait)

    lane = lax.broadcasted_iota(jnp.int32, meta_ref.shape, 1)
    pos = jnp.sum(jnp.where(lane == POS_ROW, meta_ref[...], 0.0), axis=-1, keepdims=True)
    slot = lax.broadcasted_iota(jnp.int32, (tm, MOE_TP), 1).astype(F32)
    unperm = jnp.where(slot == pos, 1.0, 0.0).astype(BF16)
    out = x_ref[...] + _dot(unperm, comp_ref[i % 2])
    if final_norm:
        out = _rms(out, gf_ref[...])
    o_ref[...] = out


def _combine(x2, meta, ys, lens, norm_f, final_norm, cap):
    n = x2.shape[0]
    return pl.pallas_call(
        functools.partial(_combine_kernel, final_norm, cap),
        grid_spec=pltpu.PrefetchScalarGridSpec(
            num_scalar_prefetch=1,
            grid=(n // MOE_TM,),
            in_specs=[pl.BlockSpec((MOE_TM, D), lambda i, *_: (i, 0)),
                      pl.BlockSpec((MOE_TM, LANES), lambda i, *_: (i, 0)),
                      pl.BlockSpec(memory_space=pl.ANY),
                      pl.BlockSpec((1, D), lambda i, *_: (0, 0))],
            out_specs=pl.BlockSpec((MOE_TM, D), lambda i, *_: (i, 0)),
            scratch_shapes=[pltpu.VMEM((2, MOE_TP, D), BF16), pltpu.SMEM((2, N_GROUPS), jnp.int32),
                            pltpu.SemaphoreType.DMA((2,))]),
        out_shape=jax.ShapeDtypeStruct((n, D), F32),
        compiler_params=_cparams(("arbitrary",)),
        name="combine",
    )(lens, x2, meta, ys, norm_f)


def _moe(x2, wts, l, norm_f, final_norm):
    n = x2.shape[0]
    nt = n // MOE_TM
    cap = -(-(n + RUN_ALIGN * nt + MOE_T) // MOE_T) * MOE_T
    n_tiles = (n + N_GROUPS * RUN_ALIGN * nt) // MOE_T + N_GROUPS + 1
    hn, meta, pos, cnt = _router(x2, wts, l)

    lens = cnt[:, :N_GROUPS, 0]
    tiles = (jnp.sum(lens, axis=0) + (MOE_T - 1)) // MOE_T
    tend = jnp.cumsum(tiles).astype(jnp.int32)
    tstart = tend - tiles
    lens = lens.reshape(-1)

    xs, ms = _dispatch(hn, meta, pos, lens, cap)
    ys = _experts(xs, ms, tend, tstart, wts, l, n_tiles, cap)
    return _combine(x2, meta, ys, lens, norm_f, final_norm, cap)


def _prepare_weights(w):
    depth = w["w_in"].shape[0]
    rw = 3 * D
    n_lo = LORA_DECAY + LORA_AAA + LORA_GATE
    w_in = w["w_in"]
    vres_in = jnp.pad(w["w_vres_in"], ((1, 0), (0, 0), (0, 0)))
    lora = jnp.concatenate([w_in[:, :, rw:rw + n_lo], vres_in], axis=2)
    lora = jnp.pad(lora, ((0, 0), (0, 0), (0, LORA_W - lora.shape[2]))).astype(BF16)
    mu_lora = jnp.concatenate([w["mu_in"][:, rw:rw + n_lo], jnp.pad(w["mu_vres"], ((1, 0), (0, 0)))], axis=1)
    mu_lora = jnp.pad(mu_lora, ((0, 0), (0, LORA_W - mu_lora.shape[1])))
    c0 = rw + n_lo
    starts = [0, D, 2 * D, c0, c0 + 2 * D, c0 + D, c0 + 3 * D, c0 + 4 * D]
    w_main = jnp.stack([w_in[:, :, s:s + D] for s in starts], axis=1).astype(BF16)

    def pad_rows(a, before, total):
        return jnp.pad(a, ((0, 0), (before, total - before - a.shape[1]), (0, 0))).astype(BF16)

    router = jnp.swapaxes(jnp.concatenate([w["router_g"], w["router_e"]], axis=2), 1, 2)
    router = jnp.pad(router, ((0, 0), (0, LANES - router.shape[1]), (0, 0)))
    r_hi = router.astype(BF16)
    r_lo = (router - r_hi.astype(F32)).astype(BF16)
    router_b = jnp.concatenate([w["router_g_b"], w["router_e_b"]], axis=1)
    router_b = jnp.pad(router_b, ((0, 0), (0, LANES - router_b.shape[1])))[:, :, None]

    row = lambda a: a.reshape(a.shape[0], 1, -1)
    experts = lambda a: a.astype(BF16).reshape(depth * N_EXPERTS, a.shape[2], a.shape[3])
    return {
        "norm_mix": row(w["norm_mix"]), "w_main": w_main, "w_lora": lora,
        "mu_rkv": row(w["mu_in"][:, :rw]), "mu_lora": row(mu_lora),
        "wup_dec": pad_rows(w["w_decay_up"], 0, LANES),
        "wup_aaa": pad_rows(w["w_aaa_up"], LORA_DECAY, LANES),
        "wup_gate": pad_rows(w["w_gate_up"], 0, 2 * LANES),
        "wup_vres": pad_rows(w["w_vres_up"], LORA_GATE, 2 * LANES), "v0": row(w["v0"]),
        "w0": row(w["w0"]), "a0": row(w["a0"]), "k_k": row(w["k_k"]), "k_a": row(w["k_a"]),
        "conv_w": w["conv_w"],
        "r_k": row(w["r_k"]), "lnx_g": row(w["lnx_g"]), "lnx_b": row(w["lnx_b"]),
        "gate_b": row(w["gate_b"]),
        "w_br_a": w["w_br_a"].astype(BF16), "w_br_b": w["w_br_b"].astype(BF16),
        "w_mix_out": w["w_mix_out"].astype(BF16),
        "norm_memq": row(w["norm_memq"]), "norm_memkv": row(w["norm_memkv"]),
        "wq_mem": w["wq_mem"].astype(BF16), "wkv_mem": w["wkv_mem"].astype(BF16),
        "wo_mem": w["wo_mem"].astype(BF16),
        "norm_ffn": row(w["norm_ffn"]),
        "router_wt": jnp.stack([r_hi, r_lo], axis=1), "router_bt": router_b,
        "w_e_gate": experts(w["w_e_gate"]), "w_e_up": experts(w["w_e_up"]),
        "w_e_down": experts(w["w_e_down"]),
    }


def _layer_block(a, l):
    index = (l,) + (0,) * (a.ndim - 1)
    return pl.BlockSpec((None,) + a.shape[1:], lambda *_: index, pipeline_mode=pl.Buffered(1))


def _tile(total, want):
    t = min(want, total)
    assert total % t == 0, (total, t)
    return t


def kernel(x, mem, norm_mix, w_in, mu_in, w_vres_in, mu_vres, w0, w_decay_up, a0, w_aaa_up, w_gate_up,
           v0, w_vres_up, k_k, k_a, r_k, lnx_g, lnx_b, conv_w, gate_b, w_br_a, w_br_b, w_mix_out,
           norm_memq, norm_memkv, wq_mem, wkv_mem, wo_mem, norm_ffn, router_g, router_g_b, router_e,
           router_e_b, w_e_gate, w_e_up, w_e_down, norm_f):
    w = dict(norm_mix=norm_mix, w_in=w_in, mu_in=mu_in, w_vres_in=w_vres_in, mu_vres=mu_vres, w0=w0,
             w_decay_up=w_decay_up, a0=a0, w_aaa_up=w_aaa_up, w_gate_up=w_gate_up, v0=v0,
             w_vres_up=w_vres_up, k_k=k_k, k_a=k_a, r_k=r_k, lnx_g=lnx_g, lnx_b=lnx_b, conv_w=conv_w,
             gate_b=gate_b, w_br_a=w_br_a, w_br_b=w_br_b, w_mix_out=w_mix_out, norm_memq=norm_memq,
             norm_memkv=norm_memkv, wq_mem=wq_mem, wkv_mem=wkv_mem, wo_mem=wo_mem, norm_ffn=norm_ffn,
             router_g=router_g, router_g_b=router_g_b, router_e=router_e, router_e_b=router_e_b,
             w_e_gate=w_e_gate, w_e_up=w_e_up, w_e_down=w_e_down)
    bsz, seq, _ = x.shape
    depth = norm_mix.shape[0]
    n = bsz * seq
    assert seq % CHUNK == 0 and mem.shape[1] == MEM_LEN
    x2 = x.reshape(n, D)
    mem2 = mem.reshape(bsz * MEM_LEN, D)
    tm_in = _tile(seq, 512)
    tc_wkv = _tile(seq, 256)
    tm_attn = _tile(seq, 512)
    assert n % MOE_TM == 0
    norm_f2 = norm_f.reshape(1, D)

    wts = _prepare_weights(w)
    v_first = None
    for l in range(depth):
        r, k, v, kk, aa, lwd, og, yb, ga, gb = _mixin(x2, v_first, wts, l, tm_in, seq)
        if l == 0:
            v_first = v
        ya = _wkv(r, k, v, kk, aa, lwd, wts, l, bsz, seq, tc_wkv)
        kv = _memkv(mem2, wts, l)
        x2 = _attn(x2, ya, og, yb, ga, gb, kv, wts, l, tm_attn, seq)
        x2 = _moe(x2, wts, l, norm_f2, l == depth - 1)
    return x2.reshape(bsz, seq, D)
```

```python
import functools
import math

import jax
import jax.numpy as jnp
from jax import lax
from jax.experimental import pallas as pl
from jax.experimental.pallas import tpu as pltpu

F32 = jnp.float32
BF16 = jnp.bfloat16

D = 1024
HEAD = 64
N_HEADS = 16
LANES = 128
N_PAIRS = D // LANES
CHUNK = 64
WKV_UNROLL = 2
WKV_WAVES = 4
LORA_DECAY, LORA_AAA, LORA_GATE, LORA_VRES = 64, 64, 160, 32
LORA_W = 512
CONV_K = 3
MEM_LEN = 256
MEM_HEADS = 4
MEM_HD = D // MEM_HEADS
N_GROUPS, EPG, N_EXPERTS, D_EXPERT = 4, 4, 16, 512
RMS_EPS, LNX_EPS, L2_EPS = 1e-6, 64e-5, 1e-12
VMEM_LIMIT = 56 * 1024 * 1024


def _cparams(sem):
    return pltpu.CompilerParams(dimension_semantics=sem, vmem_limit_bytes=VMEM_LIMIT)


def _rms(x, g):
    return x * lax.rsqrt(jnp.mean(x * x, axis=-1, keepdims=True) + RMS_EPS) * g


def _sigmoid(x):
    return 0.5 * jnp.tanh(0.5 * x) + 0.5


def _dot(a, b):
    return jnp.dot(a, b, preferred_element_type=F32)


def _dot_nt(a, b):
    return lax.dot_general(a, b, (((1,), (1,)), ((), ())), preferred_element_type=F32)


def _split_bf16(x):
    hi = x.astype(BF16)
    lo = (x - hi.astype(F32)).astype(BF16)
    return hi, lo


BLOCKS = (BLK_R, BLK_K, BLK_V, BLK_CX, BLK_CC, BLK_CB, BLK_GA, BLK_GB) = tuple(range(8))


def _mixin_kernel(has_vres, tiles_per_seq, *refs):
    n_in = 17 if has_vres else 14
    ins, outs, scr = refs[:n_in], refs[n_in:n_in + 10], refs[n_in + 10:]
    (x_ref, g_ref, wm_ref, wl_ref, mu_ref, mul_ref, wdec_ref, waaa_ref, wgate_ref, w0_ref, a0_ref,
     kk_ref, ka_ref, cw_ref) = ins[:14]
    ro_ref, ko_ref, vo_ref, kko_ref, aao_ref, lwo_ref, ogo_ref, ybo_ref, gao_ref, gbo_ref = outs
    clo_ref, crkv_ref, cu_ref = scr
    if has_vres:
        vf_ref, wvres_ref, v0_ref = ins[14:]
    tm = x_ref.shape[0]

    @pl.when(pl.program_id(0) % tiles_per_seq == 0)
    def _():
        crkv_ref[...] = jnp.zeros_like(crkv_ref)
        clo_ref[...] = jnp.zeros_like(clo_ref)
        cu_ref[...] = jnp.zeros_like(cu_ref)

    xn = _rms(x_ref[...], g_ref[...]).astype(BF16)

    def mixed(z, carry, mu):
        return z + (_shift_rows(z, carry, 1) - z) * mu

    def block(b):
        return _dot(xn, wm_ref[b])

    def shifted_block(idx):
        z = block(idx)
        cols = slice(idx * D, (idx + 1) * D)
        zm = mixed(z, crkv_ref[:, cols], mu_ref[:, cols])
        crkv_ref[:, cols] = z[tm - 8:, :]
        return zm

    lo = _dot(xn, wl_ref[...])
    lom = mixed(lo, clo_ref[...], mul_ref[...])
    clo_ref[...] = lo[tm - 8:, :]
    lane = lax.broadcasted_iota(jnp.int32, (tm, LANES), 1)
    t0, t1, t2 = (lom[:, t * LANES:(t + 1) * LANES] for t in range(3))
    gate_tail = LORA_DECAY + LORA_AAA + LORA_GATE - 2 * LANES
    act_a = jnp.where(lane < LORA_DECAY, jnp.tanh(t0), t0).astype(BF16)
    act_b = jnp.concatenate([_sigmoid(t1), jnp.where(lane < gate_tail, _sigmoid(t2), t2)],
                            axis=1).astype(BF16)

    u = w0_ref[...] + _dot(act_a, wdec_ref[...])
    lwo_ref[...] = -math.exp(-0.5) * _sigmoid(u)
    ro_ref[...] = shifted_block(BLK_R).astype(BF16)

    aa = _sigmoid(a0_ref[...] + _dot(act_a, waaa_ref[...]))
    aao_ref[...] = aa.astype(BF16)
    zm = shifted_block(BLK_K)
    kko_ref[...] = (zm * kk_ref[...]).astype(BF16)
    ko_ref[...] = (zm * (1.0 + (aa - 1.0) * ka_ref[...])).astype(BF16)

    if has_vres:
        vgate = _sigmoid(v0_ref[...] + _dot(act_b, wvres_ref[...]))
    zm = shifted_block(BLK_V)
    if has_vres:
        zm = zm + (vf_ref[...].astype(F32) - zm) * vgate
    vo_ref[...] = zm.astype(BF16)

    ogo_ref[...] = _dot(act_b, wgate_ref[...]).astype(BF16)

    u = block(BLK_CC) * block(BLK_CX)
    cu = cu_ref[...]
    conv = cw_ref[CONV_K - 1:CONV_K, :] * u
    for t in range(CONV_K - 1):
        conv = conv + cw_ref[t:t + 1, :] * _shift_rows(u, cu, CONV_K - 1 - t)
    cu_ref[...] = u[tm - 8:, :]
    ybo_ref[...] = (block(BLK_CB) * conv).astype(BF16)

    gao_ref[...] = block(BLK_GA).astype(BF16)
    gbo_ref[...] = block(BLK_GB).astype(BF16)


def _mixin(x2, vfirst, wts, l, tm, seq):
    n = x2.shape[0]
    has_vres = vfirst is not None
    tile = pl.BlockSpec((tm, D), lambda i: (i, 0))
    names = ["norm_mix", "w_main", "w_lora", "mu_rkv", "mu_lora", "wup_dec", "wup_aaa", "wup_gate", "w0",
             "a0", "k_k", "k_a", "conv_w"]
    in_specs = [tile] + [_layer_block(wts[name], l) for name in names]
    in_specs[1 + names.index("w_main")] = pl.BlockSpec(
        (len(BLOCKS), None, D, D), lambda i: (0, l, 0, 0), pipeline_mode=pl.Buffered(1))
    args = [x2] + [wts[name] for name in names]
    if has_vres:
        in_specs += [tile, _layer_block(wts["wup_vres"], l - 1), _layer_block(wts["v0"], l - 1)]
        args += [vfirst, wts["wup_vres"], wts["v0"]]
    bf = jax.ShapeDtypeStruct((n, D), BF16)
    out_shape = [bf, bf, bf, bf, bf, jax.ShapeDtypeStruct((n, D), F32), bf, bf, bf, bf]
    return pl.pallas_call(
        functools.partial(_mixin_kernel, has_vres, seq // tm),
        grid=(n // tm,),
        in_specs=in_specs,
        out_specs=[tile] * 10,
        out_shape=out_shape,
        scratch_shapes=[pltpu.VMEM((8, LORA_W), F32), pltpu.VMEM((8, 3 * D), F32), pltpu.VMEM((8, D), F32)],
        compiler_params=_cparams(("arbitrary",)),
        name="mixin",
    )(*args)


def _shift_rows(z, prev_rows, k):
    rolled = pltpu.roll(z, k, 0)
    rows = lax.broadcasted_iota(jnp.int32, z.shape, 0)
    out = rolled
    for j in range(k):
        src = prev_rows[8 - k + j:8 - k + j + 1, :]
        out = jnp.where(rows == j, src, out)
    return out


def _pair_expand(x, m0, m1):
    return jnp.concatenate([jnp.where(m0, x, 0.0), jnp.where(m1, x, 0.0)], axis=0)


def _halves_sum(x, m0):
    s0 = jnp.sum(jnp.where(m0, x, 0.0), axis=-1, keepdims=True)
    s1 = jnp.sum(jnp.where(m0, 0.0, x), axis=-1, keepdims=True)
    return jnp.where(m0, s0, s1)


def _wkv_kernel(r_ref, k_ref, v_ref, kk_ref, aa_ref, lw_ref, rk_ref, g_ref, b_ref, o_ref, s_ref):
    @pl.when(pl.program_id(1) == 0)
    def _():
        s_ref[...] = jnp.zeros_like(s_ref)

    c2 = 2 * CHUNK
    lane = lax.broadcasted_iota(jnp.int32, (CHUNK, LANES), 1)
    m0 = lane < HEAD
    m1 = jnp.logical_not(m0)
    rows = lax.broadcasted_iota(jnp.int32, (c2, c2), 0)
    cols = lax.broadcasted_iota(jnp.int32, (c2, c2), 1)
    strict = cols < rows
    incl = cols <= rows
    eye = cols == rows
    tri = (lax.broadcasted_iota(jnp.int32, (CHUNK, CHUNK), 1)
           <= lax.broadcasted_iota(jnp.int32, (CHUNK, CHUNK), 0)).astype(BF16)
    n_chunks = r_ref.shape[0] // CHUNK

    def wave(first_chunk):
        t0s = [pl.multiple_of((first_chunk + u) * CHUNK, CHUNK) for u in range(WKV_UNROLL)]
        probs = [(t0, slice(pi * LANES, (pi + 1) * LANES)) for t0 in t0s for pi in range(N_PAIRS)]
        cat0 = lambda *xs: jnp.concatenate(xs, axis=0)
        cat1 = lambda *xs: jnp.concatenate(xs, axis=1)
        expand = lambda x: _pair_expand(x, m0, m1)
        ld = lambda ref, p: ref[pl.ds(p[0], CHUNK), p[1]].astype(F32)

        lw_all = [lw_ref[pl.ds(t0, CHUNK), :] for t0 in t0s]
        cum_all = []
        for lw_u in lw_all:
            lw_hi, lw_lo = _split_bf16(lw_u)
            cum_all.append(_dot(tri, lw_hi) + _dot(tri, lw_lo))
        lw = [lw_u[:, pi * LANES:(pi + 1) * LANES] for lw_u in lw_all for pi in range(N_PAIRS)]
        cum = [c_u[:, pi * LANES:(pi + 1) * LANES] for c_u in cum_all for pi in range(N_PAIRS)]
        cum_end = [c[CHUNK - 1:CHUNK, :] for c in cum]
        w_inv = [jnp.exp(-c) for c in cum]
        w_end = [jnp.exp(ce - c) for c, ce in zip(cum, cum_end)]
        r = [ld(r_ref, p) for p in probs]
        k = [ld(k_ref, p) for p in probs]
        kk = [ld(kk_ref, p) for p in probs]
        kk = [x / jnp.maximum(jnp.sqrt(_halves_sum(x * x, m0)), L2_EPS) for x in kk]
        b = [x * ld(aa_ref, p) for x, p in zip(kk, probs)]
        a_t = [expand(-x * jnp.exp(c - l)) for x, c, l in zip(kk, cum, lw)]
        r_t = [expand(x * jnp.exp(c)) for x, c in zip(r, cum)]
        lhs1 = [cat0(a, rr).astype(BF16) for a, rr in zip(a_t, r_t)]
        rhs1 = [cat0(expand(bb * wi), expand(kx * wi)).astype(BF16) for bb, kx, wi in zip(b, k, w_inv)]
        aa_m = [_dot_nt(x, y) for x, y in zip(lhs1, rhs1)]
        a_ab = [jnp.where(strict, m[:c2, :c2], 0.0) for m in aa_m]
        a_ak = [jnp.where(strict, m[:c2, c2:], 0.0).astype(BF16) for m in aa_m]
        lhs_top = [cat1(jnp.where(incl, m[c2:, :c2], 0.0), jnp.where(incl, m[c2:, c2:], 0.0)).astype(BF16)
                   for m in aa_m]

        t_m = [jnp.where(eye, 1.0, 0.0) + m for m in a_ab]
        a_b = [m.astype(BF16) for m in a_ab]
        a_m = [_dot(x, x) for x in a_b]
        n_steps = CHUNK.bit_length() - 1
        for step in range(1, n_steps):
            a_b = [m.astype(BF16) for m in a_m]
            if step < n_steps - 1:
                both = [_dot(x, cat1(x, t.astype(BF16))) for x, t in zip(a_b, t_m)]
                a_m = [m[:, :c2] for m in both]
                t_m = [t + m[:, c2:] for t, m in zip(t_m, both)]
            else:
                t_m = [t + _dot(x, t.astype(BF16)) for x, t in zip(a_b, t_m)]

        v = [ld(v_ref, p) for p in probs]
        v_b = [expand(x).astype(BF16) for x in v]
        akv = [_dot(x, y) for x, y in zip(a_ak, v_b)]
        tx = [_dot(t.astype(BF16), cat1(a, y).astype(BF16)) for t, a, y in zip(t_m, a_t, akv)]
        rhs2 = [cat0(x.astype(BF16), cat1(jnp.zeros((c2, c2), BF16), y)) for x, y in zip(tx, v_b)]
        lhs_bot = [cat1(expand(bb * we).T, expand(kx * we).T).astype(BF16)
                   for bb, kx, we in zip(b, k, w_end)]
        z = [_dot(cat0(x, y), w) for x, y, w in zip(lhs_top, lhs_bot, rhs2)]
        gp = [cat0(rt + m[:c2, :c2], m[c2:, :c2] + jnp.where(eye, jnp.exp(ce), 0.0)).astype(BF16)
              for rt, m, ce in zip(r_t, z, cum_end)]

        state = [s_ref[pi] for pi in range(N_PAIRS)]
        y = []
        for u in range(WKV_UNROLL):
            base = u * N_PAIRS
            zz = [_dot(gp[base + pi], state[pi].astype(BF16)) for pi in range(N_PAIRS)]
            state = [zz[pi][c2:] + z[base + pi][c2:, c2:] for pi in range(N_PAIRS)]
            y += [(zz[pi][:CHUNK] + zz[pi][CHUNK:c2])
                  + (z[base + pi][:CHUNK, c2:] + z[base + pi][CHUNK:c2, c2:]) for pi in range(N_PAIRS)]
        for pi in range(N_PAIRS):
            s_ref[pi] = state[pi]

        for i, (t0, sl) in enumerate(probs):
            mu = _halves_sum(y[i], m0) * (1.0 / HEAD)
            yc = y[i] - mu
            var = _halves_sum(yc * yc, m0) * (1.0 / HEAD)
            yn = yc * lax.rsqrt(var + LNX_EPS) * g_ref[:, sl] + b_ref[:, sl]
            bonus = _halves_sum(r[i] * k[i] * rk_ref[:, sl], m0) * v[i]
            o_ref[pl.ds(t0, CHUNK), sl] = (yn + bonus).astype(BF16)

    per_trip = WKV_UNROLL * WKV_WAVES
    assert n_chunks % per_trip == 0

    def trip(ti, carry):
        for w in range(WKV_WAVES):
            wave(ti * per_trip + w * WKV_UNROLL)
        return carry

    lax.fori_loop(0, n_chunks // per_trip, trip, 0)


def _wkv(r, k, v, kk, aa, lw, wts, l, bsz, seq, tc):
    n = r.shape[0]
    nt = seq // tc
    blk = pl.BlockSpec((tc, D), lambda b, t: (b * nt + t, 0))
    rows = [wts["r_k"], wts["lnx_g"], wts["lnx_b"]]
    return pl.pallas_call(
        _wkv_kernel,
        grid=(bsz, nt),
        in_specs=[blk] * 6 + [_layer_block(a, l) for a in rows],
        out_specs=blk,
        out_shape=jax.ShapeDtypeStruct((n, D), BF16),
        scratch_shapes=[pltpu.VMEM((N_PAIRS, LANES, LANES), F32)],
        compiler_params=_cparams(("parallel", "arbitrary")),
        name="wkv",
    )(r, k, v, kk, aa, lw, *rows)


def _memkv_kernel(m_ref, g_ref, w_ref, o_ref):
    o_ref[...] = _dot(_rms(m_ref[...], g_ref[...]).astype(BF16), w_ref[...]).astype(BF16)


def _memkv(mem2, wts, l):
    n = mem2.shape[0]
    g, w = wts["norm_memkv"], wts["wkv_mem"]
    return pl.pallas_call(
        _memkv_kernel,
        grid=(n // MEM_LEN,),
        in_specs=[pl.BlockSpec((MEM_LEN, D), lambda i: (i, 0)), _layer_block(g, l), _layer_block(w, l)],
        out_specs=pl.BlockSpec((MEM_LEN, 2 * D), lambda i: (i, 0)),
        out_shape=jax.ShapeDtypeStruct((n, 2 * D), BF16),
        compiler_params=_cparams(("parallel",)),
        name="memkv",
    )(mem2, g, w)


def _attn_kernel(x_ref, ya_ref, og_ref, yb_ref, ga_ref, gb_ref, kv_ref, gbias_ref, wa_ref, wb_ref,
                 wmix_ref, gq_ref, wq_ref, wo_ref, o_ref):
    y_a = (ya_ref[...].astype(F32) * og_ref[...].astype(F32)).astype(BF16)
    g_a = _sigmoid(ga_ref[...].astype(F32) + gbias_ref[:, :D])
    g_b = _sigmoid(gb_ref[...].astype(F32) + gbias_ref[:, D:])
    merged = g_a * _dot(y_a, wa_ref[...]) + g_b * _dot(yb_ref[...], wb_ref[...])
    x1 = x_ref[...] + _dot(merged.astype(BF16), wmix_ref[...])

    hn = _rms(x1, gq_ref[...]).astype(BF16)
    q = _dot(hn, wq_ref[...]).astype(BF16)
    heads = []
    for h in range(MEM_HEADS):
        sl = slice(h * MEM_HD, (h + 1) * MEM_HD)
        s = _dot_nt(q[:, sl], kv_ref[:, sl]) * (MEM_HD ** -0.5)
        s = s - jnp.max(s, axis=-1, keepdims=True)
        e = jnp.exp(s)
        pr = (e / jnp.sum(e, axis=-1, keepdims=True)).astype(BF16)
        heads.append(_dot(pr, kv_ref[:, D + h * MEM_HD:D + (h + 1) * MEM_HD]).astype(BF16))
    o = jnp.concatenate(heads, axis=1)
    o_ref[...] = x1 + _dot(o, wo_ref[...])


def _attn(x2, ya, og, yb, ga, gb, kv, wts, l, tm, seq):
    n = x2.shape[0]
    tps = seq // tm
    tile = pl.BlockSpec((tm, D), lambda i: (i, 0))
    params = [wts[name] for name in ("gate_b", "w_br_a", "w_br_b", "w_mix_out", "norm_memq", "wq_mem",
                                     "wo_mem")]
    return pl.pallas_call(
        _attn_kernel,
        grid=(n // tm,),
        in_specs=[tile] * 6 + [pl.BlockSpec((MEM_LEN, 2 * D), lambda i: (i // tps, 0))]
        + [_layer_block(a, l) for a in params],
        out_specs=tile,
        out_shape=jax.ShapeDtypeStruct((n, D), F32),
        compiler_params=_cparams(("parallel",)),
        name="attn",
    )(x2, ya, og, yb, ga, gb, kv, *params)


MOE_TM = 512
MOE_TP = MOE_TM + LANES
MOE_T = 512
RUN_ALIGN = 16
POS_ROW = N_GROUPS + N_EXPERTS
RUN_BITS = tuple(range((MOE_TM // RUN_ALIGN).bit_length() - 1, -1, -1))


def _router_kernel(x_ref, g_ref, rw_ref, rb_ref, hn_ref, meta_ref, pos_ref, cnt_ref):
    tm = x_ref.shape[0]
    hn = _rms(x_ref[...], g_ref[...])
    hi, lo = _split_bf16(hn)
    hn_ref[...] = hi
    lt = _dot_nt(rw_ref[0], hi) + _dot_nt(rw_ref[0], lo) + _dot_nt(rw_ref[1], hi) + rb_ref[...]
    row = lax.broadcasted_iota(jnp.int32, lt.shape, 0).astype(F32)
    neg = jnp.float32(-jnp.inf)
    big = jnp.float32(1 << 20)
    is_g = row < N_GROUPS
    lg = jnp.where(is_g, lt, neg)
    gmax = jnp.max(lg, axis=0, keepdims=True)
    ge = jnp.where(is_g, jnp.exp(lg - gmax), 0.0)
    gp = ge / jnp.sum(ge, axis=0, keepdims=True)
    g_w = jnp.max(gp, axis=0, keepdims=True)
    g_i = jnp.min(jnp.where(is_g & (gp == g_w), row, big), axis=0, keepdims=True)
    first = N_GROUPS + g_i * EPG
    sel = (row >= first) & (row < first + EPG)
    le = jnp.where(sel, lt, neg)
    emax = jnp.max(le, axis=0, keepdims=True)
    ee = jnp.where(sel, jnp.exp(le - emax), 0.0)
    ep = ee / jnp.sum(ee, axis=0, keepdims=True)
    p1 = jnp.max(ep, axis=0, keepdims=True)
    i1 = jnp.min(jnp.where(sel & (ep == p1), row, big), axis=0, keepdims=True)
    rest = sel & (row != i1)
    p2 = jnp.max(jnp.where(rest, ep, -1.0), axis=0, keepdims=True)
    i2 = jnp.min(jnp.where(rest & (ep == p2), row, big), axis=0, keepdims=True)
    tot = p1 + p2
    cw = jnp.where(row == i1, g_w * (p1 / tot), 0.0) + jnp.where(row == i2, g_w * (p2 / tot), 0.0)

    grow = lax.broadcasted_iota(jnp.int32, (8, tm), 0).astype(F32)
    onehot = jnp.where(grow == g_i, 1.0, 0.0)
    before = (lax.broadcasted_iota(jnp.int32, (tm, tm), 0)
              < lax.broadcasted_iota(jnp.int32, (tm, tm), 1)).astype(BF16)
    rank = _dot(onehot.astype(BF16), before)
    count = jnp.sum(onehot, axis=1, keepdims=True)
    run = jnp.floor((count + (RUN_ALIGN - 1)) * (1.0 / RUN_ALIGN)) * RUN_ALIGN
    gcol = lax.broadcasted_iota(jnp.int32, (8, 1), 0)
    start = jnp.zeros((8, 1), F32)
    acc = jnp.zeros((1, 1), F32)
    for g in range(1, N_GROUPS):
        acc = acc + run[g - 1:g, :]
        start = start + jnp.where(gcol == g, acc, 0.0)
    pos = jnp.sum(onehot * (start + rank), axis=0, keepdims=True)
    meta_ref[...] = (cw + jnp.where(row == POS_ROW, pos, 0.0)).T
    pos_ref[0] = jnp.broadcast_to(pos, (8, tm))
    cnt_ref[0] = jnp.broadcast_to(run, (8, LANES)).astype(jnp.int32)


def _router(x2, wts, l):
    n = x2.shape[0]
    nt = n // MOE_TM
    params = [wts["norm_ffn"], wts["router_wt"], wts["router_bt"]]
    return pl.pallas_call(
        _router_kernel,
        grid=(nt,),
        in_specs=[pl.BlockSpec((MOE_TM, D), lambda i: (i, 0))] + [_layer_block(a, l) for a in params],
        out_specs=[pl.BlockSpec((MOE_TM, D), lambda i: (i, 0)),
                   pl.BlockSpec((MOE_TM, LANES), lambda i: (i, 0)),
                   pl.BlockSpec((1, 8, MOE_TM), lambda i: (i, 0, 0)),
                   pl.BlockSpec((1, 8, LANES), lambda i: (i, 0, 0))],
        out_shape=[jax.ShapeDtypeStruct((n, D), BF16), jax.ShapeDtypeStruct((n, LANES), F32),
                   jax.ShapeDtypeStruct((nt, 8, MOE_TM), F32),
                   jax.ShapeDtypeStruct((nt, 8, LANES), jnp.int32)],
        compiler_params=_cparams(("parallel",)),
        name="router",
    )(x2, *params)


def _advance_offsets(lens_ref, off_ref, tile, cap):
    @pl.when(tile == 0)
    def _():
        for g in range(N_GROUPS):
            off_ref[0, g] = g * cap

    @pl.when(tile > 0)
    def _():
        for g in range(N_GROUPS):
            off_ref[tile % 2, g] = off_ref[(tile + 1) % 2, g] + lens_ref[(tile - 1) * N_GROUPS + g]


def _run_copies(lens_ref, off_ref, tile, make_copy):
    out = []
    local = 0
    for g in range(N_GROUPS):
        length = lens_ref[tile * N_GROUPS + g]
        base = off_ref[tile % 2, g]
        for bit in RUN_BITS:
            rows = RUN_ALIGN << bit
            done = length & (-2 * rows)
            cond = (length & rows) != 0
            out.append((cond, make_copy(pl.multiple_of(local + done, RUN_ALIGN),
                                        pl.multiple_of(base + done, RUN_ALIGN), rows)))
        local = local + length
    return out


def _dispatch_kernel(cap, lens_ref, hn_ref, meta_ref, pos_ref, xs_ref, ms_ref,
                     comp_ref, compm_ref, zx_ref, zm_ref, off_ref, sem):
    i = pl.program_id(0)
    last = pl.num_programs(0) - 1
    tm = hn_ref.shape[0]
    _advance_offsets(lens_ref, off_ref, i, cap)

    def copies_of(tile):
        buf = tile % 2

        def make_copy(local, sorted_row, rows):
            return (pltpu.make_async_copy(comp_ref.at[buf, pl.ds(local, rows)],
                                          xs_ref.at[pl.ds(sorted_row, rows)], sem.at[0, buf]),
                    pltpu.make_async_copy(compm_ref.at[buf, pl.ds(local, rows)],
                                          ms_ref.at[pl.ds(sorted_row, rows)], sem.at[1, buf]))

        return _run_copies(lens_ref, off_ref, tile, make_copy)

    def wait_copies(tile):
        for cond, (cx, cm) in copies_of(tile):
            @pl.when(cond)
            def _(cx=cx, cm=cm):
                cx.wait()
                cm.wait()

    slot = lax.broadcasted_iota(jnp.int32, (MOE_TP, tm), 0).astype(F32)
    perm = jnp.where(slot == pos_ref[0][0:1, :], 1.0, 0.0).astype(BF16)
    comp_ref[i % 2] = _dot(perm, hn_ref[...]).astype(BF16)
    m = meta_ref[...]
    m1 = m.astype(BF16)
    r1 = m - m1.astype(F32)
    m2 = r1.astype(BF16)
    m3 = (r1 - m2.astype(F32)).astype(BF16)
    compm_ref[i % 2] = _dot(perm, m1) + _dot(perm, m2) + _dot(perm, m3)

    for cond, (cx, cm) in copies_of(i):
        @pl.when(cond)
        def _(cx=cx, cm=cm):
            cx.start()
            cm.start()

    @pl.when(i > 0)
    def _():
        wait_copies(i - 1)

    @pl.when(i == last)
    def _():
        wait_copies(i)
        zx_ref[...] = jnp.zeros_like(zx_ref)
        zm_ref[...] = jnp.zeros_like(zm_ref)
        fills = []
        for g in range(N_GROUPS):
            row0 = pl.multiple_of(off_ref[i % 2, g] + lens_ref[i * N_GROUPS + g], RUN_ALIGN)
            fills.append(pltpu.make_async_copy(zx_ref, xs_ref.at[pl.ds(row0, MOE_T)], sem.at[0, 0]))
            fills.append(pltpu.make_async_copy(zm_ref, ms_ref.at[pl.ds(row0, MOE_T)], sem.at[1, 0]))
        for c in fills:
            c.start()
        for c in fills:
            c.wait()


def _dispatch(hn, meta, pos, lens, cap):
    n = hn.shape[0]
    nt = n // MOE_TM
    return pl.pallas_call(
        functools.partial(_dispatch_kernel, cap),
        grid_spec=pltpu.PrefetchScalarGridSpec(
            num_scalar_prefetch=1,
            grid=(nt,),
            in_specs=[pl.BlockSpec((MOE_TM, D), lambda i, *_: (i, 0)),
                      pl.BlockSpec((MOE_TM, LANES), lambda i, *_: (i, 0)),
                      pl.BlockSpec((1, 8, MOE_TM), lambda i, *_: (i, 0, 0))],
            out_specs=[pl.BlockSpec(memory_space=pl.ANY), pl.BlockSpec(memory_space=pl.ANY)],
            scratch_shapes=[pltpu.VMEM((2, MOE_TP, D), BF16), pltpu.VMEM((2, MOE_TP, LANES), F32),
                            pltpu.VMEM((MOE_T, D), BF16), pltpu.VMEM((MOE_T, LANES), F32),
                            pltpu.SMEM((2, N_GROUPS), jnp.int32), pltpu.SemaphoreType.DMA((2, 2))]),
        out_shape=[jax.ShapeDtypeStruct((N_GROUPS * cap, D), BF16),
                   jax.ShapeDtypeStruct((N_GROUPS * cap, LANES), F32)],
        compiler_params=_cparams(("arbitrary",)),
        name="dispatch",
    )(lens, hn, meta, pos)


def _tile_group(j, tend_ref):
    jc = jnp.minimum(j, jnp.maximum(tend_ref[N_GROUPS - 1] - 1, 0))
    g = sum((jc >= tend_ref[t]).astype(jnp.int32) for t in range(N_GROUPS - 1))
    return jc, g


def _experts_kernel(tend_ref, tstart_ref, xs_ref, ms_ref, wg_ref, wu_ref, wd_ref, ys_ref):
    j = pl.program_id(0)

    @pl.when(j < tend_ref[N_GROUPS - 1])
    def _():
        xs = xs_ref[...]
        ms = ms_ref[...]
        lane = lax.broadcasted_iota(jnp.int32, ms.shape, 1)
        first = N_GROUPS + _tile_group(j, tend_ref)[1] * EPG
        acc = None
        for e in range(EPG):
            gate = _dot(xs, wg_ref[e])
            he = (gate * _sigmoid(gate) * _dot(xs, wu_ref[e])).astype(BF16)
            cw = jnp.sum(jnp.where(lane == first + e, ms, 0.0), axis=-1, keepdims=True)
            term = cw * _dot(he, wd_ref[e])
            acc = term if acc is None else acc + term
        ys_ref[...] = acc.astype(BF16)


def _experts(xs, ms, tend, tstart, wts, l, n_tiles, cap):
    def wmap(j, tend, tstart):
        return (l * N_GROUPS + _tile_group(j, tend)[1], 0, 0)

    def rmap(j, tend, tstart):
        jc, g = _tile_group(j, tend)
        return (g * (cap // MOE_T) + jc - tstart[g], 0)

    return pl.pallas_call(
        _experts_kernel,
        grid_spec=pltpu.PrefetchScalarGridSpec(
            num_scalar_prefetch=2,
            grid=(n_tiles,),
            in_specs=[pl.BlockSpec((MOE_T, D), rmap), pl.BlockSpec((MOE_T, LANES), rmap),
                      pl.BlockSpec((EPG, D, D_EXPERT), wmap), pl.BlockSpec((EPG, D, D_EXPERT), wmap),
                      pl.BlockSpec((EPG, D_EXPERT, D), wmap)],
            out_specs=pl.BlockSpec((MOE_T, D), rmap)),
        out_shape=jax.ShapeDtypeStruct(xs.shape, BF16),
        compiler_params=_cparams(("arbitrary",)),
        name="experts",
    )(tend, tstart, xs, ms, wts["w_e_gate"], wts["w_e_up"], wts["w_e_down"])


def _combine_kernel(final_norm, cap, lens_ref, x_ref, meta_ref, ys_ref, gf_ref, o_ref, comp_ref,
                    off_ref, sem):
    i = pl.program_id(0)
    tm = x_ref.shape[0]

    def copies_of(tile):
        buf = tile % 2

        def make_copy(local, sorted_row, rows):
            return pltpu.make_async_copy(ys_ref.at[pl.ds(sorted_row, rows)],
                                         comp_ref.at[buf, pl.ds(local, rows)], sem.at[buf])

        return _run_copies(lens_ref, off_ref, tile, make_copy)

    def start_copies(tile):
        _advance_offsets(lens_ref, off_ref, tile, cap)
        for cond, c in copies_of(tile):
            pl.when(cond)(c.start)

    @pl.when(i == 0)
    def _():
        comp_ref[...] = jnp.zeros_like(comp_ref)
        start_copies(i)

    @pl.when(i < pl.num_programs(0) - 1)
    def _():
        start_copies(i + 1)

    for cond, c in copies_of(i):
        pl.when(cond)(c.wait)

    lane = lax.broadcasted_iota(jnp.int32, meta_ref.shape, 1)
    pos = jnp.sum(jnp.where(lane == POS_ROW, meta_ref[...], 0.0), axis=-1, keepdims=True)
    slot = lax.broadcasted_iota(jnp.int32, (tm, MOE_TP), 1).astype(F32)
    unperm = jnp.where(slot == pos, 1.0, 0.0).astype(BF16)
    out = x_ref[...] + _dot(unperm, comp_ref[i % 2])
    if final_norm:
        out = _rms(out, gf_ref[...])
    o_ref[...] = out


def _combine(x2, meta, ys, lens, norm_f, final_norm, cap):
    n = x2.shape[0]
    return pl.pallas_call(
        functools.partial(_combine_kernel, final_norm, cap),
        grid_spec=pltpu.PrefetchScalarGridSpec(
            num_scalar_prefetch=1,
            grid=(n // MOE_TM,),
            in_specs=[pl.BlockSpec((MOE_TM, D), lambda i, *_: (i, 0)),
                      pl.BlockSpec((MOE_TM, LANES), lambda i, *_: (i, 0)),
                      pl.BlockSpec(memory_space=pl.ANY),
                      pl.BlockSpec((1, D), lambda i, *_: (0, 0))],
            out_specs=pl.BlockSpec((MOE_TM, D), lambda i, *_: (i, 0)),
            scratch_shapes=[pltpu.VMEM((2, MOE_TP, D), BF16), pltpu.SMEM((2, N_GROUPS), jnp.int32),
                            pltpu.SemaphoreType.DMA((2,))]),
        out_shape=jax.ShapeDtypeStruct((n, D), F32),
        compiler_params=_cparams(("arbitrary",)),
        name="combine",
    )(lens, x2, meta, ys, norm_f)


def _moe(x2, wts, l, norm_f, final_norm):
    n = x2.shape[0]
    nt = n // MOE_TM
    cap = -(-(n + RUN_ALIGN * nt + MOE_T) // MOE_T) * MOE_T
    n_tiles = (n + N_GROUPS * RUN_ALIGN * nt) // MOE_T + N_GROUPS + 1
    hn, meta, pos, cnt = _router(x2, wts, l)

    lens = cnt[:, :N_GROUPS, 0]
    tiles = (jnp.sum(lens, axis=0) + (MOE_T - 1)) // MOE_T
    tend = jnp.cumsum(tiles).astype(jnp.int32)
    tstart = tend - tiles
    lens = lens.reshape(-1)

    xs, ms = _dispatch(hn, meta, pos, lens, cap)
    ys = _experts(xs, ms, tend, tstart, wts, l, n_tiles, cap)
    return _combine(x2, meta, ys, lens, norm_f, final_norm, cap)


def _prepare_weights(w):
    depth = w["w_in"].shape[0]
    rw = 3 * D
    n_lo = LORA_DECAY + LORA_AAA + LORA_GATE
    w_in = w["w_in"]
    vres_in = jnp.pad(w["w_vres_in"], ((1, 0), (0, 0), (0, 0)))
    lora = jnp.concatenate([w_in[:, :, rw:rw + n_lo], vres_in], axis=2)
    lora = jnp.pad(lora, ((0, 0), (0, 0), (0, LORA_W - lora.shape[2]))).astype(BF16)
    mu_lora = jnp.concatenate([w["mu_in"][:, rw:rw + n_lo], jnp.pad(w["mu_vres"], ((1, 0), (0, 0)))], axis=1)
    mu_lora = jnp.pad(mu_lora, ((0, 0), (0, LORA_W - mu_lora.shape[1])))
    c0 = rw + n_lo
    starts = [0, D, 2 * D, c0, c0 + 2 * D, c0 + D, c0 + 3 * D, c0 + 4 * D]
    w_main = jnp.stack([w_in[:, :, s:s + D] for s in starts]).astype(BF16)

    def pad_rows(a, before, total):
        return jnp.pad(a, ((0, 0), (before, total - before - a.shape[1]), (0, 0))).astype(BF16)

    router = jnp.swapaxes(jnp.concatenate([w["router_g"], w["router_e"]], axis=2), 1, 2)
    router = jnp.pad(router, ((0, 0), (0, LANES - router.shape[1]), (0, 0)))
    r_hi = router.astype(BF16)
    r_lo = (router - r_hi.astype(F32)).astype(BF16)
    router_b = jnp.concatenate([w["router_g_b"], w["router_e_b"]], axis=1)
    router_b = jnp.pad(router_b, ((0, 0), (0, LANES - router_b.shape[1])))[:, :, None]

    row = lambda a: a.reshape(a.shape[0], 1, -1)
    experts = lambda a: a.astype(BF16).reshape(depth * N_EXPERTS, a.shape[2], a.shape[3])
    return {
        "norm_mix": row(w["norm_mix"]), "w_main": w_main, "w_lora": lora,
        "mu_rkv": row(w["mu_in"][:, :rw]), "mu_lora": row(mu_lora),
        "wup_dec": pad_rows(w["w_decay_up"], 0, LANES),
        "wup_aaa": pad_rows(w["w_aaa_up"], LORA_DECAY, LANES),
        "wup_gate": pad_rows(w["w_gate_up"], 0, 2 * LANES),
        "wup_vres": pad_rows(w["w_vres_up"], LORA_GATE, 2 * LANES), "v0": row(w["v0"]),
        "w0": row(w["w0"]), "a0": row(w["a0"]), "k_k": row(w["k_k"]), "k_a": row(w["k_a"]),
        "conv_w": w["conv_w"],
        "r_k": row(w["r_k"]), "lnx_g": row(w["lnx_g"]), "lnx_b": row(w["lnx_b"]),
        "gate_b": row(w["gate_b"]),
        "w_br_a": w["w_br_a"].astype(BF16), "w_br_b": w["w_br_b"].astype(BF16),
        "w_mix_out": w["w_mix_out"].astype(BF16),
        "norm_memq": row(w["norm_memq"]), "norm_memkv": row(w["norm_memkv"]),
        "wq_mem": w["wq_mem"].astype(BF16), "wkv_mem": w["wkv_mem"].astype(BF16),
        "wo_mem": w["wo_mem"].astype(BF16),
        "norm_ffn": row(w["norm_ffn"]),
        "router_wt": jnp.stack([r_hi, r_lo], axis=1), "router_bt": router_b,
        "w_e_gate": experts(w["w_e_gate"]), "w_e_up": experts(w["w_e_up"]),
        "w_e_down": experts(w["w_e_down"]),
    }


def _layer_block(a, l):
    index = (l,) + (0,) * (a.ndim - 1)
    return pl.BlockSpec((None,) + a.shape[1:], lambda *_: index, pipeline_mode=pl.Buffered(1))


def _tile(total, want):
    t = min(want, total)
    assert total % t == 0, (total, t)
    return t


def kernel(x, mem, norm_mix, w_in, mu_in, w_vres_in, mu_vres, w0, w_decay_up, a0, w_aaa_up, w_gate_up,
           v0, w_vres_up, k_k, k_a, r_k, lnx_g, lnx_b, conv_w, gate_b, w_br_a, w_br_b, w_mix_out,
           norm_memq, norm_memkv, wq_mem, wkv_mem, wo_mem, norm_ffn, router_g, router_g_b, router_e,
           router_e_b, w_e_gate, w_e_up, w_e_down, norm_f):
    w = dict(norm_mix=norm_mix, w_in=w_in, mu_in=mu_in, w_vres_in=w_vres_in, mu_vres=mu_vres, w0=w0,
             w_decay_up=w_decay_up, a0=a0, w_aaa_up=w_aaa_up, w_gate_up=w_gate_up, v0=v0,
             w_vres_up=w_vres_up, k_k=k_k, k_a=k_a, r_k=r_k, lnx_g=lnx_g, lnx_b=lnx_b, conv_w=conv_w,
             gate_b=gate_b, w_br_a=w_br_a, w_br_b=w_br_b, w_mix_out=w_mix_out, norm_memq=norm_memq,
             norm_memkv=norm_memkv, wq_mem=wq_mem, wkv_mem=wkv_mem, wo_mem=wo_mem, norm_ffn=norm_ffn,
             router_g=router_g, router_g_b=router_g_b, router_e=router_e, router_e_b=router_e_b,
             w_e_gate=w_e_gate, w_e_up=w_e_up, w_e_down=w_e_down)
    bsz, seq, _ = x.shape
    depth = norm_mix.shape[0]
    n = bsz * seq
    assert seq % CHUNK == 0 and mem.shape[1] == MEM_LEN
    x2 = x.reshape(n, D)
    mem2 = mem.reshape(bsz * MEM_LEN, D)
    tm_in = _tile(seq, 512)
    tc_wkv = _tile(seq, CHUNK * WKV_UNROLL * WKV_WAVES)
    tm_attn = _tile(seq, 512)
    assert n % MOE_TM == 0
    norm_f2 = norm_f.reshape(1, D)

    wts = _prepare_weights(w)
    v_first = None
    for l in range(depth):
        r, k, v, kk, aa, lwd, og, yb, ga, gb = _mixin(x2, v_first, wts, l, tm_in, seq)
        if l == 0:
            v_first = v
        ya = _wkv(r, k, v, kk, aa, lwd, wts, l, bsz, seq, tc_wkv)
        kv = _memkv(mem2, wts, l)
        x2 = _attn(x2, ya, og, yb, ga, gb, kv, wts, l, tm_attn, seq)
        x2 = _moe(x2, wts, l, norm_f2, l == depth - 1)
    return x2.reshape(bsz, seq, D)
```

```python
import functools
import math

import jax
import jax.numpy as jnp
from jax import lax
from jax.experimental import pallas as pl
from jax.experimental.pallas import tpu as pltpu

F32 = jnp.float32
BF16 = jnp.bfloat16

D = 1024
HEAD = 64
N_HEADS = 16
LANES = 128
N_PAIRS = D // LANES
CHUNK = 64
WKV_UNROLL = 2
WKV_WAVES = 4
LORA_DECAY, LORA_AAA, LORA_GATE, LORA_VRES = 64, 64, 160, 32
LORA_W = 512
CONV_K = 3
MEM_LEN = 256
MEM_HEADS = 4
MEM_HD = D // MEM_HEADS
N_GROUPS, EPG, N_EXPERTS, D_EXPERT = 4, 4, 16, 512
RMS_EPS, LNX_EPS, L2_EPS = 1e-6, 64e-5, 1e-12
VMEM_LIMIT = 56 * 1024 * 1024


def _cparams(sem):
    return pltpu.CompilerParams(dimension_semantics=sem, vmem_limit_bytes=VMEM_LIMIT)


def _rms(x, g):
    return x * lax.rsqrt(jnp.mean(x * x, axis=-1, keepdims=True) + RMS_EPS) * g


def _sigmoid(x):
    return 0.5 * jnp.tanh(0.5 * x) + 0.5


def _dot(a, b):
    return jnp.dot(a, b, preferred_element_type=F32)


def _dot_nt(a, b):
    return lax.dot_general(a, b, (((1,), (1,)), ((), ())), preferred_element_type=F32)


def _split_bf16(x):
    hi = x.astype(BF16)
    lo = (x - hi.astype(F32)).astype(BF16)
    return hi, lo


BLOCKS = (BLK_R, BLK_K, BLK_V, BLK_CX, BLK_CC, BLK_CB, BLK_GA, BLK_GB) = tuple(range(8))


def _mixin_kernel(has_vres, tiles_per_seq, *refs):
    n_in = 17 if has_vres else 14
    ins, outs, scr = refs[:n_in], refs[n_in:n_in + 10], refs[n_in + 10:]
    (x_ref, g_ref, wm_ref, wl_ref, mu_ref, mul_ref, wdec_ref, waaa_ref, wgate_ref, w0_ref, a0_ref,
     kk_ref, ka_ref, cw_ref) = ins[:14]
    ro_ref, ko_ref, vo_ref, kko_ref, aao_ref, lwo_ref, ogo_ref, ybo_ref, gao_ref, gbo_ref = outs
    clo_ref, crkv_ref, cu_ref = scr
    if has_vres:
        vf_ref, wvres_ref, v0_ref = ins[14:]
    tm = x_ref.shape[0]

    @pl.when(pl.program_id(0) % tiles_per_seq == 0)
    def _():
        crkv_ref[...] = jnp.zeros_like(crkv_ref)
        clo_ref[...] = jnp.zeros_like(clo_ref)
        cu_ref[...] = jnp.zeros_like(cu_ref)

    xn = _rms(x_ref[...], g_ref[...]).astype(BF16)

    def mixed(z, carry, mu):
        return z + (_shift_rows(z, carry, 1) - z) * mu

    def block(b):
        return _dot(xn, wm_ref[b])

    def shifted_block(idx):
        z = block(idx)
        cols = slice(idx * D, (idx + 1) * D)
        zm = mixed(z, crkv_ref[:, cols], mu_ref[:, cols])
        crkv_ref[:, cols] = z[tm - 8:, :]
        return zm

    lo = _dot(xn, wl_ref[...])
    lom = mixed(lo, clo_ref[...], mul_ref[...])
    clo_ref[...] = lo[tm - 8:, :]
    lane = lax.broadcasted_iota(jnp.int32, (tm, LANES), 1)
    t0, t1, t2 = (lom[:, t * LANES:(t + 1) * LANES] for t in range(3))
    gate_tail = LORA_DECAY + LORA_AAA + LORA_GATE - 2 * LANES
    act_a = jnp.where(lane < LORA_DECAY, jnp.tanh(t0), t0).astype(BF16)
    act_b = jnp.concatenate([_sigmoid(t1), jnp.where(lane < gate_tail, _sigmoid(t2), t2)],
                            axis=1).astype(BF16)

    u = w0_ref[...] + _dot(act_a, wdec_ref[...])
    lwo_ref[...] = -math.exp(-0.5) * _sigmoid(u)
    ro_ref[...] = shifted_block(BLK_R).astype(BF16)

    aa = _sigmoid(a0_ref[...] + _dot(act_a, waaa_ref[...]))
    aao_ref[...] = aa.astype(BF16)
    zm = shifted_block(BLK_K)
    kko_ref[...] = (zm * kk_ref[...]).astype(BF16)
    ko_ref[...] = (zm * (1.0 + (aa - 1.0) * ka_ref[...])).astype(BF16)

    if has_vres:
        vgate = _sigmoid(v0_ref[...] + _dot(act_b, wvres_ref[...]))
    zm = shifted_block(BLK_V)
    if has_vres:
        zm = zm + (vf_ref[...].astype(F32) - zm) * vgate
    vo_ref[...] = zm.astype(BF16)

    ogo_ref[...] = _dot(act_b, wgate_ref[...]).astype(BF16)

    u = block(BLK_CC) * block(BLK_CX)
    cu = cu_ref[...]
    conv = cw_ref[CONV_K - 1:CONV_K, :] * u
    for t in range(CONV_K - 1):
        conv = conv + cw_ref[t:t + 1, :] * _shift_rows(u, cu, CONV_K - 1 - t)
    cu_ref[...] = u[tm - 8:, :]
    ybo_ref[...] = (block(BLK_CB) * conv).astype(BF16)

    gao_ref[...] = block(BLK_GA).astype(BF16)
    gbo_ref[...] = block(BLK_GB).astype(BF16)


def _mixin(x2, vfirst, wts, l, tm, seq):
    n = x2.shape[0]
    has_vres = vfirst is not None
    tile = pl.BlockSpec((tm, D), lambda i: (i, 0))
    names = ["norm_mix", "w_main", "w_lora", "mu_rkv", "mu_lora", "wup_dec", "wup_aaa", "wup_gate", "w0",
             "a0", "k_k", "k_a", "conv_w"]
    in_specs = [tile] + [_layer_block(wts[name], l) for name in names]
    in_specs[1 + names.index("w_main")] = pl.BlockSpec(
        (len(BLOCKS), None, D, D), lambda i: (0, l, 0, 0), pipeline_mode=pl.Buffered(1))
    args = [x2] + [wts[name] for name in names]
    if has_vres:
        in_specs += [tile, _layer_block(wts["wup_vres"], l - 1), _layer_block(wts["v0"], l - 1)]
        args += [vfirst, wts["wup_vres"], wts["v0"]]
    bf = jax.ShapeDtypeStruct((n, D), BF16)
    out_shape = [bf, bf, bf, bf, bf, jax.ShapeDtypeStruct((n, D), F32), bf, bf, bf, bf]
    return pl.pallas_call(
        functools.partial(_mixin_kernel, has_vres, seq // tm),
        grid=(n // tm,),
        in_specs=in_specs,
        out_specs=[tile] * 10,
        out_shape=out_shape,
        scratch_shapes=[pltpu.VMEM((8, LORA_W), F32), pltpu.VMEM((8, 3 * D), F32), pltpu.VMEM((8, D), F32)],
        compiler_params=_cparams(("arbitrary",)),
        name="mixin",
    )(*args)


def _shift_rows(z, prev_rows, k):
    rolled = pltpu.roll(z, k, 0)
    rows = lax.broadcasted_iota(jnp.int32, z.shape, 0)
    out = rolled
    for j in range(k):
        src = prev_rows[8 - k + j:8 - k + j + 1, :]
        out = jnp.where(rows == j, src, out)
    return out


def _pair_expand(x, m0, m1):
    return jnp.concatenate([jnp.where(m0, x, 0.0), jnp.where(m1, x, 0.0)], axis=0)


def _halves_sum(x, m0):
    s0 = jnp.sum(jnp.where(m0, x, 0.0), axis=-1, keepdims=True)
    s1 = jnp.sum(jnp.where(m0, 0.0, x), axis=-1, keepdims=True)
    return jnp.where(m0, s0, s1)


def _wkv_kernel(r_ref, k_ref, v_ref, kk_ref, aa_ref, lw_ref, rk_ref, g_ref, b_ref, o_ref, s_ref):
    @pl.when(pl.program_id(1) == 0)
    def _():
        s_ref[...] = jnp.zeros_like(s_ref)

    c2 = 2 * CHUNK
    lane = lax.broadcasted_iota(jnp.int32, (CHUNK, LANES), 1)
    m0 = lane < HEAD
    m1 = jnp.logical_not(m0)
    rows = lax.broadcasted_iota(jnp.int32, (c2, c2), 0)
    cols = lax.broadcasted_iota(jnp.int32, (c2, c2), 1)
    strict = cols < rows
    incl = cols <= rows
    eye = cols == rows
    tri = (lax.broadcasted_iota(jnp.int32, (CHUNK, CHUNK), 1)
           <= lax.broadcasted_iota(jnp.int32, (CHUNK, CHUNK), 0)).astype(BF16)
    n_chunks = r_ref.shape[0] // CHUNK

    def wave(first_chunk):
        t0s = [pl.multiple_of((first_chunk + u) * CHUNK, CHUNK) for u in range(WKV_UNROLL)]
        probs = [(t0, slice(pi * LANES, (pi + 1) * LANES)) for t0 in t0s for pi in range(N_PAIRS)]
        cat0 = lambda *xs: jnp.concatenate(xs, axis=0)
        cat1 = lambda *xs: jnp.concatenate(xs, axis=1)
        expand = lambda x: _pair_expand(x, m0, m1)
        ld = lambda ref, p: ref[pl.ds(p[0], CHUNK), p[1]].astype(F32)

        lw_all = [lw_ref[pl.ds(t0, CHUNK), :] for t0 in t0s]
        cum_all = []
        for lw_u in lw_all:
            lw_hi, lw_lo = _split_bf16(lw_u)
            cum_all.append(_dot(tri, lw_hi) + _dot(tri, lw_lo))
        lw = [lw_u[:, pi * LANES:(pi + 1) * LANES] for lw_u in lw_all for pi in range(N_PAIRS)]
        cum = [c_u[:, pi * LANES:(pi + 1) * LANES] for c_u in cum_all for pi in range(N_PAIRS)]
        cum_end = [c[CHUNK - 1:CHUNK, :] for c in cum]
        w_inv = [jnp.exp(-c) for c in cum]
        w_end = [jnp.exp(ce - c) for c, ce in zip(cum, cum_end)]
        r = [ld(r_ref, p) for p in probs]
        k = [ld(k_ref, p) for p in probs]
        kk = [ld(kk_ref, p) for p in probs]
        kk = [x / jnp.maximum(jnp.sqrt(_halves_sum(x * x, m0)), L2_EPS) for x in kk]
        b = [x * ld(aa_ref, p) for x, p in zip(kk, probs)]
        a_t = [expand(-x * jnp.exp(c - l)) for x, c, l in zip(kk, cum, lw)]
        r_t = [expand(x * jnp.exp(c)) for x, c in zip(r, cum)]
        lhs1 = [cat0(a, rr).astype(BF16) for a, rr in zip(a_t, r_t)]
        rhs1 = [cat0(expand(bb * wi), expand(kx * wi)).astype(BF16) for bb, kx, wi in zip(b, k, w_inv)]
        aa_m = [_dot_nt(x, y) for x, y in zip(lhs1, rhs1)]
        a_ab = [jnp.where(strict, m[:c2, :c2], 0.0) for m in aa_m]
        a_ak = [jnp.where(strict, m[:c2, c2:], 0.0).astype(BF16) for m in aa_m]
        lhs_top = [cat1(jnp.where(incl, m[c2:, :c2], 0.0), jnp.where(incl, m[c2:, c2:], 0.0)).astype(BF16)
                   for m in aa_m]

        t_m = [jnp.where(eye, 1.0, 0.0) + m for m in a_ab]
        a_b = [m.astype(BF16) for m in a_ab]
        a_m = [_dot(x, x) for x in a_b]
        n_steps = CHUNK.bit_length() - 1
        for step in range(1, n_steps):
            a_b = [m.astype(BF16) for m in a_m]
            if step < n_steps - 1:
                both = [_dot(x, cat1(x, t.astype(BF16))) for x, t in zip(a_b, t_m)]
                a_m = [m[:, :c2] for m in both]
                t_m = [t + m[:, c2:] for t, m in zip(t_m, both)]
            else:
                t_m = [t + _dot(x, t.astype(BF16)) for x, t in zip(a_b, t_m)]

        v = [ld(v_ref, p) for p in probs]
        v_b = [expand(x).astype(BF16) for x in v]
        akv = [_dot(x, y) for x, y in zip(a_ak, v_b)]
        tx = [_dot(t.astype(BF16), cat1(a, y).astype(BF16)) for t, a, y in zip(t_m, a_t, akv)]
        rhs2 = [cat0(x.astype(BF16), cat1(jnp.zeros((c2, c2), BF16), y)) for x, y in zip(tx, v_b)]
        lhs_bot = [cat1(expand(bb * we).T, expand(kx * we).T).astype(BF16)
                   for bb, kx, we in zip(b, k, w_end)]
        z = [_dot(cat0(x, y), w) for x, y, w in zip(lhs_top, lhs_bot, rhs2)]
        gp = [cat0(rt + m[:c2, :c2], m[c2:, :c2] + jnp.where(eye, jnp.exp(ce), 0.0)).astype(BF16)
              for rt, m, ce in zip(r_t, z, cum_end)]

        state = [s_ref[pi] for pi in range(N_PAIRS)]
        y = []
        for u in range(WKV_UNROLL):
            base = u * N_PAIRS
            zz = [_dot(gp[base + pi], state[pi].astype(BF16)) for pi in range(N_PAIRS)]
            state = [zz[pi][c2:] + z[base + pi][c2:, c2:] for pi in range(N_PAIRS)]
            y += [(zz[pi][:CHUNK] + zz[pi][CHUNK:c2])
                  + (z[base + pi][:CHUNK, c2:] + z[base + pi][CHUNK:c2, c2:]) for pi in range(N_PAIRS)]
        for pi in range(N_PAIRS):
            s_ref[pi] = state[pi]

        for i, (t0, sl) in enumerate(probs):
            mu = _halves_sum(y[i], m0) * (1.0 / HEAD)
            yc = y[i] - mu
            var = _halves_sum(yc * yc, m0) * (1.0 / HEAD)
            yn = yc * lax.rsqrt(var + LNX_EPS) * g_ref[:, sl] + b_ref[:, sl]
            bonus = _halves_sum(r[i] * k[i] * rk_ref[:, sl], m0) * v[i]
            o_ref[pl.ds(t0, CHUNK), sl] = (yn + bonus).astype(BF16)

    per_trip = WKV_UNROLL * WKV_WAVES
    assert n_chunks % per_trip == 0

    def trip(ti, carry):
        for w in range(WKV_WAVES):
            wave(ti * per_trip + w * WKV_UNROLL)
        return carry

    lax.fori_loop(0, n_chunks // per_trip, trip, 0)


def _wkv(r, k, v, kk, aa, lw, wts, l, bsz, seq, tc):
    n = r.shape[0]
    nt = seq // tc
    blk = pl.BlockSpec((tc, D), lambda b, t: (b * nt + t, 0))
    rows = [wts["r_k"], wts["lnx_g"], wts["lnx_b"]]
    return pl.pallas_call(
        _wkv_kernel,
        grid=(bsz, nt),
        in_specs=[blk] * 6 + [_layer_block(a, l) for a in rows],
        out_specs=blk,
        out_shape=jax.ShapeDtypeStruct((n, D), BF16),
        scratch_shapes=[pltpu.VMEM((N_PAIRS, LANES, LANES), F32)],
        compiler_params=_cparams(("parallel", "arbitrary")),
        name="wkv",
    )(r, k, v, kk, aa, lw, *rows)


def _attn_kernel(tiles_per_seq, x_ref, ya_ref, og_ref, yb_ref, ga_ref, gb_ref, mem_ref, gbias_ref, wa_ref,
                 wb_ref, wmix_ref, gq_ref, wq_ref, wo_ref, gkv_ref, wkv_ref, gffn_ref, rw_ref, rb_ref,
                 o_ref, hn_ref, meta_ref, pos_ref, cnt_ref, kv_ref):
    @pl.when(pl.program_id(0) % tiles_per_seq == 0)
    def _():
        mn = _rms(mem_ref[...], gkv_ref[...]).astype(BF16)
        kv_ref[...] = _dot(mn, wkv_ref[...]).astype(BF16)

    y_a = (ya_ref[...].astype(F32) * og_ref[...].astype(F32)).astype(BF16)
    g_a = _sigmoid(ga_ref[...].astype(F32) + gbias_ref[:, :D])
    g_b = _sigmoid(gb_ref[...].astype(F32) + gbias_ref[:, D:])
    merged = g_a * _dot(y_a, wa_ref[...]) + g_b * _dot(yb_ref[...], wb_ref[...])
    x1 = x_ref[...] + _dot(merged.astype(BF16), wmix_ref[...])

    hn = _rms(x1, gq_ref[...]).astype(BF16)
    q = _dot(hn, wq_ref[...]).astype(BF16)
    heads = []
    for h in range(MEM_HEADS):
        sl = slice(h * MEM_HD, (h + 1) * MEM_HD)
        s = _dot_nt(q[:, sl], kv_ref[:, sl]) * (MEM_HD ** -0.5)
        s = s - jnp.max(s, axis=-1, keepdims=True)
        e = jnp.exp(s)
        pr = (e / jnp.sum(e, axis=-1, keepdims=True)).astype(BF16)
        heads.append(_dot(pr, kv_ref[:, D + h * MEM_HD:D + (h + 1) * MEM_HD]).astype(BF16))
    o = jnp.concatenate(heads, axis=1)
    x2 = x1 + _dot(o, wo_ref[...])
    o_ref[...] = x2
    _route_tile(x2, gffn_ref, rw_ref, rb_ref, hn_ref, meta_ref, pos_ref, cnt_ref)


def _attn(x2, ya, og, yb, ga, gb, mem2, wts, l, seq):
    n = x2.shape[0]
    tm = MOE_TM
    nt = n // tm
    tps = seq // tm
    tile = pl.BlockSpec((tm, D), lambda i: (i, 0))
    params = [wts[name] for name in ("gate_b", "w_br_a", "w_br_b", "w_mix_out", "norm_memq", "wq_mem",
                                     "wo_mem", "norm_memkv", "wkv_mem", "norm_ffn", "router_wt",
                                     "router_bt")]
    return pl.pallas_call(
        functools.partial(_attn_kernel, tps),
        grid=(nt,),
        in_specs=[tile] * 6 + [pl.BlockSpec((MEM_LEN, D), lambda i: (i // tps, 0))]
        + [_layer_block(a, l) for a in params],
        out_specs=[tile, tile, pl.BlockSpec((tm, LANES), lambda i: (i, 0)),
                   pl.BlockSpec((1, 8, tm), lambda i: (i, 0, 0)),
                   pl.BlockSpec((1, 8, LANES), lambda i: (i, 0, 0))],
        out_shape=[jax.ShapeDtypeStruct((n, D), F32), jax.ShapeDtypeStruct((n, D), BF16),
                   jax.ShapeDtypeStruct((n, LANES), F32), jax.ShapeDtypeStruct((nt, 8, tm), F32),
                   jax.ShapeDtypeStruct((nt, 8, LANES), jnp.int32)],
        scratch_shapes=[pltpu.VMEM((MEM_LEN, 2 * D), BF16)],
        compiler_params=_cparams(("arbitrary",)),
        name="attn",
    )(x2, ya, og, yb, ga, gb, mem2, *params)


MOE_TM = 512
MOE_TP = MOE_TM + LANES
MOE_T = 512
RUN_ALIGN = 16
POS_ROW = N_GROUPS + N_EXPERTS
RUN_BITS = tuple(range((MOE_TM // RUN_ALIGN).bit_length() - 1, -1, -1))


def _route_tile(x, g_ref, rw_ref, rb_ref, hn_ref, meta_ref, pos_ref, cnt_ref):
    tm = x.shape[0]
    hn = _rms(x, g_ref[...])
    hi, lo = _split_bf16(hn)
    hn_ref[...] = hi
    lt = _dot_nt(rw_ref[0], hi) + _dot_nt(rw_ref[0], lo) + _dot_nt(rw_ref[1], hi) + rb_ref[...]
    row = lax.broadcasted_iota(jnp.int32, lt.shape, 0).astype(F32)
    neg = jnp.float32(-jnp.inf)
    big = jnp.float32(1 << 20)
    is_g = row < N_GROUPS
    lg = jnp.where(is_g, lt, neg)
    gmax = jnp.max(lg, axis=0, keepdims=True)
    ge = jnp.where(is_g, jnp.exp(lg - gmax), 0.0)
    gp = ge / jnp.sum(ge, axis=0, keepdims=True)
    g_w = jnp.max(gp, axis=0, keepdims=True)
    g_i = jnp.min(jnp.where(is_g & (gp == g_w), row, big), axis=0, keepdims=True)
    first = N_GROUPS + g_i * EPG
    sel = (row >= first) & (row < first + EPG)
    le = jnp.where(sel, lt, neg)
    emax = jnp.max(le, axis=0, keepdims=True)
    ee = jnp.where(sel, jnp.exp(le - emax), 0.0)
    ep = ee / jnp.sum(ee, axis=0, keepdims=True)
    p1 = jnp.max(ep, axis=0, keepdims=True)
    i1 = jnp.min(jnp.where(sel & (ep == p1), row, big), axis=0, keepdims=True)
    rest = sel & (row != i1)
    p2 = jnp.max(jnp.where(rest, ep, -1.0), axis=0, keepdims=True)
    i2 = jnp.min(jnp.where(rest & (ep == p2), row, big), axis=0, keepdims=True)
    tot = p1 + p2
    cw = jnp.where(row == i1, g_w * (p1 / tot), 0.0) + jnp.where(row == i2, g_w * (p2 / tot), 0.0)

    grow = lax.broadcasted_iota(jnp.int32, (8, tm), 0).astype(F32)
    onehot = jnp.where(grow == g_i, 1.0, 0.0)
    before = (lax.broadcasted_iota(jnp.int32, (tm, tm), 0)
              < lax.broadcasted_iota(jnp.int32, (tm, tm), 1)).astype(BF16)
    rank = _dot(onehot.astype(BF16), before)
    count = jnp.sum(onehot, axis=1, keepdims=True)
    run = jnp.floor((count + (RUN_ALIGN - 1)) * (1.0 / RUN_ALIGN)) * RUN_ALIGN
    gcol = lax.broadcasted_iota(jnp.int32, (8, 1), 0)
    start = jnp.zeros((8, 1), F32)
    acc = jnp.zeros((1, 1), F32)
    for g in range(1, N_GROUPS):
        acc = acc + run[g - 1:g, :]
        start = start + jnp.where(gcol == g, acc, 0.0)
    pos = jnp.sum(onehot * (start + rank), axis=0, keepdims=True)
    meta_ref[...] = (cw + jnp.where(row == POS_ROW, pos, 0.0)).T
    pos_ref[0] = jnp.broadcast_to(pos, (8, tm))
    cnt_ref[0] = jnp.broadcast_to(run, (8, LANES)).astype(jnp.int32)


def _advance_offsets(lens_ref, off_ref, tile, cap):
    @pl.when(tile == 0)
    def _():
        for g in range(N_GROUPS):
            off_ref[0, g] = g * cap

    @pl.when(tile > 0)
    def _():
        for g in range(N_GROUPS):
            off_ref[tile % 2, g] = off_ref[(tile + 1) % 2, g] + lens_ref[(tile - 1) * N_GROUPS + g]


def _run_copies(lens_ref, off_ref, tile, make_copy):
    out = []
    local = 0
    for g in range(N_GROUPS):
        length = lens_ref[tile * N_GROUPS + g]
        base = off_ref[tile % 2, g]
        for bit in RUN_BITS:
            rows = RUN_ALIGN << bit
            done = length & (-2 * rows)
            cond = (length & rows) != 0
            out.append((cond, make_copy(pl.multiple_of(local + done, RUN_ALIGN),
                                        pl.multiple_of(base + done, RUN_ALIGN), rows)))
        local = local + length
    return out


def _dispatch_kernel(cap, lens_ref, hn_ref, meta_ref, pos_ref, xs_ref, ms_ref,
                     comp_ref, compm_ref, zx_ref, zm_ref, off_ref, sem):
    i = pl.program_id(0)
    last = pl.num_programs(0) - 1
    tm = hn_ref.shape[0]
    _advance_offsets(lens_ref, off_ref, i, cap)

    def copies_of(tile):
        buf = tile % 2

        def make_copy(local, sorted_row, rows):
            return (pltpu.make_async_copy(comp_ref.at[buf, pl.ds(local, rows)],
                                          xs_ref.at[pl.ds(sorted_row, rows)], sem.at[0, buf]),
                    pltpu.make_async_copy(compm_ref.at[buf, pl.ds(local, rows)],
                                          ms_ref.at[pl.ds(sorted_row, rows)], sem.at[1, buf]))

        return _run_copies(lens_ref, off_ref, tile, make_copy)

    def wait_copies(tile):
        for cond, (cx, cm) in copies_of(tile):
            @pl.when(cond)
            def _(cx=cx, cm=cm):
                cx.wait()
                cm.wait()

    slot = lax.broadcasted_iota(jnp.int32, (MOE_TP, tm), 0).astype(F32)
    perm = jnp.where(slot == pos_ref[0][0:1, :], 1.0, 0.0).astype(BF16)
    comp_ref[i % 2] = _dot(perm, hn_ref[...]).astype(BF16)
    m = meta_ref[...]
    m1 = m.astype(BF16)
    r1 = m - m1.astype(F32)
    m2 = r1.astype(BF16)
    m3 = (r1 - m2.astype(F32)).astype(BF16)
    compm_ref[i % 2] = _dot(perm, m1) + _dot(perm, m2) + _dot(perm, m3)

    for cond, (cx, cm) in copies_of(i):
        @pl.when(cond)
        def _(cx=cx, cm=cm):
            cx.start()
            cm.start()

    @pl.when(i > 0)
    def _():
        wait_copies(i - 1)

    @pl.when(i == last)
    def _():
        wait_copies(i)
        zx_ref[...] = jnp.zeros_like(zx_ref)
        zm_ref[...] = jnp.zeros_like(zm_ref)
        fills = []
        for g in range(N_GROUPS):
            row0 = pl.multiple_of(off_ref[i % 2, g] + lens_ref[i * N_GROUPS + g], RUN_ALIGN)
            fills.append(pltpu.make_async_copy(zx_ref, xs_ref.at[pl.ds(row0, MOE_T)], sem.at[0, 0]))
            fills.append(pltpu.make_async_copy(zm_ref, ms_ref.at[pl.ds(row0, MOE_T)], sem.at[1, 0]))
        for c in fills:
            c.start()
        for c in fills:
            c.wait()


def _dispatch(hn, meta, pos, lens, cap):
    n = hn.shape[0]
    nt = n // MOE_TM
    return pl.pallas_call(
        functools.partial(_dispatch_kernel, cap),
        grid_spec=pltpu.PrefetchScalarGridSpec(
            num_scalar_prefetch=1,
            grid=(nt,),
            in_specs=[pl.BlockSpec((MOE_TM, D), lambda i, *_: (i, 0)),
                      pl.BlockSpec((MOE_TM, LANES), lambda i, *_: (i, 0)),
                      pl.BlockSpec((1, 8, MOE_TM), lambda i, *_: (i, 0, 0))],
            out_specs=[pl.BlockSpec(memory_space=pl.ANY), pl.BlockSpec(memory_space=pl.ANY)],
            scratch_shapes=[pltpu.VMEM((2, MOE_TP, D), BF16), pltpu.VMEM((2, MOE_TP, LANES), F32),
                            pltpu.VMEM((MOE_T, D), BF16), pltpu.VMEM((MOE_T, LANES), F32),
                            pltpu.SMEM((2, N_GROUPS), jnp.int32), pltpu.SemaphoreType.DMA((2, 2))]),
        out_shape=[jax.ShapeDtypeStruct((N_GROUPS * cap, D), BF16),
                   jax.ShapeDtypeStruct((N_GROUPS * cap, LANES), F32)],
        compiler_params=_cparams(("arbitrary",)),
        name="dispatch",
    )(lens, hn, meta, pos)


def _tile_group(j, tend_ref):
    jc = jnp.minimum(j, jnp.maximum(tend_ref[N_GROUPS - 1] - 1, 0))
    g = sum((jc >= tend_ref[t]).astype(jnp.int32) for t in range(N_GROUPS - 1))
    return jc, g


def _experts_kernel(tend_ref, tstart_ref, xs_ref, ms_ref, wg_ref, wu_ref, wd_ref, ys_ref):
    j = pl.program_id(0)

    @pl.when(j < tend_ref[N_GROUPS - 1])
    def _():
        xs = xs_ref[...]
        ms = ms_ref[...]
        lane = lax.broadcasted_iota(jnp.int32, ms.shape, 1)
        first = N_GROUPS + _tile_group(j, tend_ref)[1] * EPG
        acc = None
        for e in range(EPG):
            gate = _dot(xs, wg_ref[e])
            he = (gate * _sigmoid(gate) * _dot(xs, wu_ref[e])).astype(BF16)
            cw = jnp.sum(jnp.where(lane == first + e, ms, 0.0), axis=-1, keepdims=True)
            term = cw * _dot(he, wd_ref[e])
            acc = term if acc is None else acc + term
        ys_ref[...] = acc.astype(BF16)


def _experts(xs, ms, tend, tstart, wts, l, n_tiles, cap):
    def wmap(j, tend, tstart):
        return (l * N_GROUPS + _tile_group(j, tend)[1], 0, 0)

    def rmap(j, tend, tstart):
        jc, g = _tile_group(j, tend)
        return (g * (cap // MOE_T) + jc - tstart[g], 0)

    return pl.pallas_call(
        _experts_kernel,
        grid_spec=pltpu.PrefetchScalarGridSpec(
            num_scalar_prefetch=2,
            grid=(n_tiles,),
            in_specs=[pl.BlockSpec((MOE_T, D), rmap), pl.BlockSpec((MOE_T, LANES), rmap),
                      pl.BlockSpec((EPG, D, D_EXPERT), wmap), pl.BlockSpec((EPG, D, D_EXPERT), wmap),
                      pl.BlockSpec((EPG, D_EXPERT, D), wmap)],
            out_specs=pl.BlockSpec((MOE_T, D), rmap)),
        out_shape=jax.ShapeDtypeStruct(xs.shape, BF16),
        compiler_params=_cparams(("arbitrary",)),
        name="experts",
    )(tend, tstart, xs, ms, wts["w_e_gate"], wts["w_e_up"], wts["w_e_down"])


def _combine_kernel(final_norm, cap, lens_ref, x_ref, meta_ref, ys_ref, gf_ref, o_ref, comp_ref,
                    off_ref, sem):
    i = pl.program_id(0)
    tm = x_ref.shape[0]

    def copies_of(tile):
        buf = tile % 2

        def make_copy(local, sorted_row, rows):
            return pltpu.make_async_copy(ys_ref.at[pl.ds(sorted_row, rows)],
                                         comp_ref.at[buf, pl.ds(local, rows)], sem.at[buf])

        return _run_copies(lens_ref, off_ref, tile, make_copy)

    def start_copies(tile):
        _advance_offsets(lens_ref, off_ref, tile, cap)
        for cond, c in copies_of(tile):
            pl.when(cond)(c.start)

    @pl.when(i == 0)
    def _():
        comp_ref[...] = jnp.zeros_like(comp_ref)
        start_copies(i)

    @pl.when(i < pl.num_programs(0) - 1)
    def _():
        start_copies(i + 1)

    for cond, c in copies_of(i):
        pl.when(cond)(c.wait)

    lane = lax.broadcasted_iota(jnp.int32, meta_ref.shape, 1)
    pos = jnp.sum(jnp.where(lane == POS_ROW, meta_ref[...], 0.0), axis=-1, keepdims=True)
    slot = lax.broadcasted_iota(jnp.int32, (tm, MOE_TP), 1).astype(F32)
    unperm = jnp.where(slot == pos, 1.0, 0.0).astype(BF16)
    out = x_ref[...] + _dot(unperm, comp_ref[i % 2])
    if final_norm:
        out = _rms(out, gf_ref[...])
    o_ref[...] = out


def _combine(x2, meta, ys, lens, norm_f, final_norm, cap):
    n = x2.shape[0]
    return pl.pallas_call(
        functools.partial(_combine_kernel, final_norm, cap),
        grid_spec=pltpu.PrefetchScalarGridSpec(
            num_scalar_prefetch=1,
            grid=(n // MOE_TM,),
            in_specs=[pl.BlockSpec((MOE_TM, D), lambda i, *_: (i, 0)),
                      pl.BlockSpec((MOE_TM, LANES), lambda i, *_: (i, 0)),
                      pl.BlockSpec(memory_space=pl.ANY),
                      pl.BlockSpec((1, D), lambda i, *_: (0, 0))],
            out_specs=pl.BlockSpec((MOE_TM, D), lambda i, *_: (i, 0)),
            scratch_shapes=[pltpu.VMEM((2, MOE_TP, D), BF16), pltpu.SMEM((2, N_GROUPS), jnp.int32),
                            pltpu.SemaphoreType.DMA((2,))]),
        out_shape=jax.ShapeDtypeStruct((n, D), F32),
        compiler_params=_cparams(("arbitrary",)),
        name="combine",
    )(lens, x2, meta, ys, norm_f)


def _moe(x2, hn, meta, pos, cnt, wts, l, norm_f, final_norm):
    n = x2.shape[0]
    nt = n // MOE_TM
    cap = -(-(n + RUN_ALIGN * nt + MOE_T) // MOE_T) * MOE_T
    n_tiles = (n + N_GROUPS * RUN_ALIGN * nt) // MOE_T + N_GROUPS + 1

    lens = cnt[:, :N_GROUPS, 0]
    tiles = (jnp.sum(lens, axis=0) + (MOE_T - 1)) // MOE_T
    tend = jnp.cumsum(tiles).astype(jnp.int32)
    tstart = tend - tiles
    lens = lens.reshape(-1)

    xs, ms = _dispatch(hn, meta, pos, lens, cap)
    ys = _experts(xs, ms, tend, tstart, wts, l, n_tiles, cap)
    return _combine(x2, meta, ys, lens, norm_f, final_norm, cap)


def _prepare_weights(w):
    depth = w["w_in"].shape[0]
    rw = 3 * D
    n_lo = LORA_DECAY + LORA_AAA + LORA_GATE
    w_in = w["w_in"]
    vres_in = jnp.pad(w["w_vres_in"], ((1, 0), (0, 0), (0, 0)))
    lora = jnp.concatenate([w_in[:, :, rw:rw + n_lo], vres_in], axis=2)
    lora = jnp.pad(lora, ((0, 0), (0, 0), (0, LORA_W - lora.shape[2]))).astype(BF16)
    mu_lora = jnp.concatenate([w["mu_in"][:, rw:rw + n_lo], jnp.pad(w["mu_vres"], ((1, 0), (0, 0)))], axis=1)
    mu_lora = jnp.pad(mu_lora, ((0, 0), (0, LORA_W - mu_lora.shape[1])))
    c0 = rw + n_lo
    starts = [0, D, 2 * D, c0, c0 + 2 * D, c0 + D, c0 + 3 * D, c0 + 4 * D]
    w_main = jnp.stack([w_in[:, :, s:s + D] for s in starts]).astype(BF16)

    def pad_rows(a, before, total):
        return jnp.pad(a, ((0, 0), (before, total - before - a.shape[1]), (0, 0))).astype(BF16)

    router = jnp.swapaxes(jnp.concatenate([w["router_g"], w["router_e"]], axis=2), 1, 2)
    router = jnp.pad(router, ((0, 0), (0, LANES - router.shape[1]), (0, 0)))
    r_hi = router.astype(BF16)
    r_lo = (router - r_hi.astype(F32)).astype(BF16)
    router_b = jnp.concatenate([w["router_g_b"], w["router_e_b"]], axis=1)
    router_b = jnp.pad(router_b, ((0, 0), (0, LANES - router_b.shape[1])))[:, :, None]

    row = lambda a: a.reshape(a.shape[0], 1, -1)
    experts = lambda a: a.astype(BF16).reshape(depth * N_EXPERTS, a.shape[2], a.shape[3])
    return {
        "norm_mix": row(w["norm_mix"]), "w_main": w_main, "w_lora": lora,
        "mu_rkv": row(w["mu_in"][:, :rw]), "mu_lora": row(mu_lora),
        "wup_dec": pad_rows(w["w_decay_up"], 0, LANES),
        "wup_aaa": pad_rows(w["w_aaa_up"], LORA_DECAY, LANES),
        "wup_gate": pad_rows(w["w_gate_up"], 0, 2 * LANES),
        "wup_vres": pad_rows(w["w_vres_up"], LORA_GATE, 2 * LANES), "v0": row(w["v0"]),
        "w0": row(w["w0"]), "a0": row(w["a0"]), "k_k": row(w["k_k"]), "k_a": row(w["k_a"]),
        "conv_w": w["conv_w"],
        "r_k": row(w["r_k"]), "lnx_g": row(w["lnx_g"]), "lnx_b": row(w["lnx_b"]),
        "gate_b": row(w["gate_b"]),
        "w_br_a": w["w_br_a"].astype(BF16), "w_br_b": w["w_br_b"].astype(BF16),
        "w_mix_out": w["w_mix_out"].astype(BF16),
        "norm_memq": row(w["norm_memq"]), "norm_memkv": row(w["norm_memkv"]),
        "wq_mem": w["wq_mem"].astype(BF16), "wkv_mem": w["wkv_mem"].astype(BF16),
        "wo_mem": w["wo_mem"].astype(BF16),
        "norm_ffn": row(w["norm_ffn"]),
        "router_wt": jnp.stack([r_hi, r_lo], axis=1), "router_bt": router_b,
        "w_e_gate": experts(w["w_e_gate"]), "w_e_up": experts(w["w_e_up"]),
        "w_e_down": experts(w["w_e_down"]),
    }


def _layer_block(a, l):
    index = (l,) + (0,) * (a.ndim - 1)
    return pl.BlockSpec((None,) + a.shape[1:], lambda *_: index, pipeline_mode=pl.Buffered(1))


def _tile(total, want):
    t = min(want, total)
    assert total % t == 0, (total, t)
    return t


def kernel(x, mem, norm_mix, w_in, mu_in, w_vres_in, mu_vres, w0, w_decay_up, a0, w_aaa_up, w_gate_up,
           v0, w_vres_up, k_k, k_a, r_k, lnx_g, lnx_b, conv_w, gate_b, w_br_a, w_br_b, w_mix_out,
           norm_memq, norm_memkv, wq_mem, wkv_mem, wo_mem, norm_ffn, router_g, router_g_b, router_e,
           router_e_b, w_e_gate, w_e_up, w_e_down, norm_f):
    w = dict(norm_mix=norm_mix, w_in=w_in, mu_in=mu_in, w_vres_in=w_vres_in, mu_vres=mu_vres, w0=w0,
             w_decay_up=w_decay_up, a0=a0, w_aaa_up=w_aaa_up, w_gate_up=w_gate_up, v0=v0,
             w_vres_up=w_vres_up, k_k=k_k, k_a=k_a, r_k=r_k, lnx_g=lnx_g, lnx_b=lnx_b, conv_w=conv_w,
             gate_b=gate_b, w_br_a=w_br_a, w_br_b=w_br_b, w_mix_out=w_mix_out, norm_memq=norm_memq,
             norm_memkv=norm_memkv, wq_mem=wq_mem, wkv_mem=wkv_mem, wo_mem=wo_mem, norm_ffn=norm_ffn,
             router_g=router_g, router_g_b=router_g_b, router_e=router_e, router_e_b=router_e_b,
             w_e_gate=w_e_gate, w_e_up=w_e_up, w_e_down=w_e_down)
    bsz, seq, _ = x.shape
    depth = norm_mix.shape[0]
    n = bsz * seq
    assert seq % CHUNK == 0 and mem.shape[1] == MEM_LEN
    x2 = x.reshape(n, D)
    mem2 = mem.reshape(bsz * MEM_LEN, D)
    tm_in = _tile(seq, 512)
    tc_wkv = _tile(seq, CHUNK * WKV_UNROLL * WKV_WAVES)
    assert seq % MOE_TM == 0
    norm_f2 = norm_f.reshape(1, D)

    wts = _prepare_weights(w)
    v_first = None
    for l in range(depth):
        r, k, v, kk, aa, lwd, og, yb, ga, gb = _mixin(x2, v_first, wts, l, tm_in, seq)
        if l == 0:
            v_first = v
        ya = _wkv(r, k, v, kk, aa, lwd, wts, l, bsz, seq, tc_wkv)
        x2, hn, meta, pos, cnt = _attn(x2, ya, og, yb, ga, gb, mem2, wts, l, seq)
        x2 = _moe(x2, hn, meta, pos, cnt, wts, l, norm_f2, l == depth - 1)
    return x2.reshape(bsz, seq, D)
```

```python
import functools
import math

import jax
import jax.numpy as jnp
from jax import lax
from jax.experimental import pallas as pl
from jax.experimental.pallas import tpu as pltpu

F32 = jnp.float32
BF16 = jnp.bfloat16

D = 1024
HEAD = 64
N_HEADS = 16
LANES = 128
SUBLANES = 8
N_PAIRS = D // LANES
CHUNK = 64
WKV_UNROLL = 2
WKV_WAVES = 4
LORA_DECAY, LORA_AAA, LORA_GATE, LORA_VRES = 64, 64, 160, 32
LORA_W = 512
CONV_K = 3
MEM_LEN = 256
MEM_HEADS = 4
MEM_HD = D // MEM_HEADS
N_GROUPS, EPG, N_EXPERTS, D_EXPERT = 4, 4, 16, 512
RMS_EPS, LNX_EPS, L2_EPS = 1e-6, 64e-5, 1e-12
VMEM_LIMIT = 56 * 1024 * 1024


def _cparams(sem):
    return pltpu.CompilerParams(dimension_semantics=sem, vmem_limit_bytes=VMEM_LIMIT)


def _rms(x, g):
    return x * lax.rsqrt(jnp.mean(x * x, axis=-1, keepdims=True) + RMS_EPS) * g


def _sigmoid(x):
    return 0.5 * jnp.tanh(0.5 * x) + 0.5


def _dot(a, b):
    return jnp.dot(a, b, preferred_element_type=F32)


def _dot_nt(a, b):
    return lax.dot_general(a, b, (((1,), (1,)), ((), ())), preferred_element_type=F32)


def _split_bf16(x):
    hi = x.astype(BF16)
    lo = (x - hi.astype(F32)).astype(BF16)
    return hi, lo


BLOCKS = (BLK_R, BLK_K, BLK_V, BLK_CX, BLK_CC, BLK_CB, BLK_GA, BLK_GB) = tuple(range(8))


def _mixin_kernel(has_vres, tiles_per_seq, *refs):
    n_in = 17 if has_vres else 14
    ins, outs, scr = refs[:n_in], refs[n_in:n_in + 10], refs[n_in + 10:]
    (x_ref, g_ref, wm_ref, wl_ref, mu_ref, mul_ref, wdec_ref, waaa_ref, wgate_ref, w0_ref, a0_ref,
     kk_ref, ka_ref, cw_ref) = ins[:14]
    ro_ref, ko_ref, vo_ref, kko_ref, aao_ref, lwo_ref, ogo_ref, ybo_ref, gao_ref, gbo_ref = outs
    clo_ref, crkv_ref, cu_ref = scr
    if has_vres:
        vf_ref, wvres_ref, v0_ref = ins[14:]
    tm = x_ref.shape[0]

    @pl.when(pl.program_id(0) % tiles_per_seq == 0)
    def _():
        crkv_ref[...] = jnp.zeros_like(crkv_ref)
        clo_ref[...] = jnp.zeros_like(clo_ref)
        cu_ref[...] = jnp.zeros_like(cu_ref)

    xn = _rms(x_ref[...], g_ref[...]).astype(BF16)

    def mixed(z, carry, mu):
        return z + (_shift_rows(z, carry, 1) - z) * mu

    def block(b):
        return _dot(xn, wm_ref[b])

    def shifted_block(idx):
        z = block(idx)
        cols = slice(idx * D, (idx + 1) * D)
        zm = mixed(z, crkv_ref[:, cols], mu_ref[:, cols])
        crkv_ref[:, cols] = z[tm - SUBLANES:, :]
        return zm

    lo = _dot(xn, wl_ref[...])
    lom = mixed(lo, clo_ref[...], mul_ref[...])
    clo_ref[...] = lo[tm - SUBLANES:, :]
    lane = lax.broadcasted_iota(jnp.int32, (tm, LANES), 1)
    t0, t1, t2 = (lom[:, t * LANES:(t + 1) * LANES] for t in range(3))
    gate_tail = LORA_DECAY + LORA_AAA + LORA_GATE - 2 * LANES
    act_a = jnp.where(lane < LORA_DECAY, jnp.tanh(t0), t0).astype(BF16)
    act_b = jnp.concatenate([_sigmoid(t1), jnp.where(lane < gate_tail, _sigmoid(t2), t2)],
                            axis=1).astype(BF16)

    u = w0_ref[...] + _dot(act_a, wdec_ref[...])
    lwo_ref[...] = -math.exp(-0.5) * _sigmoid(u)
    ro_ref[...] = shifted_block(BLK_R).astype(BF16)

    aa = _sigmoid(a0_ref[...] + _dot(act_a, waaa_ref[...]))
    aao_ref[...] = aa.astype(BF16)
    zm = shifted_block(BLK_K)
    kko_ref[...] = (zm * kk_ref[...]).astype(BF16)
    ko_ref[...] = (zm * (1.0 + (aa - 1.0) * ka_ref[...])).astype(BF16)

    if has_vres:
        vgate = _sigmoid(v0_ref[...] + _dot(act_b, wvres_ref[...]))
    zm = shifted_block(BLK_V)
    if has_vres:
        zm = zm + (vf_ref[...].astype(F32) - zm) * vgate
    vo_ref[...] = zm.astype(BF16)

    ogo_ref[...] = _dot(act_b, wgate_ref[...]).astype(BF16)

    u = block(BLK_CC) * block(BLK_CX)
    cu = cu_ref[...]
    conv = cw_ref[CONV_K - 1:CONV_K, :] * u
    for t in range(CONV_K - 1):
        conv = conv + cw_ref[t:t + 1, :] * _shift_rows(u, cu, CONV_K - 1 - t)
    cu_ref[...] = u[tm - SUBLANES:, :]
    ybo_ref[...] = (block(BLK_CB) * conv).astype(BF16)

    gao_ref[...] = block(BLK_GA).astype(BF16)
    gbo_ref[...] = block(BLK_GB).astype(BF16)


def _mixin(x2, vfirst, wts, l, tm, seq):
    n = x2.shape[0]
    has_vres = vfirst is not None
    tile = pl.BlockSpec((tm, D), lambda i: (i, 0))
    names = ["norm_mix", "w_main", "w_lora", "mu_rkv", "mu_lora", "wup_dec", "wup_aaa", "wup_gate", "w0",
             "a0", "k_k", "k_a", "conv_w"]
    in_specs = [tile] + [_layer_block(wts[name], l) for name in names]
    in_specs[1 + names.index("w_main")] = pl.BlockSpec(
        (len(BLOCKS), None, D, D), lambda i: (0, l, 0, 0), pipeline_mode=pl.Buffered(1))
    args = [x2] + [wts[name] for name in names]
    if has_vres:
        in_specs += [tile, _layer_block(wts["wup_vres"], l - 1), _layer_block(wts["v0"], l - 1)]
        args += [vfirst, wts["wup_vres"], wts["v0"]]
    bf = jax.ShapeDtypeStruct((n, D), BF16)
    out_shape = [bf, bf, bf, bf, bf, jax.ShapeDtypeStruct((n, D), F32), bf, bf, bf, bf]
    return pl.pallas_call(
        functools.partial(_mixin_kernel, has_vres, seq // tm),
        grid=(n // tm,),
        in_specs=in_specs,
        out_specs=[tile] * 10,
        out_shape=out_shape,
        scratch_shapes=[pltpu.VMEM((SUBLANES, LORA_W), F32), pltpu.VMEM((SUBLANES, 3 * D), F32),
                        pltpu.VMEM((SUBLANES, D), F32)],
        compiler_params=_cparams(("arbitrary",)),
        name="mixin",
    )(*args)


def _shift_rows(z, prev_rows, k):
    rolled = pltpu.roll(z, k, 0)
    rows = lax.broadcasted_iota(jnp.int32, z.shape, 0)
    out = rolled
    for j in range(k):
        src = prev_rows[SUBLANES - k + j:SUBLANES - k + j + 1, :]
        out = jnp.where(rows == j, src, out)
    return out


def _pair_expand(x, m0, m1):
    return jnp.concatenate([jnp.where(m0, x, 0.0), jnp.where(m1, x, 0.0)], axis=0)


def _halves_sum(x, m0):
    s0 = jnp.sum(jnp.where(m0, x, 0.0), axis=-1, keepdims=True)
    s1 = jnp.sum(jnp.where(m0, 0.0, x), axis=-1, keepdims=True)
    return jnp.where(m0, s0, s1)


def _wkv_kernel(r_ref, k_ref, v_ref, kk_ref, aa_ref, lw_ref, rk_ref, g_ref, b_ref, o_ref, s_ref):
    @pl.when(pl.program_id(1) == 0)
    def _():
        s_ref[...] = jnp.zeros_like(s_ref)

    c2 = 2 * CHUNK
    lane = lax.broadcasted_iota(jnp.int32, (CHUNK, LANES), 1)
    m0 = lane < HEAD
    m1 = jnp.logical_not(m0)
    rows = lax.broadcasted_iota(jnp.int32, (c2, c2), 0)
    cols = lax.broadcasted_iota(jnp.int32, (c2, c2), 1)
    strict = cols < rows
    incl = cols <= rows
    eye = cols == rows
    tri = (lax.broadcasted_iota(jnp.int32, (CHUNK, CHUNK), 1)
           <= lax.broadcasted_iota(jnp.int32, (CHUNK, CHUNK), 0)).astype(BF16)
    n_chunks = r_ref.shape[0] // CHUNK

    def wave(first_chunk):
        t0s = [pl.multiple_of((first_chunk + u) * CHUNK, CHUNK) for u in range(WKV_UNROLL)]
        probs = [(t0, slice(pi * LANES, (pi + 1) * LANES)) for t0 in t0s for pi in range(N_PAIRS)]
        cat0 = lambda *xs: jnp.concatenate(xs, axis=0)
        cat1 = lambda *xs: jnp.concatenate(xs, axis=1)
        expand = lambda x: _pair_expand(x, m0, m1)
        ld = lambda ref, p: ref[pl.ds(p[0], CHUNK), p[1]].astype(F32)

        lw_all = [lw_ref[pl.ds(t0, CHUNK), :] for t0 in t0s]
        cum_all = []
        for lw_u in lw_all:
            lw_hi, lw_lo = _split_bf16(lw_u)
            cum_all.append(_dot(tri, lw_hi) + _dot(tri, lw_lo))
        lw = [lw_u[:, pi * LANES:(pi + 1) * LANES] for lw_u in lw_all for pi in range(N_PAIRS)]
        cum = [c_u[:, pi * LANES:(pi + 1) * LANES] for c_u in cum_all for pi in range(N_PAIRS)]
        cum_end = [c[CHUNK - 1:CHUNK, :] for c in cum]
        w_inv = [jnp.exp(-c) for c in cum]
        w_end = [jnp.exp(ce - c) for c, ce in zip(cum, cum_end)]
        r = [ld(r_ref, p) for p in probs]
        k = [ld(k_ref, p) for p in probs]
        kk = [ld(kk_ref, p) for p in probs]
        kk = [x / jnp.maximum(jnp.sqrt(_halves_sum(x * x, m0)), L2_EPS) for x in kk]
        b = [x * ld(aa_ref, p) for x, p in zip(kk, probs)]
        a_t = [expand(-x * jnp.exp(c - l)) for x, c, l in zip(kk, cum, lw)]
        r_t = [expand(x * jnp.exp(c)) for x, c in zip(r, cum)]
        lhs1 = [cat0(a, rr).astype(BF16) for a, rr in zip(a_t, r_t)]
        rhs1 = [cat0(expand(bb * wi), expand(kx * wi)).astype(BF16) for bb, kx, wi in zip(b, k, w_inv)]
        aa_m = [_dot_nt(x, y) for x, y in zip(lhs1, rhs1)]
        a_ab = [jnp.where(strict, m[:c2, :c2], 0.0) for m in aa_m]
        a_ak = [jnp.where(strict, m[:c2, c2:], 0.0).astype(BF16) for m in aa_m]
        lhs_top = [cat1(jnp.where(incl, m[c2:, :c2], 0.0), jnp.where(incl, m[c2:, c2:], 0.0)).astype(BF16)
                   for m in aa_m]

        t_m = [jnp.where(eye, 1.0, 0.0) + m for m in a_ab]
        a_b = [m.astype(BF16) for m in a_ab]
        a_m = [_dot(x, x) for x in a_b]
        n_steps = CHUNK.bit_length() - 1
        for step in range(1, n_steps):
            a_b = [m.astype(BF16) for m in a_m]
            if step < n_steps - 1:
                both = [_dot(x, cat1(x, t.astype(BF16))) for x, t in zip(a_b, t_m)]
                a_m = [m[:, :c2] for m in both]
                t_m = [t + m[:, c2:] for t, m in zip(t_m, both)]
            else:
                t_m = [t + _dot(x, t.astype(BF16)) for x, t in zip(a_b, t_m)]

        v = [ld(v_ref, p) for p in probs]
        v_b = [expand(x).astype(BF16) for x in v]
        akv = [_dot(x, y) for x, y in zip(a_ak, v_b)]
        tx = [_dot(t.astype(BF16), cat1(a, y).astype(BF16)) for t, a, y in zip(t_m, a_t, akv)]
        rhs2 = [cat0(x.astype(BF16), cat1(jnp.zeros((c2, c2), BF16), y)) for x, y in zip(tx, v_b)]
        lhs_bot = [cat1(expand(bb * we).T, expand(kx * we).T).astype(BF16)
                   for bb, kx, we in zip(b, k, w_end)]
        z = [_dot(cat0(x, y), w) for x, y, w in zip(lhs_top, lhs_bot, rhs2)]
        gp = [cat0(rt + m[:c2, :c2], m[c2:, :c2] + jnp.where(eye, jnp.exp(ce), 0.0)).astype(BF16)
              for rt, m, ce in zip(r_t, z, cum_end)]

        state = [s_ref[pi] for pi in range(N_PAIRS)]
        y = []
        for u in range(WKV_UNROLL):
            base = u * N_PAIRS
            zz = [_dot(gp[base + pi], state[pi].astype(BF16)) for pi in range(N_PAIRS)]
            state = [zz[pi][c2:] + z[base + pi][c2:, c2:] for pi in range(N_PAIRS)]
            y += [(zz[pi][:CHUNK] + zz[pi][CHUNK:c2])
                  + (z[base + pi][:CHUNK, c2:] + z[base + pi][CHUNK:c2, c2:]) for pi in range(N_PAIRS)]
        for pi in range(N_PAIRS):
            s_ref[pi] = state[pi]

        for i, (t0, sl) in enumerate(probs):
            mu = _halves_sum(y[i], m0) * (1.0 / HEAD)
            yc = y[i] - mu
            var = _halves_sum(yc * yc, m0) * (1.0 / HEAD)
            yn = yc * lax.rsqrt(var + LNX_EPS) * g_ref[:, sl] + b_ref[:, sl]
            bonus = _halves_sum(r[i] * k[i] * rk_ref[:, sl], m0) * v[i]
            o_ref[pl.ds(t0, CHUNK), sl] = (yn + bonus).astype(BF16)

    per_trip = WKV_UNROLL * WKV_WAVES
    assert n_chunks % per_trip == 0

    def trip(ti, carry):
        for w in range(WKV_WAVES):
            wave(ti * per_trip + w * WKV_UNROLL)
        return carry

    lax.fori_loop(0, n_chunks // per_trip, trip, 0)


def _wkv(r, k, v, kk, aa, lw, wts, l, bsz, seq, tc):
    n = r.shape[0]
    nt = seq // tc
    blk = pl.BlockSpec((tc, D), lambda b, t: (b * nt + t, 0))
    rows = [wts["r_k"], wts["lnx_g"], wts["lnx_b"]]
    return pl.pallas_call(
        _wkv_kernel,
        grid=(bsz, nt),
        in_specs=[blk] * 6 + [_layer_block(a, l) for a in rows],
        out_specs=blk,
        out_shape=jax.ShapeDtypeStruct((n, D), BF16),
        scratch_shapes=[pltpu.VMEM((N_PAIRS, LANES, LANES), F32)],
        compiler_params=_cparams(("parallel", "arbitrary")),
        name="wkv",
    )(r, k, v, kk, aa, lw, *rows)


def _attn_kernel(tiles_per_seq, x_ref, ya_ref, og_ref, yb_ref, ga_ref, gb_ref, mem_ref, gbias_ref, wa_ref,
                 wb_ref, wmix_ref, gq_ref, wq_ref, wo_ref, gkv_ref, wkv_ref, gffn_ref, rw_ref, rb_ref,
                 o_ref, hn_ref, meta_ref, pos_ref, cnt_ref, kv_ref):
    @pl.when(pl.program_id(0) % tiles_per_seq == 0)
    def _():
        mn = _rms(mem_ref[...], gkv_ref[...]).astype(BF16)
        kv_ref[...] = _dot(mn, wkv_ref[...]).astype(BF16)

    y_a = (ya_ref[...].astype(F32) * og_ref[...].astype(F32)).astype(BF16)
    g_a = _sigmoid(ga_ref[...].astype(F32) + gbias_ref[:, :D])
    g_b = _sigmoid(gb_ref[...].astype(F32) + gbias_ref[:, D:])
    merged = g_a * _dot(y_a, wa_ref[...]) + g_b * _dot(yb_ref[...], wb_ref[...])
    x1 = x_ref[...] + _dot(merged.astype(BF16), wmix_ref[...])

    hn = _rms(x1, gq_ref[...]).astype(BF16)
    q = _dot(hn, wq_ref[...]).astype(BF16)
    heads = []
    for h in range(MEM_HEADS):
        sl = slice(h * MEM_HD, (h + 1) * MEM_HD)
        s = _dot_nt(q[:, sl], kv_ref[:, sl]) * (MEM_HD ** -0.5)
        s = s - jnp.max(s, axis=-1, keepdims=True)
        e = jnp.exp(s)
        pr = (e / jnp.sum(e, axis=-1, keepdims=True)).astype(BF16)
        heads.append(_dot(pr, kv_ref[:, D + h * MEM_HD:D + (h + 1) * MEM_HD]).astype(BF16))
    o = jnp.concatenate(heads, axis=1)
    x2 = x1 + _dot(o, wo_ref[...])
    o_ref[...] = x2
    _route_tile(x2, gffn_ref, rw_ref, rb_ref, hn_ref, meta_ref, pos_ref, cnt_ref)


def _attn(x2, ya, og, yb, ga, gb, mem2, wts, l, seq):
    n = x2.shape[0]
    tm = MOE_TM
    nt = n // tm
    tps = seq // tm
    tile = pl.BlockSpec((tm, D), lambda i: (i, 0))
    params = [wts[name] for name in ("gate_b", "w_br_a", "w_br_b", "w_mix_out", "norm_memq", "wq_mem",
                                     "wo_mem", "norm_memkv", "wkv_mem", "norm_ffn", "router_wt",
                                     "router_bt")]
    return pl.pallas_call(
        functools.partial(_attn_kernel, tps),
        grid=(nt,),
        in_specs=[tile] * 6 + [pl.BlockSpec((MEM_LEN, D), lambda i: (i // tps, 0))]
        + [_layer_block(a, l) for a in params],
        out_specs=[tile, tile, pl.BlockSpec((tm, LANES), lambda i: (i, 0)),
                   pl.BlockSpec((1, SUBLANES, tm), lambda i: (i, 0, 0)),
                   pl.BlockSpec((1, SUBLANES, LANES), lambda i: (i, 0, 0))],
        out_shape=[jax.ShapeDtypeStruct((n, D), F32), jax.ShapeDtypeStruct((n, D), BF16),
                   jax.ShapeDtypeStruct((n, LANES), F32), jax.ShapeDtypeStruct((nt, SUBLANES, tm), F32),
                   jax.ShapeDtypeStruct((nt, SUBLANES, LANES), jnp.int32)],
        scratch_shapes=[pltpu.VMEM((MEM_LEN, 2 * D), BF16)],
        compiler_params=_cparams(("arbitrary",)),
        name="attn",
    )(x2, ya, og, yb, ga, gb, mem2, *params)


MOE_TM = 512
MOE_TP = MOE_TM + LANES
MOE_T = 512
RUN_ALIGN = 16
POS_ROW = N_GROUPS + N_EXPERTS
RUN_BITS = tuple(range((MOE_TM // RUN_ALIGN).bit_length() - 1, -1, -1))


def _route_tile(x, g_ref, rw_ref, rb_ref, hn_ref, meta_ref, pos_ref, cnt_ref):
    tm = x.shape[0]
    hn = _rms(x, g_ref[...])
    hi, lo = _split_bf16(hn)
    hn_ref[...] = hi
    lt = _dot_nt(rw_ref[0], hi) + _dot_nt(rw_ref[0], lo) + _dot_nt(rw_ref[1], hi) + rb_ref[...]
    row = lax.broadcasted_iota(jnp.int32, lt.shape, 0).astype(F32)
    neg = jnp.float32(-jnp.inf)
    big = jnp.float32(1 << 20)
    is_g = row < N_GROUPS
    lg = jnp.where(is_g, lt, neg)
    gmax = jnp.max(lg, axis=0, keepdims=True)
    ge = jnp.where(is_g, jnp.exp(lg - gmax), 0.0)
    gp = ge / jnp.sum(ge, axis=0, keepdims=True)
    g_w = jnp.max(gp, axis=0, keepdims=True)
    g_i = jnp.min(jnp.where(is_g & (gp == g_w), row, big), axis=0, keepdims=True)
    first = N_GROUPS + g_i * EPG
    sel = (row >= first) & (row < first + EPG)
    le = jnp.where(sel, lt, neg)
    emax = jnp.max(le, axis=0, keepdims=True)
    ee = jnp.where(sel, jnp.exp(le - emax), 0.0)
    ep = ee / jnp.sum(ee, axis=0, keepdims=True)
    p1 = jnp.max(ep, axis=0, keepdims=True)
    i1 = jnp.min(jnp.where(sel & (ep == p1), row, big), axis=0, keepdims=True)
    rest = sel & (row != i1)
    p2 = jnp.max(jnp.where(rest, ep, -1.0), axis=0, keepdims=True)
    i2 = jnp.min(jnp.where(rest & (ep == p2), row, big), axis=0, keepdims=True)
    tot = p1 + p2
    cw = jnp.where(row == i1, g_w * (p1 / tot), 0.0) + jnp.where(row == i2, g_w * (p2 / tot), 0.0)

    grow = lax.broadcasted_iota(jnp.int32, (SUBLANES, tm), 0).astype(F32)
    onehot = jnp.where(grow == g_i, 1.0, 0.0)
    before = (lax.broadcasted_iota(jnp.int32, (tm, tm), 0)
              < lax.broadcasted_iota(jnp.int32, (tm, tm), 1)).astype(BF16)
    rank = _dot(onehot.astype(BF16), before)
    count = jnp.sum(onehot, axis=1, keepdims=True)
    run = jnp.floor((count + (RUN_ALIGN - 1)) * (1.0 / RUN_ALIGN)) * RUN_ALIGN
    gcol = lax.broadcasted_iota(jnp.int32, (SUBLANES, 1), 0)
    start = jnp.zeros((SUBLANES, 1), F32)
    acc = jnp.zeros((1, 1), F32)
    for g in range(1, N_GROUPS):
        acc = acc + run[g - 1:g, :]
        start = start + jnp.where(gcol == g, acc, 0.0)
    pos = jnp.sum(onehot * (start + rank), axis=0, keepdims=True)
    meta_ref[...] = (cw + jnp.where(row == POS_ROW, pos, 0.0)).T
    pos_ref[0] = jnp.broadcast_to(pos, (SUBLANES, tm))
    cnt_ref[0] = jnp.broadcast_to(run, (SUBLANES, LANES)).astype(jnp.int32)


def _advance_offsets(lens_ref, off_ref, tile, cap):
    @pl.when(tile == 0)
    def _():
        for g in range(N_GROUPS):
            off_ref[0, g] = g * cap

    @pl.when(tile > 0)
    def _():
        for g in range(N_GROUPS):
            off_ref[tile % 2, g] = off_ref[(tile + 1) % 2, g] + lens_ref[(tile - 1) * N_GROUPS + g]


def _run_copies(lens_ref, off_ref, tile, make_copy):
    out = []
    local = 0
    for g in range(N_GROUPS):
        length = lens_ref[tile * N_GROUPS + g]
        base = off_ref[tile % 2, g]
        for bit in RUN_BITS:
            rows = RUN_ALIGN << bit
            done = length & (-2 * rows)
            cond = (length & rows) != 0
            out.append((cond, make_copy(pl.multiple_of(local + done, RUN_ALIGN),
                                        pl.multiple_of(base + done, RUN_ALIGN), rows)))
        local = local + length
    return out


def _dispatch_kernel(cap, lens_ref, hn_ref, meta_ref, pos_ref, xs_ref, ms_ref,
                     comp_ref, compm_ref, zx_ref, zm_ref, off_ref, sem):
    i = pl.program_id(0)
    last = pl.num_programs(0) - 1
    tm = hn_ref.shape[0]
    _advance_offsets(lens_ref, off_ref, i, cap)

    def copies_of(tile):
        buf = tile % 2

        def make_copy(local, sorted_row, rows):
            return (pltpu.make_async_copy(comp_ref.at[buf, pl.ds(local, rows)],
                                          xs_ref.at[pl.ds(sorted_row, rows)], sem.at[0, buf]),
                    pltpu.make_async_copy(compm_ref.at[buf, pl.ds(local, rows)],
                                          ms_ref.at[pl.ds(sorted_row, rows)], sem.at[1, buf]))

        return _run_copies(lens_ref, off_ref, tile, make_copy)

    def wait_copies(tile):
        for cond, (cx, cm) in copies_of(tile):
            @pl.when(cond)
            def _(cx=cx, cm=cm):
                cx.wait()
                cm.wait()

    slot = lax.broadcasted_iota(jnp.int32, (MOE_TP, tm), 0).astype(F32)
    perm = jnp.where(slot == pos_ref[0][0:1, :], 1.0, 0.0).astype(BF16)
    comp_ref[i % 2] = _dot(perm, hn_ref[...]).astype(BF16)
    m = meta_ref[...]
    m1 = m.astype(BF16)
    r1 = m - m1.astype(F32)
    m2 = r1.astype(BF16)
    m3 = (r1 - m2.astype(F32)).astype(BF16)
    compm_ref[i % 2] = _dot(perm, m1) + _dot(perm, m2) + _dot(perm, m3)

    for cond, (cx, cm) in copies_of(i):
        @pl.when(cond)
        def _(cx=cx, cm=cm):
            cx.start()
            cm.start()

    @pl.when(i > 0)
    def _():
        wait_copies(i - 1)

    @pl.when(i == last)
    def _():
        wait_copies(i)
        zx_ref[...] = jnp.zeros_like(zx_ref)
        zm_ref[...] = jnp.zeros_like(zm_ref)
        fills = []
        for g in range(N_GROUPS):
            row0 = pl.multiple_of(off_ref[i % 2, g] + lens_ref[i * N_GROUPS + g], RUN_ALIGN)
            fills.append(pltpu.make_async_copy(zx_ref, xs_ref.at[pl.ds(row0, MOE_T)], sem.at[0, 0]))
            fills.append(pltpu.make_async_copy(zm_ref, ms_ref.at[pl.ds(row0, MOE_T)], sem.at[1, 0]))
        for c in fills:
            c.start()
        for c in fills:
            c.wait()


def _dispatch(hn, meta, pos, lens, cap):
    n = hn.shape[0]
    nt = n // MOE_TM
    return pl.pallas_call(
        functools.partial(_dispatch_kernel, cap),
        grid_spec=pltpu.PrefetchScalarGridSpec(
            num_scalar_prefetch=1,
            grid=(nt,),
            in_specs=[pl.BlockSpec((MOE_TM, D), lambda i, *_: (i, 0)),
                      pl.BlockSpec((MOE_TM, LANES), lambda i, *_: (i, 0)),
                      pl.BlockSpec((1, SUBLANES,MOE_TM), lambda i, *_: (i, 0, 0))],
            out_specs=[pl.BlockSpec(memory_space=pl.ANY), pl.BlockSpec(memory_space=pl.ANY)],
            scratch_shapes=[pltpu.VMEM((2, MOE_TP, D), BF16), pltpu.VMEM((2, MOE_TP, LANES), F32),
                            pltpu.VMEM((MOE_T, D), BF16), pltpu.VMEM((MOE_T, LANES), F32),
                            pltpu.SMEM((2, N_GROUPS), jnp.int32), pltpu.SemaphoreType.DMA((2, 2))]),
        out_shape=[jax.ShapeDtypeStruct((N_GROUPS * cap, D), BF16),
                   jax.ShapeDtypeStruct((N_GROUPS * cap, LANES), F32)],
        compiler_params=_cparams(("arbitrary",)),
        name="dispatch",
    )(lens, hn, meta, pos)


def _tile_group(j, tend_ref):
    jc = jnp.minimum(j, jnp.maximum(tend_ref[N_GROUPS - 1] - 1, 0))
    g = sum((jc >= tend_ref[t]).astype(jnp.int32) for t in range(N_GROUPS - 1))
    return jc, g


def _experts_kernel(tend_ref, tstart_ref, xs_ref, ms_ref, wg_ref, wu_ref, wd_ref, ys_ref, wg_s, wu_s,
                    wd_s):
    j = pl.program_id(0)

    @pl.when(j < tend_ref[N_GROUPS - 1])
    def _():
        group = _tile_group(j, tend_ref)[1]

        @pl.when(j == tstart_ref[group])
        def _():
            for e in range(EPG):
                wg_s[e] = wg_ref[e].astype(BF16)
                wu_s[e] = wu_ref[e].astype(BF16)
                wd_s[e] = wd_ref[e].astype(BF16)

        xs = xs_ref[...]
        ms = ms_ref[...]
        lane = lax.broadcasted_iota(jnp.int32, ms.shape, 1)
        first = N_GROUPS + group * EPG
        acc = None
        for e in range(EPG):
            gate = _dot(xs, wg_s[e])
            he = (gate * _sigmoid(gate) * _dot(xs, wu_s[e])).astype(BF16)
            cw = jnp.sum(jnp.where(lane == first + e, ms, 0.0), axis=-1, keepdims=True)
            term = cw * _dot(he, wd_s[e])
            acc = term if acc is None else acc + term
        ys_ref[...] = acc.astype(BF16)


def _experts(xs, ms, tend, tstart, wts, l, n_tiles, cap):
    def wmap(j, tend, tstart):
        return (l * N_GROUPS + _tile_group(j, tend)[1], 0, 0)

    def rmap(j, tend, tstart):
        jc, g = _tile_group(j, tend)
        return (g * (cap // MOE_T) + jc - tstart[g], 0)

    return pl.pallas_call(
        _experts_kernel,
        grid_spec=pltpu.PrefetchScalarGridSpec(
            num_scalar_prefetch=2,
            grid=(n_tiles,),
            in_specs=[pl.BlockSpec((MOE_T, D), rmap), pl.BlockSpec((MOE_T, LANES), rmap),
                      pl.BlockSpec((EPG, D, D_EXPERT), wmap, pipeline_mode=pl.Buffered(1)),
                      pl.BlockSpec((EPG, D, D_EXPERT), wmap, pipeline_mode=pl.Buffered(1)),
                      pl.BlockSpec((EPG, D_EXPERT, D), wmap, pipeline_mode=pl.Buffered(1))],
            out_specs=pl.BlockSpec((MOE_T, D), rmap),
            scratch_shapes=[pltpu.VMEM((EPG, D, D_EXPERT), BF16), pltpu.VMEM((EPG, D, D_EXPERT), BF16),
                            pltpu.VMEM((EPG, D_EXPERT, D), BF16)]),
        out_shape=jax.ShapeDtypeStruct(xs.shape, BF16),
        compiler_params=_cparams(("arbitrary",)),
        name="experts",
    )(tend, tstart, xs, ms, wts["w_e_gate"], wts["w_e_up"], wts["w_e_down"])


def _combine_kernel(final_norm, cap, lens_ref, x_ref, meta_ref, ys_ref, gf_ref, o_ref, comp_ref,
                    off_ref, sem):
    i = pl.program_id(0)
    tm = x_ref.shape[0]

    def copies_of(tile):
        buf = tile % 2

        def make_copy(local, sorted_row, rows):
            return pltpu.make_async_copy(ys_ref.at[pl.ds(sorted_row, rows)],
                                         comp_ref.at[buf, pl.ds(local, rows)], sem.at[buf])

        return _run_copies(lens_ref, off_ref, tile, make_copy)

    def start_copies(tile):
        _advance_offsets(lens_ref, off_ref, tile, cap)
        for cond, c in copies_of(tile):
            pl.when(cond)(c.start)

    @pl.when(i == 0)
    def _():
        comp_ref[...] = jnp.zeros_like(comp_ref)
        start_copies(i)

    @pl.when(i < pl.num_programs(0) - 1)
    def _():
        start_copies(i + 1)

    for cond, c in copies_of(i):
        pl.when(cond)(c.wait)

    lane = lax.broadcasted_iota(jnp.int32, meta_ref.shape, 1)
    pos = jnp.sum(jnp.where(lane == POS_ROW, meta_ref[...], 0.0), axis=-1, keepdims=True)
    slot = lax.broadcasted_iota(jnp.int32, (tm, MOE_TP), 1).astype(F32)
    unperm = jnp.where(slot == pos, 1.0, 0.0).astype(BF16)
    out = x_ref[...] + _dot(unperm, comp_ref[i % 2])
    if final_norm:
        out = _rms(out, gf_ref[...])
    o_ref[...] = out


def _combine(x2, meta, ys, lens, norm_f, final_norm, cap):
    n = x2.shape[0]
    return pl.pallas_call(
        functools.partial(_combine_kernel, final_norm, cap),
        grid_spec=pltpu.PrefetchScalarGridSpec(
            num_scalar_prefetch=1,
            grid=(n // MOE_TM,),
            in_specs=[pl.BlockSpec((MOE_TM, D), lambda i, *_: (i, 0)),
                      pl.BlockSpec((MOE_TM, LANES), lambda i, *_: (i, 0)),
                      pl.BlockSpec(memory_space=pl.ANY),
                      pl.BlockSpec((1, D), lambda i, *_: (0, 0))],
            out_specs=pl.BlockSpec((MOE_TM, D), lambda i, *_: (i, 0)),
            scratch_shapes=[pltpu.VMEM((2, MOE_TP, D), BF16), pltpu.SMEM((2, N_GROUPS), jnp.int32),
                            pltpu.SemaphoreType.DMA((2,))]),
        out_shape=jax.ShapeDtypeStruct((n, D), F32),
        compiler_params=_cparams(("arbitrary",)),
        name="combine",
    )(lens, x2, meta, ys, norm_f)


def _moe(x2, hn, meta, pos, cnt, wts, l, norm_f, final_norm):
    n = x2.shape[0]
    nt = n // MOE_TM
    cap = -(-(n + RUN_ALIGN * nt + MOE_T) // MOE_T) * MOE_T
    n_tiles = (n + N_GROUPS * RUN_ALIGN * nt) // MOE_T + N_GROUPS + 1

    lens = cnt[:, :N_GROUPS, 0]
    tiles = (jnp.sum(lens, axis=0) + (MOE_T - 1)) // MOE_T
    tend = jnp.cumsum(tiles).astype(jnp.int32)
    tstart = tend - tiles
    lens = lens.reshape(-1)

    xs, ms = _dispatch(hn, meta, pos, lens, cap)
    ys = _experts(xs, ms, tend, tstart, wts, l, n_tiles, cap)
    return _combine(x2, meta, ys, lens, norm_f, final_norm, cap)


def _prepare_weights(w):
    depth = w["w_in"].shape[0]
    rw = 3 * D
    n_lo = LORA_DECAY + LORA_AAA + LORA_GATE
    w_in = w["w_in"]
    vres_in = jnp.pad(w["w_vres_in"], ((1, 0), (0, 0), (0, 0)))
    lora = jnp.concatenate([w_in[:, :, rw:rw + n_lo], vres_in], axis=2)
    lora = jnp.pad(lora, ((0, 0), (0, 0), (0, LORA_W - lora.shape[2]))).astype(BF16)
    mu_lora = jnp.concatenate([w["mu_in"][:, rw:rw + n_lo], jnp.pad(w["mu_vres"], ((1, 0), (0, 0)))], axis=1)
    mu_lora = jnp.pad(mu_lora, ((0, 0), (0, LORA_W - mu_lora.shape[1])))
    c0 = rw + n_lo
    starts = [0, D, 2 * D, c0, c0 + 2 * D, c0 + D, c0 + 3 * D, c0 + 4 * D]
    w_main = jnp.stack([w_in[:, :, s:s + D] for s in starts]).astype(BF16)

    def pad_rows(a, before, total):
        return jnp.pad(a, ((0, 0), (before, total - before - a.shape[1]), (0, 0))).astype(BF16)

    router = jnp.swapaxes(jnp.concatenate([w["router_g"], w["router_e"]], axis=2), 1, 2)
    router = jnp.pad(router, ((0, 0), (0, LANES - router.shape[1]), (0, 0)))
    r_hi = router.astype(BF16)
    r_lo = (router - r_hi.astype(F32)).astype(BF16)
    router_b = jnp.concatenate([w["router_g_b"], w["router_e_b"]], axis=1)
    router_b = jnp.pad(router_b, ((0, 0), (0, LANES - router_b.shape[1])))[:, :, None]

    row = lambda a: a.reshape(a.shape[0], 1, -1)
    experts = lambda a: a.reshape(depth * N_EXPERTS, a.shape[2], a.shape[3])
    return {
        "norm_mix": row(w["norm_mix"]), "w_main": w_main, "w_lora": lora,
        "mu_rkv": row(w["mu_in"][:, :rw]), "mu_lora": row(mu_lora),
        "wup_dec": pad_rows(w["w_decay_up"], 0, LANES),
        "wup_aaa": pad_rows(w["w_aaa_up"], LORA_DECAY, LANES),
        "wup_gate": pad_rows(w["w_gate_up"], 0, 2 * LANES),
        "wup_vres": pad_rows(w["w_vres_up"], LORA_GATE, 2 * LANES), "v0": row(w["v0"]),
        "w0": row(w["w0"]), "a0": row(w["a0"]), "k_k": row(w["k_k"]), "k_a": row(w["k_a"]),
        "conv_w": w["conv_w"],
        "r_k": row(w["r_k"]), "lnx_g": row(w["lnx_g"]), "lnx_b": row(w["lnx_b"]),
        "gate_b": row(w["gate_b"]),
        "w_br_a": w["w_br_a"].astype(BF16), "w_br_b": w["w_br_b"].astype(BF16),
        "w_mix_out": w["w_mix_out"].astype(BF16),
        "norm_memq": row(w["norm_memq"]), "norm_memkv": row(w["norm_memkv"]),
        "wq_mem": w["wq_mem"].astype(BF16), "wkv_mem": w["wkv_mem"].astype(BF16),
        "wo_mem": w["wo_mem"].astype(BF16),
        "norm_ffn": row(w["norm_ffn"]),
        "router_wt": jnp.stack([r_hi, r_lo], axis=1), "router_bt": router_b,
        "w_e_gate": experts(w["w_e_gate"]), "w_e_up": experts(w["w_e_up"]),
        "w_e_down": experts(w["w_e_down"]),
    }


def _layer_block(a, l):
    index = (l,) + (0,) * (a.ndim - 1)
    return pl.BlockSpec((None,) + a.shape[1:], lambda *_: index, pipeline_mode=pl.Buffered(1))


def _tile(total, want):
    t = min(want, total)
    assert total % t == 0, (total, t)
    return t


def kernel(x, mem, norm_mix, w_in, mu_in, w_vres_in, mu_vres, w0, w_decay_up, a0, w_aaa_up, w_gate_up,
           v0, w_vres_up, k_k, k_a, r_k, lnx_g, lnx_b, conv_w, gate_b, w_br_a, w_br_b, w_mix_out,
           norm_memq, norm_memkv, wq_mem, wkv_mem, wo_mem, norm_ffn, router_g, router_g_b, router_e,
           router_e_b, w_e_gate, w_e_up, w_e_down, norm_f):
    w = dict(norm_mix=norm_mix, w_in=w_in, mu_in=mu_in, w_vres_in=w_vres_in, mu_vres=mu_vres, w0=w0,
             w_decay_up=w_decay_up, a0=a0, w_aaa_up=w_aaa_up, w_gate_up=w_gate_up, v0=v0,
             w_vres_up=w_vres_up, k_k=k_k, k_a=k_a, r_k=r_k, lnx_g=lnx_g, lnx_b=lnx_b, conv_w=conv_w,
             gate_b=gate_b, w_br_a=w_br_a, w_br_b=w_br_b, w_mix_out=w_mix_out, norm_memq=norm_memq,
             norm_memkv=norm_memkv, wq_mem=wq_mem, wkv_mem=wkv_mem, wo_mem=wo_mem, norm_ffn=norm_ffn,
             router_g=router_g, router_g_b=router_g_b, router_e=router_e, router_e_b=router_e_b,
             w_e_gate=w_e_gate, w_e_up=w_e_up, w_e_down=w_e_down)
    bsz, seq, _ = x.shape
    depth = norm_mix.shape[0]
    n = bsz * seq
    assert seq % CHUNK == 0 and mem.shape[1] == MEM_LEN
    x2 = x.reshape(n, D)
    mem2 = mem.reshape(bsz * MEM_LEN, D)
    tm_in = _tile(seq, 512)
    tc_wkv = _tile(seq, CHUNK * WKV_UNROLL * WKV_WAVES)
    assert seq % MOE_TM == 0
    norm_f2 = norm_f.reshape(1, D)

    wts = _prepare_weights(w)
    v_first = None
    for l in range(depth):
        r, k, v, kk, aa, lwd, og, yb, ga, gb = _mixin(x2, v_first, wts, l, tm_in, seq)
        if l == 0:
            v_first = v
        ya = _wkv(r, k, v, kk, aa, lwd, wts, l, bsz, seq, tc_wkv)
        x2, hn, meta, pos, cnt = _attn(x2, ya, og, yb, ga, gb, mem2, wts, l, seq)
        x2 = _moe(x2, hn, meta, pos, cnt, wts, l, norm_f2, l == depth - 1)
    return x2.reshape(bsz, seq, D)
```

```python
import functools
import math

import jax
import jax.numpy as jnp
from jax import lax
from jax.experimental import pallas as pl
from jax.experimental.pallas import tpu as pltpu

F32 = jnp.float32
BF16 = jnp.bfloat16

D = 1024
HEAD = 64
N_HEADS = 16
LANES = 128
SUBLANES = 8
N_PAIRS = D // LANES
CHUNK = 64
WKV_UNROLL = 2
WKV_WAVES = 4
LORA_DECAY, LORA_AAA, LORA_GATE, LORA_VRES = 64, 64, 160, 32
LORA_W = 512
CONV_K = 3
MEM_LEN = 256
MEM_HEADS = 4
MEM_HD = D // MEM_HEADS
N_GROUPS, EPG, N_EXPERTS, D_EXPERT = 4, 4, 16, 512
RMS_EPS, LNX_EPS, L2_EPS = 1e-6, 64e-5, 1e-12
VMEM_LIMIT = 56 * 1024 * 1024


def _cparams(sem):
    return pltpu.CompilerParams(dimension_semantics=sem, vmem_limit_bytes=VMEM_LIMIT)


def _rms(x, g):
    return x * lax.rsqrt(jnp.mean(x * x, axis=-1, keepdims=True) + RMS_EPS) * g


def _sigmoid(x):
    return 0.5 * jnp.tanh(0.5 * x) + 0.5


def _dot(a, b):
    return jnp.dot(a, b, preferred_element_type=F32)


def _dot_nt(a, b):
    return lax.dot_general(a, b, (((1,), (1,)), ((), ())), preferred_element_type=F32)


def _split_bf16(x):
    hi = x.astype(BF16)
    lo = (x - hi.astype(F32)).astype(BF16)
    return hi, lo


BLOCKS = (BLK_R, BLK_K, BLK_V, BLK_CX, BLK_CC, BLK_CB, BLK_GA, BLK_GB) = tuple(range(8))


def _mixin_kernel(has_vres, tiles_per_seq, *refs):
    n_in = 17 if has_vres else 14
    ins, outs, scr = refs[:n_in], refs[n_in:n_in + 10], refs[n_in + 10:]
    (x_ref, g_ref, wm_ref, wl_ref, mu_ref, mul_ref, wdec_ref, waaa_ref, wgate_ref, w0_ref, a0_ref,
     kk_ref, ka_ref, cw_ref) = ins[:14]
    ro_ref, ko_ref, vo_ref, kko_ref, aao_ref, lwo_ref, ogo_ref, ybo_ref, gao_ref, gbo_ref = outs
    clo_ref, crkv_ref, cu_ref = scr
    if has_vres:
        vf_ref, wvres_ref, v0_ref = ins[14:]
    tm = x_ref.shape[0]

    @pl.when(pl.program_id(0) % tiles_per_seq == 0)
    def _():
        crkv_ref[...] = jnp.zeros_like(crkv_ref)
        clo_ref[...] = jnp.zeros_like(clo_ref)
        cu_ref[...] = jnp.zeros_like(cu_ref)

    xn = _rms(x_ref[...], g_ref[...]).astype(BF16)

    def mixed(z, carry, mu):
        return z + (_shift_rows(z, carry, 1) - z) * mu

    def block(b):
        return _dot(xn, wm_ref[b])

    def shifted_block(idx):
        z = block(idx)
        cols = slice(idx * D, (idx + 1) * D)
        zm = mixed(z, crkv_ref[:, cols], mu_ref[:, cols])
        crkv_ref[:, cols] = z[tm - SUBLANES:, :]
        return zm

    lo = _dot(xn, wl_ref[...])
    lom = mixed(lo, clo_ref[...], mul_ref[...])
    clo_ref[...] = lo[tm - SUBLANES:, :]
    lane = lax.broadcasted_iota(jnp.int32, (tm, LANES), 1)
    t0, t1, t2 = (lom[:, t * LANES:(t + 1) * LANES] for t in range(3))
    gate_tail = LORA_DECAY + LORA_AAA + LORA_GATE - 2 * LANES
    act_a = jnp.where(lane < LORA_DECAY, jnp.tanh(t0), t0).astype(BF16)
    act_b = jnp.concatenate([_sigmoid(t1), jnp.where(lane < gate_tail, _sigmoid(t2), t2)],
                            axis=1).astype(BF16)

    u = w0_ref[...] + _dot(act_a, wdec_ref[...])
    lwo_ref[...] = -math.exp(-0.5) * _sigmoid(u)
    ro_ref[...] = shifted_block(BLK_R).astype(BF16)

    aa = _sigmoid(a0_ref[...] + _dot(act_a, waaa_ref[...]))
    aao_ref[...] = aa.astype(BF16)
    zm = shifted_block(BLK_K)
    kko_ref[...] = (zm * kk_ref[...]).astype(BF16)
    ko_ref[...] = (zm * (1.0 + (aa - 1.0) * ka_ref[...])).astype(BF16)

    if has_vres:
        vgate = _sigmoid(v0_ref[...] + _dot(act_b, wvres_ref[...]))
    zm = shifted_block(BLK_V)
    if has_vres:
        zm = zm + (vf_ref[...].astype(F32) - zm) * vgate
    vo_ref[...] = zm.astype(BF16)

    ogo_ref[...] = _dot(act_b, wgate_ref[...]).astype(BF16)

    u = block(BLK_CC) * block(BLK_CX)
    cu = cu_ref[...]
    conv = cw_ref[CONV_K - 1:CONV_K, :] * u
    for t in range(CONV_K - 1):
        conv = conv + cw_ref[t:t + 1, :] * _shift_rows(u, cu, CONV_K - 1 - t)
    cu_ref[...] = u[tm - SUBLANES:, :]
    ybo_ref[...] = (block(BLK_CB) * conv).astype(BF16)

    gao_ref[...] = block(BLK_GA).astype(BF16)
    gbo_ref[...] = block(BLK_GB).astype(BF16)


def _mixin(x2, vfirst, wts, l, tm, seq):
    n = x2.shape[0]
    has_vres = vfirst is not None
    tile = pl.BlockSpec((tm, D), lambda i: (i, 0))
    names = ["norm_mix", "w_main", "w_lora", "mu_rkv", "mu_lora", "wup_dec", "wup_aaa", "wup_gate", "w0",
             "a0", "k_k", "k_a", "conv_w"]
    in_specs = [tile] + [_layer_block(wts[name], l) for name in names]
    in_specs[1 + names.index("w_main")] = pl.BlockSpec(
        (len(BLOCKS), None, D, D), lambda i: (0, l, 0, 0), pipeline_mode=pl.Buffered(1))
    args = [x2] + [wts[name] for name in names]
    if has_vres:
        in_specs += [tile, _layer_block(wts["wup_vres"], l - 1), _layer_block(wts["v0"], l - 1)]
        args += [vfirst, wts["wup_vres"], wts["v0"]]
    bf = jax.ShapeDtypeStruct((n, D), BF16)
    out_shape = [bf, bf, bf, bf, bf, jax.ShapeDtypeStruct((n, D), F32), bf, bf, bf, bf]
    return pl.pallas_call(
        functools.partial(_mixin_kernel, has_vres, seq // tm),
        grid=(n // tm,),
        in_specs=in_specs,
        out_specs=[tile] * 10,
        out_shape=out_shape,
        scratch_shapes=[pltpu.VMEM((SUBLANES, LORA_W), F32), pltpu.VMEM((SUBLANES, 3 * D), F32),
                        pltpu.VMEM((SUBLANES, D), F32)],
        compiler_params=_cparams(("arbitrary",)),
        name="mixin",
    )(*args)


def _shift_rows(z, prev_rows, k):
    rolled = pltpu.roll(z, k, 0)
    rows = lax.broadcasted_iota(jnp.int32, z.shape, 0)
    out = rolled
    for j in range(k):
        src = prev_rows[SUBLANES - k + j:SUBLANES - k + j + 1, :]
        out = jnp.where(rows == j, src, out)
    return out


def _pair_expand(x, m0, m1):
    return jnp.concatenate([jnp.where(m0, x, 0.0), jnp.where(m1, x, 0.0)], axis=0)


def _halves_sum(x, m0):
    s0 = jnp.sum(jnp.where(m0, x, 0.0), axis=-1, keepdims=True)
    s1 = jnp.sum(jnp.where(m0, 0.0, x), axis=-1, keepdims=True)
    return jnp.where(m0, s0, s1)


def _wkv_kernel(r_ref, k_ref, v_ref, kk_ref, aa_ref, lw_ref, rk_ref, g_ref, b_ref, o_ref, s_ref):
    @pl.when(pl.program_id(1) == 0)
    def _():
        s_ref[...] = jnp.zeros_like(s_ref)

    c2 = 2 * CHUNK
    lane = lax.broadcasted_iota(jnp.int32, (CHUNK, LANES), 1)
    m0 = lane < HEAD
    m1 = jnp.logical_not(m0)
    rows = lax.broadcasted_iota(jnp.int32, (c2, c2), 0)
    cols = lax.broadcasted_iota(jnp.int32, (c2, c2), 1)
    strict = cols < rows
    incl = cols <= rows
    eye = cols == rows
    tri = (lax.broadcasted_iota(jnp.int32, (CHUNK, CHUNK), 1)
           <= lax.broadcasted_iota(jnp.int32, (CHUNK, CHUNK), 0)).astype(BF16)
    n_chunks = r_ref.shape[0] // CHUNK

    def wave(first_chunk):
        t0s = [pl.multiple_of((first_chunk + u) * CHUNK, CHUNK) for u in range(WKV_UNROLL)]
        probs = [(t0, slice(pi * LANES, (pi + 1) * LANES)) for t0 in t0s for pi in range(N_PAIRS)]
        cat0 = lambda *xs: jnp.concatenate(xs, axis=0)
        cat1 = lambda *xs: jnp.concatenate(xs, axis=1)
        expand = lambda x: _pair_expand(x, m0, m1)
        ld = lambda ref, p: ref[pl.ds(p[0], CHUNK), p[1]].astype(F32)

        lw_all = [lw_ref[pl.ds(t0, CHUNK), :] for t0 in t0s]
        cum_all = []
        for lw_u in lw_all:
            lw_hi, lw_lo = _split_bf16(lw_u)
            cum_all.append(_dot(tri, lw_hi) + _dot(tri, lw_lo))
        lw = [lw_u[:, pi * LANES:(pi + 1) * LANES] for lw_u in lw_all for pi in range(N_PAIRS)]
        cum = [c_u[:, pi * LANES:(pi + 1) * LANES] for c_u in cum_all for pi in range(N_PAIRS)]
        cum_end = [c[CHUNK - 1:CHUNK, :] for c in cum]
        w_inv = [jnp.exp(-c) for c in cum]
        w_end = [jnp.exp(ce - c) for c, ce in zip(cum, cum_end)]
        r = [ld(r_ref, p) for p in probs]
        k = [ld(k_ref, p) for p in probs]
        kk = [ld(kk_ref, p) for p in probs]
        kk = [x / jnp.maximum(jnp.sqrt(_halves_sum(x * x, m0)), L2_EPS) for x in kk]
        b = [x * ld(aa_ref, p) for x, p in zip(kk, probs)]
        a_t = [expand(-x * jnp.exp(c - l)) for x, c, l in zip(kk, cum, lw)]
        r_t = [expand(x * jnp.exp(c)) for x, c in zip(r, cum)]
        lhs1 = [cat0(a, rr).astype(BF16) for a, rr in zip(a_t, r_t)]
        rhs1 = [cat0(expand(bb * wi), expand(kx * wi)).astype(BF16) for bb, kx, wi in zip(b, k, w_inv)]
        aa_m = [_dot_nt(x, y) for x, y in zip(lhs1, rhs1)]
        a_ab = [jnp.where(strict, m[:c2, :c2], 0.0) for m in aa_m]
        a_ak = [jnp.where(strict, m[:c2, c2:], 0.0).astype(BF16) for m in aa_m]
        lhs_top = [cat1(jnp.where(incl, m[c2:, :c2], 0.0), jnp.where(incl, m[c2:, c2:], 0.0)).astype(BF16)
                   for m in aa_m]

        t_m = [jnp.where(eye, 1.0, 0.0) + m for m in a_ab]
        a_b = [m.astype(BF16) for m in a_ab]
        a_m = [_dot(x, x) for x in a_b]
        n_steps = CHUNK.bit_length() - 1
        for step in range(1, n_steps):
            a_b = [m.astype(BF16) for m in a_m]
            if step < n_steps - 1:
                both = [_dot(x, cat1(x, t.astype(BF16))) for x, t in zip(a_b, t_m)]
                a_m = [m[:, :c2] for m in both]
                t_m = [t + m[:, c2:] for t, m in zip(t_m, both)]
            else:
                t_m = [t + _dot(x, t.astype(BF16)) for x, t in zip(a_b, t_m)]

        v = [ld(v_ref, p) for p in probs]
        v_b = [expand(x).astype(BF16) for x in v]
        akv = [_dot(x, y) for x, y in zip(a_ak, v_b)]
        tx = [_dot(t.astype(BF16), cat1(a, y).astype(BF16)) for t, a, y in zip(t_m, a_t, akv)]
        rhs2 = [cat0(x.astype(BF16), cat1(jnp.zeros((c2, c2), BF16), y)) for x, y in zip(tx, v_b)]
        lhs_bot = [cat1(expand(bb * we).T, expand(kx * we).T).astype(BF16)
                   for bb, kx, we in zip(b, k, w_end)]
        z = [_dot(cat0(x, y), w) for x, y, w in zip(lhs_top, lhs_bot, rhs2)]
        gp = [cat0(rt + m[:c2, :c2], m[c2:, :c2] + jnp.where(eye, jnp.exp(ce), 0.0)).astype(BF16)
              for rt, m, ce in zip(r_t, z, cum_end)]

        state = [s_ref[pi] for pi in range(N_PAIRS)]
        y = []
        for u in range(WKV_UNROLL):
            base = u * N_PAIRS
            zz = [_dot(gp[base + pi], state[pi].astype(BF16)) for pi in range(N_PAIRS)]
            state = [zz[pi][c2:] + z[base + pi][c2:, c2:] for pi in range(N_PAIRS)]
            y += [(zz[pi][:CHUNK] + zz[pi][CHUNK:c2])
                  + (z[base + pi][:CHUNK, c2:] + z[base + pi][CHUNK:c2, c2:]) for pi in range(N_PAIRS)]
        for pi in range(N_PAIRS):
            s_ref[pi] = state[pi]

        for i, (t0, sl) in enumerate(probs):
            mu = _halves_sum(y[i], m0) * (1.0 / HEAD)
            yc = y[i] - mu
            var = _halves_sum(yc * yc, m0) * (1.0 / HEAD)
            yn = yc * lax.rsqrt(var + LNX_EPS) * g_ref[:, sl] + b_ref[:, sl]
            bonus = _halves_sum(r[i] * k[i] * rk_ref[:, sl], m0) * v[i]
            o_ref[pl.ds(t0, CHUNK), sl] = (yn + bonus).astype(BF16)

    per_trip = WKV_UNROLL * WKV_WAVES
    assert n_chunks % per_trip == 0

    def trip(ti, carry):
        for w in range(WKV_WAVES):
            wave(ti * per_trip + w * WKV_UNROLL)
        return carry

    lax.fori_loop(0, n_chunks // per_trip, trip, 0)


def _wkv(r, k, v, kk, aa, lw, wts, l, bsz, seq, tc):
    n = r.shape[0]
    nt = seq // tc
    blk = pl.BlockSpec((tc, D), lambda b, t: (b * nt + t, 0))
    rows = [wts["r_k"], wts["lnx_g"], wts["lnx_b"]]
    return pl.pallas_call(
        _wkv_kernel,
        grid=(bsz, nt),
        in_specs=[blk] * 6 + [_layer_block(a, l) for a in rows],
        out_specs=blk,
        out_shape=jax.ShapeDtypeStruct((n, D), BF16),
        scratch_shapes=[pltpu.VMEM((N_PAIRS, LANES, LANES), F32)],
        compiler_params=_cparams(("parallel", "arbitrary")),
        name="wkv",
    )(r, k, v, kk, aa, lw, *rows)


def _attn_kernel(tiles_per_seq, x_ref, ya_ref, og_ref, yb_ref, ga_ref, gb_ref, mem_ref, gbias_ref, wa_ref,
                 wb_ref, wmix_ref, gq_ref, wq_ref, wo_ref, gkv_ref, wkv_ref, gffn_ref, rw_ref, rb_ref,
                 o_ref, hn_ref, meta_ref, pos_ref, cnt_ref, kv_ref):
    @pl.when(pl.program_id(0) % tiles_per_seq == 0)
    def _():
        mn = _rms(mem_ref[...], gkv_ref[...]).astype(BF16)
        kv_ref[...] = _dot(mn, wkv_ref[...]).astype(BF16)

    y_a = (ya_ref[...].astype(F32) * og_ref[...].astype(F32)).astype(BF16)
    g_a = _sigmoid(ga_ref[...].astype(F32) + gbias_ref[:, :D])
    g_b = _sigmoid(gb_ref[...].astype(F32) + gbias_ref[:, D:])
    merged = g_a * _dot(y_a, wa_ref[...]) + g_b * _dot(yb_ref[...], wb_ref[...])
    x1 = x_ref[...] + _dot(merged.astype(BF16), wmix_ref[...])

    hn = _rms(x1, gq_ref[...]).astype(BF16)
    q = _dot(hn, wq_ref[...]).astype(BF16)
    heads = []
    for h in range(MEM_HEADS):
        sl = slice(h * MEM_HD, (h + 1) * MEM_HD)
        s = _dot_nt(q[:, sl], kv_ref[:, sl]) * (MEM_HD ** -0.5)
        s = s - jnp.max(s, axis=-1, keepdims=True)
        e = jnp.exp(s)
        pr = (e / jnp.sum(e, axis=-1, keepdims=True)).astype(BF16)
        heads.append(_dot(pr, kv_ref[:, D + h * MEM_HD:D + (h + 1) * MEM_HD]).astype(BF16))
    o = jnp.concatenate(heads, axis=1)
    x2 = x1 + _dot(o, wo_ref[...])
    o_ref[...] = x2
    _route_tile(x2, gffn_ref, rw_ref, rb_ref, hn_ref, meta_ref, pos_ref, cnt_ref)


def _attn(x2, ya, og, yb, ga, gb, mem2, wts, l, seq):
    n = x2.shape[0]
    tm = MOE_TM
    nt = n // tm
    tps = seq // tm
    tile = pl.BlockSpec((tm, D), lambda i: (i, 0))
    params = [wts[name] for name in ("gate_b", "w_br_a", "w_br_b", "w_mix_out", "norm_memq", "wq_mem",
                                     "wo_mem", "norm_memkv", "wkv_mem", "norm_ffn", "router_wt",
                                     "router_bt")]
    return pl.pallas_call(
        functools.partial(_attn_kernel, tps),
        grid=(nt,),
        in_specs=[tile] * 6 + [pl.BlockSpec((MEM_LEN, D), lambda i: (i // tps, 0))]
        + [_layer_block(a, l) for a in params],
        out_specs=[tile, tile, pl.BlockSpec((tm, LANES), lambda i: (i, 0)),
                   pl.BlockSpec((1, SUBLANES, tm), lambda i: (i, 0, 0)),
                   pl.BlockSpec((1, SUBLANES, LANES), lambda i: (i, 0, 0))],
        out_shape=[jax.ShapeDtypeStruct((n, D), F32), jax.ShapeDtypeStruct((n, D), BF16),
                   jax.ShapeDtypeStruct((n, LANES), F32), jax.ShapeDtypeStruct((nt, SUBLANES, tm), F32),
                   jax.ShapeDtypeStruct((nt, SUBLANES, LANES), jnp.int32)],
        scratch_shapes=[pltpu.VMEM((MEM_LEN, 2 * D), BF16)],
        compiler_params=_cparams(("arbitrary",)),
        name="attn",
    )(x2, ya, og, yb, ga, gb, mem2, *params)


MOE_TM = 512
MOE_TP = MOE_TM + LANES
MOE_T = 512
RUN_ALIGN = 16
POS_ROW = N_GROUPS + N_EXPERTS
ROUTE_ROWS = -(-(POS_ROW + 1) // SUBLANES) * SUBLANES
RUN_BITS = tuple(range((MOE_TM // RUN_ALIGN).bit_length() - 1, -1, -1))


def _route_tile(x, g_ref, rw_ref, rb_ref, hn_ref, meta_ref, pos_ref, cnt_ref):
    tm = x.shape[0]
    hn = _rms(x, g_ref[...])
    hi, lo = _split_bf16(hn)
    hn_ref[...] = hi
    lt = _dot_nt(rw_ref[0], hi) + _dot_nt(rw_ref[0], lo) + _dot_nt(rw_ref[1], hi) + rb_ref[...]
    lt = lt[:ROUTE_ROWS]
    row = lax.broadcasted_iota(jnp.int32, lt.shape, 0).astype(F32)
    neg = jnp.float32(-jnp.inf)
    big = jnp.float32(1 << 20)
    is_g = row < N_GROUPS
    lg = jnp.where(is_g, lt, neg)
    gmax = jnp.max(lg, axis=0, keepdims=True)
    ge = jnp.where(is_g, jnp.exp(lg - gmax), 0.0)
    gp = ge / jnp.sum(ge, axis=0, keepdims=True)
    g_w = jnp.max(gp, axis=0, keepdims=True)
    g_i = jnp.min(jnp.where(is_g & (gp == g_w), row, big), axis=0, keepdims=True)
    first = N_GROUPS + g_i * EPG
    sel = (row >= first) & (row < first + EPG)
    le = jnp.where(sel, lt, neg)
    emax = jnp.max(le, axis=0, keepdims=True)
    ee = jnp.where(sel, jnp.exp(le - emax), 0.0)
    ep = ee / jnp.sum(ee, axis=0, keepdims=True)
    p1 = jnp.max(ep, axis=0, keepdims=True)
    i1 = jnp.min(jnp.where(sel & (ep == p1), row, big), axis=0, keepdims=True)
    rest = sel & (row != i1)
    p2 = jnp.max(jnp.where(rest, ep, -1.0), axis=0, keepdims=True)
    i2 = jnp.min(jnp.where(rest & (ep == p2), row, big), axis=0, keepdims=True)
    tot = p1 + p2
    cw = jnp.where(row == i1, g_w * (p1 / tot), 0.0) + jnp.where(row == i2, g_w * (p2 / tot), 0.0)

    grow = lax.broadcasted_iota(jnp.int32, (SUBLANES, tm), 0).astype(F32)
    onehot = jnp.where(grow == g_i, 1.0, 0.0)
    before = (lax.broadcasted_iota(jnp.int32, (tm, tm), 0)
              < lax.broadcasted_iota(jnp.int32, (tm, tm), 1)).astype(BF16)
    rank = _dot(onehot.astype(BF16), before)
    count = jnp.sum(onehot, axis=1, keepdims=True)
    run = jnp.floor((count + (RUN_ALIGN - 1)) * (1.0 / RUN_ALIGN)) * RUN_ALIGN
    gcol = lax.broadcasted_iota(jnp.int32, (SUBLANES, 1), 0)
    start = jnp.zeros((SUBLANES, 1), F32)
    acc = jnp.zeros((1, 1), F32)
    for g in range(1, N_GROUPS):
        acc = acc + run[g - 1:g, :]
        start = start + jnp.where(gcol == g, acc, 0.0)
    pos = jnp.sum(onehot * (start + rank), axis=0, keepdims=True)
    table = cw + jnp.where(row == POS_ROW, pos, 0.0)
    meta_ref[...] = jnp.concatenate([table, jnp.zeros((LANES - ROUTE_ROWS, tm), F32)], axis=0).T
    pos_ref[0] = jnp.broadcast_to(pos, (SUBLANES, tm))
    cnt_ref[0] = jnp.broadcast_to(run, (SUBLANES, LANES)).astype(jnp.int32)


def _advance_offsets(lens_ref, off_ref, tile, cap):
    @pl.when(tile == 0)
    def _():
        for g in range(N_GROUPS):
            off_ref[0, g] = g * cap

    @pl.when(tile > 0)
    def _():
        for g in range(N_GROUPS):
            off_ref[tile % 2, g] = off_ref[(tile + 1) % 2, g] + lens_ref[(tile - 1) * N_GROUPS + g]


def _run_copies(lens_ref, off_ref, tile, make_copy):
    out = []
    local = 0
    for g in range(N_GROUPS):
        length = lens_ref[tile * N_GROUPS + g]
        base = off_ref[tile % 2, g]
        for bit in RUN_BITS:
            rows = RUN_ALIGN << bit
            done = length & (-2 * rows)
            cond = (length & rows) != 0
            out.append((cond, make_copy(pl.multiple_of(local + done, RUN_ALIGN),
                                        pl.multiple_of(base + done, RUN_ALIGN), rows)))
        local = local + length
    return out


def _dispatch_kernel(cap, lens_ref, hn_ref, meta_ref, pos_ref, xs_ref, ms_ref,
                     comp_ref, compm_ref, zx_ref, zm_ref, off_ref, sem):
    i = pl.program_id(0)
    last = pl.num_programs(0) - 1
    tm = hn_ref.shape[0]
    _advance_offsets(lens_ref, off_ref, i, cap)

    def copies_of(tile):
        buf = tile % 2

        def make_copy(local, sorted_row, rows):
            return (pltpu.make_async_copy(comp_ref.at[buf, pl.ds(local, rows)],
                                          xs_ref.at[pl.ds(sorted_row, rows)], sem.at[0, buf]),
                    pltpu.make_async_copy(compm_ref.at[buf, pl.ds(local, rows)],
                                          ms_ref.at[pl.ds(sorted_row, rows)], sem.at[1, buf]))

        return _run_copies(lens_ref, off_ref, tile, make_copy)

    def wait_copies(tile):
        for cond, (cx, cm) in copies_of(tile):
            @pl.when(cond)
            def _(cx=cx, cm=cm):
                cx.wait()
                cm.wait()

    slot = lax.broadcasted_iota(jnp.int32, (MOE_TP, tm), 0).astype(F32)
    perm = jnp.where(slot == pos_ref[0][0:1, :], 1.0, 0.0).astype(BF16)
    comp_ref[i % 2] = _dot(perm, hn_ref[...]).astype(BF16)
    m = meta_ref[...]
    m1 = m.astype(BF16)
    r1 = m - m1.astype(F32)
    m2 = r1.astype(BF16)
    m3 = (r1 - m2.astype(F32)).astype(BF16)
    compm_ref[i % 2] = _dot(perm, m1) + _dot(perm, m2) + _dot(perm, m3)

    for cond, (cx, cm) in copies_of(i):
        @pl.when(cond)
        def _(cx=cx, cm=cm):
            cx.start()
            cm.start()

    @pl.when(i > 0)
    def _():
        wait_copies(i - 1)

    @pl.when(i == last)
    def _():
        wait_copies(i)
        zx_ref[...] = jnp.zeros_like(zx_ref)
        zm_ref[...] = jnp.zeros_like(zm_ref)
        fills = []
        for g in range(N_GROUPS):
            row0 = pl.multiple_of(off_ref[i % 2, g] + lens_ref[i * N_GROUPS + g], RUN_ALIGN)
            fills.append(pltpu.make_async_copy(zx_ref, xs_ref.at[pl.ds(row0, MOE_T)], sem.at[0, 0]))
            fills.append(pltpu.make_async_copy(zm_ref, ms_ref.at[pl.ds(row0, MOE_T)], sem.at[1, 0]))
        for c in fills:
            c.start()
        for c in fills:
            c.wait()


def _dispatch(hn, meta, pos, lens, cap):
    n = hn.shape[0]
    nt = n // MOE_TM
    return pl.pallas_call(
        functools.partial(_dispatch_kernel, cap),
        grid_spec=pltpu.PrefetchScalarGridSpec(
            num_scalar_prefetch=1,
            grid=(nt,),
            in_specs=[pl.BlockSpec((MOE_TM, D), lambda i, *_: (i, 0)),
                      pl.BlockSpec((MOE_TM, LANES), lambda i, *_: (i, 0)),
                      pl.BlockSpec((1, SUBLANES, MOE_TM), lambda i, *_: (i, 0, 0))],
            out_specs=[pl.BlockSpec(memory_space=pl.ANY), pl.BlockSpec(memory_space=pl.ANY)],
            scratch_shapes=[pltpu.VMEM((2, MOE_TP, D), BF16), pltpu.VMEM((2, MOE_TP, LANES), F32),
                            pltpu.VMEM((MOE_T, D), BF16), pltpu.VMEM((MOE_T, LANES), F32),
                            pltpu.SMEM((2, N_GROUPS), jnp.int32), pltpu.SemaphoreType.DMA((2, 2))]),
        out_shape=[jax.ShapeDtypeStruct((N_GROUPS * cap, D), BF16),
                   jax.ShapeDtypeStruct((N_GROUPS * cap, LANES), F32)],
        compiler_params=_cparams(("arbitrary",)),
        name="dispatch",
    )(lens, hn, meta, pos)


def _tile_group(j, tend_ref):
    jc = jnp.minimum(j, jnp.maximum(tend_ref[N_GROUPS - 1] - 1, 0))
    g = sum((jc >= tend_ref[t]).astype(jnp.int32) for t in range(N_GROUPS - 1))
    return jc, g


def _experts_kernel(tend_ref, tstart_ref, xs_ref, ms_ref, wg_ref, wu_ref, wd_ref, ys_ref, wg_s, wu_s,
                    wd_s):
    j = pl.program_id(0)

    @pl.when(j < tend_ref[N_GROUPS - 1])
    def _():
        group = _tile_group(j, tend_ref)[1]

        @pl.when(j == tstart_ref[group])
        def _():
            for e in range(EPG):
                wg_s[e] = wg_ref[e].astype(BF16)
                wu_s[e] = wu_ref[e].astype(BF16)
                wd_s[e] = wd_ref[e].astype(BF16)

        xs = xs_ref[...]
        ms = ms_ref[...]
        lane = lax.broadcasted_iota(jnp.int32, ms.shape, 1)
        first = N_GROUPS + group * EPG
        acc = None
        for e in range(EPG):
            gate = _dot(xs, wg_s[e])
            he = (gate * _sigmoid(gate) * _dot(xs, wu_s[e])).astype(BF16)
            cw = jnp.sum(jnp.where(lane == first + e, ms, 0.0), axis=-1, keepdims=True)
            term = cw * _dot(he, wd_s[e])
            acc = term if acc is None else acc + term
        ys_ref[...] = acc.astype(BF16)


def _experts(xs, ms, tend, tstart, wts, l, n_tiles, cap):
    def wmap(j, tend, tstart):
        return (l * N_GROUPS + _tile_group(j, tend)[1], 0, 0)

    def rmap(j, tend, tstart):
        jc, g = _tile_group(j, tend)
        return (g * (cap // MOE_T) + jc - tstart[g], 0)

    return pl.pallas_call(
        _experts_kernel,
        grid_spec=pltpu.PrefetchScalarGridSpec(
            num_scalar_prefetch=2,
            grid=(n_tiles,),
            in_specs=[pl.BlockSpec((MOE_T, D), rmap), pl.BlockSpec((MOE_T, LANES), rmap),
                      pl.BlockSpec((EPG, D, D_EXPERT), wmap, pipeline_mode=pl.Buffered(1)),
                      pl.BlockSpec((EPG, D, D_EXPERT), wmap, pipeline_mode=pl.Buffered(1)),
                      pl.BlockSpec((EPG, D_EXPERT, D), wmap, pipeline_mode=pl.Buffered(1))],
            out_specs=pl.BlockSpec((MOE_T, D), rmap),
            scratch_shapes=[pltpu.VMEM((EPG, D, D_EXPERT), BF16), pltpu.VMEM((EPG, D, D_EXPERT), BF16),
                            pltpu.VMEM((EPG, D_EXPERT, D), BF16)]),
        out_shape=jax.ShapeDtypeStruct(xs.shape, BF16),
        compiler_params=_cparams(("arbitrary",)),
        name="experts",
    )(tend, tstart, xs, ms, wts["w_e_gate"], wts["w_e_up"], wts["w_e_down"])


def _combine_kernel(final_norm, cap, lens_ref, x_ref, meta_ref, ys_ref, gf_ref, o_ref, comp_ref,
                    off_ref, sem):
    i = pl.program_id(0)
    tm = x_ref.shape[0]

    def copies_of(tile):
        buf = tile % 2

        def make_copy(local, sorted_row, rows):
            return pltpu.make_async_copy(ys_ref.at[pl.ds(sorted_row, rows)],
                                         comp_ref.at[buf, pl.ds(local, rows)], sem.at[buf])

        return _run_copies(lens_ref, off_ref, tile, make_copy)

    def start_copies(tile):
        _advance_offsets(lens_ref, off_ref, tile, cap)
        for cond, c in copies_of(tile):
            pl.when(cond)(c.start)

    @pl.when(i == 0)
    def _():
        comp_ref[...] = jnp.zeros_like(comp_ref)
        start_copies(i)

    @pl.when(i < pl.num_programs(0) - 1)
    def _():
        start_copies(i + 1)

    for cond, c in copies_of(i):
        pl.when(cond)(c.wait)

    lane = lax.broadcasted_iota(jnp.int32, meta_ref.shape, 1)
    pos = jnp.sum(jnp.where(lane == POS_ROW, meta_ref[...], 0.0), axis=-1, keepdims=True)
    slot = lax.broadcasted_iota(jnp.int32, (tm, MOE_TP), 1).astype(F32)
    unperm = jnp.where(slot == pos, 1.0, 0.0).astype(BF16)
    out = x_ref[...] + _dot(unperm, comp_ref[i % 2])
    if final_norm:
        out = _rms(out, gf_ref[...])
    o_ref[...] = out


def _combine(x2, meta, ys, lens, norm_f, final_norm, cap):
    n = x2.shape[0]
    return pl.pallas_call(
        functools.partial(_combine_kernel, final_norm, cap),
        grid_spec=pltpu.PrefetchScalarGridSpec(
            num_scalar_prefetch=1,
            grid=(n // MOE_TM,),
            in_specs=[pl.BlockSpec((MOE_TM, D), lambda i, *_: (i, 0)),
                      pl.BlockSpec((MOE_TM, LANES), lambda i, *_: (i, 0)),
                      pl.BlockSpec(memory_space=pl.ANY),
                      pl.BlockSpec((1, D), lambda i, *_: (0, 0))],
            out_specs=pl.BlockSpec((MOE_TM, D), lambda i, *_: (i, 0)),
            scratch_shapes=[pltpu.VMEM((2, MOE_TP, D), BF16), pltpu.SMEM((2, N_GROUPS), jnp.int32),
                            pltpu.SemaphoreType.DMA((2,))]),
        out_shape=jax.ShapeDtypeStruct((n, D), F32),
        compiler_params=_cparams(("arbitrary",)),
        name="combine",
    )(lens, x2, meta, ys, norm_f)


def _moe(x2, hn, meta, pos, cnt, wts, l, norm_f, final_norm):
    n = x2.shape[0]
    nt = n // MOE_TM
    cap = -(-(n + RUN_ALIGN * nt + MOE_T) // MOE_T) * MOE_T
    n_tiles = (n + N_GROUPS * RUN_ALIGN * nt) // MOE_T + N_GROUPS + 1

    lens = cnt[:, :N_GROUPS, 0]
    tiles = (jnp.sum(lens, axis=0) + (MOE_T - 1)) // MOE_T
    tend = jnp.cumsum(tiles).astype(jnp.int32)
    tstart = tend - tiles
    lens = lens.reshape(-1)

    xs, ms = _dispatch(hn, meta, pos, lens, cap)
    ys = _experts(xs, ms, tend, tstart, wts, l, n_tiles, cap)
    return _combine(x2, meta, ys, lens, norm_f, final_norm, cap)


def _prepare_weights(w):
    depth = w["w_in"].shape[0]
    rw = 3 * D
    n_lo = LORA_DECAY + LORA_AAA + LORA_GATE
    w_in = w["w_in"]
    vres_in = jnp.pad(w["w_vres_in"], ((1, 0), (0, 0), (0, 0)))
    lora = jnp.concatenate([w_in[:, :, rw:rw + n_lo], vres_in], axis=2)
    lora = jnp.pad(lora, ((0, 0), (0, 0), (0, LORA_W - lora.shape[2]))).astype(BF16)
    mu_lora = jnp.concatenate([w["mu_in"][:, rw:rw + n_lo], jnp.pad(w["mu_vres"], ((1, 0), (0, 0)))], axis=1)
    mu_lora = jnp.pad(mu_lora, ((0, 0), (0, LORA_W - mu_lora.shape[1])))
    c0 = rw + n_lo
    starts = [0, D, 2 * D, c0, c0 + 2 * D, c0 + D, c0 + 3 * D, c0 + 4 * D]
    w_main = jnp.stack([w_in[:, :, s:s + D] for s in starts]).astype(BF16)

    def pad_rows(a, before, total):
        return jnp.pad(a, ((0, 0), (before, total - before - a.shape[1]), (0, 0))).astype(BF16)

    router = jnp.swapaxes(jnp.concatenate([w["router_g"], w["router_e"]], axis=2), 1, 2)
    router = jnp.pad(router, ((0, 0), (0, LANES - router.shape[1]), (0, 0)))
    r_hi = router.astype(BF16)
    r_lo = (router - r_hi.astype(F32)).astype(BF16)
    router_b = jnp.concatenate([w["router_g_b"], w["router_e_b"]], axis=1)
    router_b = jnp.pad(router_b, ((0, 0), (0, LANES - router_b.shape[1])))[:, :, None]

    row = lambda a: a.reshape(a.shape[0], 1, -1)
    experts = lambda a: a.reshape(depth * N_EXPERTS, a.shape[2], a.shape[3])
    return {
        "norm_mix": row(w["norm_mix"]), "w_main": w_main, "w_lora": lora,
        "mu_rkv": row(w["mu_in"][:, :rw]), "mu_lora": row(mu_lora),
        "wup_dec": pad_rows(w["w_decay_up"], 0, LANES),
        "wup_aaa": pad_rows(w["w_aaa_up"], LORA_DECAY, LANES),
        "wup_gate": pad_rows(w["w_gate_up"], 0, 2 * LANES),
        "wup_vres": pad_rows(w["w_vres_up"], LORA_GATE, 2 * LANES), "v0": row(w["v0"]),
        "w0": row(w["w0"]), "a0": row(w["a0"]), "k_k": row(w["k_k"]), "k_a": row(w["k_a"]),
        "conv_w": w["conv_w"],
        "r_k": row(w["r_k"]), "lnx_g": row(w["lnx_g"]), "lnx_b": row(w["lnx_b"]),
        "gate_b": row(w["gate_b"]),
        "w_br_a": w["w_br_a"].astype(BF16), "w_br_b": w["w_br_b"].astype(BF16),
        "w_mix_out": w["w_mix_out"].astype(BF16),
        "norm_memq": row(w["norm_memq"]), "norm_memkv": row(w["norm_memkv"]),
        "wq_mem": w["wq_mem"].astype(BF16), "wkv_mem": w["wkv_mem"].astype(BF16),
        "wo_mem": w["wo_mem"].astype(BF16),
        "norm_ffn": row(w["norm_ffn"]),
        "router_wt": jnp.stack([r_hi, r_lo], axis=1), "router_bt": router_b,
        "w_e_gate": experts(w["w_e_gate"]), "w_e_up": experts(w["w_e_up"]),
        "w_e_down": experts(w["w_e_down"]),
    }


def _layer_block(a, l):
    index = (l,) + (0,) * (a.ndim - 1)
    return pl.BlockSpec((None,) + a.shape[1:], lambda *_: index, pipeline_mode=pl.Buffered(1))


def _tile(total, want):
    t = min(want, total)
    assert total % t == 0, (total, t)
    return t


def kernel(x, mem, norm_mix, w_in, mu_in, w_vres_in, mu_vres, w0, w_decay_up, a0, w_aaa_up, w_gate_up,
           v0, w_vres_up, k_k, k_a, r_k, lnx_g, lnx_b, conv_w, gate_b, w_br_a, w_br_b, w_mix_out,
           norm_memq, norm_memkv, wq_mem, wkv_mem, wo_mem, norm_ffn, router_g, router_g_b, router_e,
           router_e_b, w_e_gate, w_e_up, w_e_down, norm_f):
    w = dict(norm_mix=norm_mix, w_in=w_in, mu_in=mu_in, w_vres_in=w_vres_in, mu_vres=mu_vres, w0=w0,
             w_decay_up=w_decay_up, a0=a0, w_aaa_up=w_aaa_up, w_gate_up=w_gate_up, v0=v0,
             w_vres_up=w_vres_up, k_k=k_k, k_a=k_a, r_k=r_k, lnx_g=lnx_g, lnx_b=lnx_b, conv_w=conv_w,
             gate_b=gate_b, w_br_a=w_br_a, w_br_b=w_br_b, w_mix_out=w_mix_out, norm_memq=norm_memq,
             norm_memkv=norm_memkv, wq_mem=wq_mem, wkv_mem=wkv_mem, wo_mem=wo_mem, norm_ffn=norm_ffn,
             router_g=router_g, router_g_b=router_g_b, router_e=router_e, router_e_b=router_e_b,
             w_e_gate=w_e_gate, w_e_up=w_e_up, w_e_down=w_e_down)
    bsz, seq, _ = x.shape
    depth = norm_mix.shape[0]
    n = bsz * seq
    assert seq % CHUNK == 0 and mem.shape[1] == MEM_LEN
    x2 = x.reshape(n, D)
    mem2 = mem.reshape(bsz * MEM_LEN, D)
    tm_in = _tile(seq, 512)
    tc_wkv = _tile(seq, CHUNK * WKV_UNROLL * WKV_WAVES)
    assert seq % MOE_TM == 0
    norm_f2 = norm_f.reshape(1, D)

    wts = _prepare_weights(w)
    v_first = None
    for l in range(depth):
        r, k, v, kk, aa, lwd, og, yb, ga, gb = _mixin(x2, v_first, wts, l, tm_in, seq)
        if l == 0:
            v_first = v
        ya = _wkv(r, k, v, kk, aa, lwd, wts, l, bsz, seq, tc_wkv)
        x2, hn, meta, pos, cnt = _attn(x2, ya, og, yb, ga, gb, mem2, wts, l, seq)
        x2 = _moe(x2, hn, meta, pos, cnt, wts, l, norm_f2, l == depth - 1)
    return x2.reshape(bsz, seq, D)
```
